```python
import math
import jax, jax.numpy as jnp
from jax import lax
import numpy as np

D_MODEL = 1024
BATCH = 8
SEQ = 4096
DEPTH = 2

N_MIXERS = 4
GROUP_WIDTH = D_MODEL // N_MIXERS
HEAD_DIM = 64
BLOCK = 128
N_META = 16
SWA_HEADS = GROUP_WIDTH // HEAD_DIM
SWA_KV_HEADS = SWA_HEADS // 2
SWA_GROUP = SWA_HEADS // SWA_KV_HEADS
SWA_WINDOW = 128
DIFF_HEADS = GROUP_WIDTH // HEAD_DIM
DIFF_V_DIM = HEAD_DIM
DIFF_QK_DIM = HEAD_DIM // 2
RWKV_HEADS = GROUP_WIDTH // HEAD_DIM
RWKV_HEAD_SIZE = HEAD_DIM
DECAY_LORA = 64
AAA_LORA = 64
GATE_LORA = 128
CONV_CH = GROUP_WIDTH
CONV_WIDTH = 31
CONV_NORM_GROUPS = 4
REL_BUCKETS = 32
REL_MAX_DIST = 128
N_EXPERTS = 16
N_EXPERT_GROUPS = 4
EXPERTS_PER_GROUP = N_EXPERTS // N_EXPERT_GROUPS
TOP_K = 2
EXPERT_FF = 512
ALPHA = (2 * DEPTH) ** 0.25
BETA = (8 * DEPTH) ** -0.25
SWA_IN = (SWA_HEADS + 2 * SWA_KV_HEADS) * HEAD_DIM
DIFF_IN = 2 * DIFF_HEADS * 2 * DIFF_QK_DIM + DIFF_HEADS * DIFF_V_DIM
RWKV_IN = 3 * GROUP_WIDTH + DECAY_LORA + AAA_LORA + GATE_LORA
CONV_IN = 2 * CONV_CH
IN_WIDTH = SWA_IN + DIFF_IN + RWKV_IN + CONV_IN
NEG = -1e30

kernel_name = "hybrid_swa_diff_rwkv7_conformer_moe"


def _split(u, widths):
    cuts = [int(c) for c in np.cumsum(widths)[:-1]]
    return jnp.split(u, cuts, axis=-1)


def _layernorm(x, g, b, eps=1e-5):
    xf = x.astype(jnp.float32)
    mu = jnp.mean(xf, -1, keepdims=True)
    var = jnp.mean(jnp.square(xf - mu), -1, keepdims=True)
    return ((xf - mu) * lax.rsqrt(var + eps) * g + b).astype(x.dtype)


def _groupnorm(x, n_groups, g, b, eps):
    shp = x.shape
    xf = x.astype(jnp.float32).reshape(shp[:-1] + (n_groups, shp[-1] // n_groups))
    mu = jnp.mean(xf, -1, keepdims=True)
    var = jnp.mean(jnp.square(xf - mu), -1, keepdims=True)
    y = ((xf - mu) * lax.rsqrt(var + eps)).reshape(shp)
    return (y * g + b).astype(x.dtype)


def _t5_bucket(dist):
    n = jnp.maximum(dist, 0)
    max_exact = REL_BUCKETS // 2
    log_ratio = jnp.log(jnp.maximum(n, 1).astype(jnp.float32) / max_exact) / math.log(REL_MAX_DIST / max_exact)
    large = jnp.minimum(max_exact + (log_ratio * (REL_BUCKETS - max_exact)).astype(jnp.int32), REL_BUCKETS - 1)
    return jnp.where(n < max_exact, n, large)


def _swa_attention(q, k, v, sinks, rel_bias_a):
    b, l = q.shape[:2]
    pad = BLOCK - N_META
    nb = (l + pad) // BLOCK

    def blocks(t):
        t = jnp.pad(t, ((0, 0), (pad, 0)) + ((0, 0),) * (t.ndim - 2))
        return t.reshape((b, nb, BLOCK) + t.shape[2:])

    qb = blocks(q).reshape(b, nb, BLOCK, SWA_KV_HEADS, SWA_GROUP, HEAD_DIM)

    def keys_of(t):
        tb = blocks(t)
        prev = jnp.pad(tb, ((0, 0), (1, 0), (0, 0), (0, 0), (0, 0)))[:, :-1]
        meta = jnp.broadcast_to(t[:, None, :N_META], (b, nb, N_META) + t.shape[2:])
        return jnp.concatenate([meta, prev, tb], axis=2)

    kw, vw = keys_of(k), keys_of(v)
    n_keys = N_META + 2 * BLOCK
    qpos = jnp.arange(nb * BLOCK).reshape(nb, BLOCK) - pad
    kpos = jnp.concatenate([jnp.broadcast_to(jnp.arange(N_META), (nb, N_META)), qpos - BLOCK, qpos], axis=1)
    dq = qpos[:, :, None] - kpos[:, None, :]
    is_meta_slot = (jnp.arange(n_keys) < N_META)[None, None, :]
    band_ok = (kpos[:, None, :] >= N_META) & (dq < SWA_WINDOW)
    mask = (dq >= 0) & (is_meta_slot | band_ok)
    bias = jnp.moveaxis(rel_bias_a[_t5_bucket(dq)], -1, 1)
    bias = bias.reshape(nb, SWA_KV_HEADS, SWA_GROUP, BLOCK, n_keys).astype(jnp.float32)
    s = jnp.einsum('bnqhgd,bnkhd->bnhgqk', qb, kw).astype(jnp.float32) * (HEAD_DIM ** -0.5) + bias[None]
    s = jnp.where(mask[None, :, None, None], s, NEG)
    sink = jnp.broadcast_to(sinks.astype(jnp.float32).reshape(1, 1, SWA_KV_HEADS, SWA_GROUP, 1, 1), s.shape[:-1] + (1,))
    p = jax.nn.softmax(jnp.concatenate([s, sink], axis=-1), axis=-1)[..., :-1]
    o = jnp.einsum('bnhgqk,bnkhd->bnqhgd', p.astype(v.dtype), vw)
    return o.reshape(b, nb * BLOCK, SWA_HEADS * HEAD_DIM)[:, pad:]


def _diff_attention(q, k, v, lam, lam_init, subln_g, rel_bias_b):
    b, l = q.shape[:2]
    pad = BLOCK - N_META
    lp = l + pad
    nb = lp // BLOCK
    padt = lambda t: jnp.pad(t, ((0, 0), (pad, 0)) + ((0, 0),) * (t.ndim - 2))
    qp, kp, vp = padt(q), padt(k), padt(v)
    qblocks = jnp.moveaxis(qp.reshape((b, nb, BLOCK) + q.shape[2:]), 1, 0)
    pos = jnp.arange(lp) - pad
    qpos_blocks = pos.reshape(nb, BLOCK)
    scale = DIFF_QK_DIM ** -0.5

    def one_block(args):
        qblk, qpos = args
        s = jnp.einsum('bqhcd,bkhcd->bhcqk', qblk, kp).astype(jnp.float32) * scale
        dq = qpos[:, None] - pos[None, :]
        bias = jnp.moveaxis(rel_bias_b[_t5_bucket(dq)], -1, 0).astype(jnp.float32)
        s = s + bias[None, :, None]
        mask = (dq >= 0) & (pos >= 0)[None, :]
        p = jax.nn.softmax(jnp.where(mask, s, NEG), axis=-1)
        attn = p[:, :, 0] - lam * p[:, :, 1]
        return jnp.einsum('bhqk,bkhd->bqhd', attn.astype(v.dtype), vp)

    o = lax.map(one_block, (qblocks, qpos_blocks))
    o = jnp.moveaxis(o, 0, 1).reshape(b, lp, DIFF_HEADS, DIFF_V_DIM)[:, pad:]
    of = o.astype(jnp.float32)
    of = of * lax.rsqrt(jnp.mean(jnp.square(of), -1, keepdims=True) + 1e-5) * subln_g * (1.0 - lam_init)
    return of.reshape(b, l, DIFF_HEADS * DIFF_V_DIM).astype(v.dtype)


def _rwkv7_mix(uc, mu, w0, w2, a0, a2, g2, k_k, k_a, r_k, lnx_g, lnx_b):
    b, l, _ = uc.shape
    f32 = jnp.float32
    prev = jnp.pad(uc, ((0, 0), (1, 0), (0, 0)))[:, :-1]
    uc = uc + mu * (prev - uc)
    r, k, v, xw, xa, xg = _split(uc, (GROUP_WIDTH, GROUP_WIDTH, GROUP_WIDTH, DECAY_LORA, AAA_LORA, GATE_LORA))
    w = jnp.exp(-math.exp(-0.5) * jax.nn.sigmoid((w0 + jnp.tanh(xw) @ w2).astype(f32)))
    a = jax.nn.sigmoid((a0 + xa @ a2).astype(f32))
    g = jax.nn.sigmoid(xg) @ g2
    k = k.astype(f32)
    heads = lambda t: t.reshape(b, l, RWKV_HEADS, RWKV_HEAD_SIZE)
    kk = heads(k * k_k)
    kk = kk / jnp.maximum(jnp.sqrt(jnp.sum(jnp.square(kk), -1, keepdims=True)), 1e-12)
    k = k * (1.0 + (a - 1.0) * k_a)
    r_h, w_h, k_h, v_h, a_h = heads(r.astype(f32)), heads(w), heads(k), heads(v.astype(f32)), heads(a)

    def step(S, inp):
        r_t, w_t, k_t, v_t, kk_t, a_t = inp
        sa = jnp.einsum('bhij,bhj->bhi', S, -kk_t)
        S = S * w_t[:, :, None, :] + sa[..., None] * (kk_t * a_t)[:, :, None, :] + v_t[..., None] * k_t[:, :, None, :]
        return S, jnp.einsum('bhij,bhj->bhi', S, r_t)

    S0 = jnp.zeros((b, RWKV_HEADS, RWKV_HEAD_SIZE, RWKV_HEAD_SIZE), f32)
    xs = tuple(jnp.moveaxis(t, 1, 0) for t in (r_h, w_h, k_h, v_h, kk, a_h))
    _, ys = lax.scan(step, S0, xs)
    y = jnp.moveaxis(ys, 0, 1).reshape(b, l, GROUP_WIDTH)
    y = _groupnorm(y, RWKV_HEADS, lnx_g, lnx_b, 64e-5)
    bonus = (jnp.sum(r_h * k_h * r_k, -1, keepdims=True) * v_h).reshape(b, l, GROUP_WIDTH)
    return ((y + bonus) * g).astype(uc.dtype)


def _conv_module(ud, conv_w, conv_b, gn_g, gn_b):
    a, gate = jnp.split(ud, 2, axis=-1)
    h = a * jax.nn.sigmoid(gate)
    h = lax.conv_general_dilated(h, conv_w[:, None, :], window_strides=(1,),
                                 padding=[(CONV_WIDTH - 1, 0)],
                                 dimension_numbers=('NWC', 'WIO', 'NWC'),
                                 feature_group_count=CONV_CH) + conv_b
    h = _groupnorm(h, CONV_NORM_GROUPS, gn_g, gn_b, 1e-5)
    return jax.nn.silu(h)


def _moe(h, router_w, router_b, w1, w3, w2):
    b, l, d = h.shape
    n_tok = b * l
    t = h.reshape(n_tok, d)
    scores = jax.nn.sigmoid((t @ router_w).astype(jnp.float32))
    biased = scores + router_b.astype(jnp.float32)
    grp_top = lax.top_k(biased.reshape(n_tok, N_EXPERT_GROUPS, EXPERTS_PER_GROUP), TOP_K)[0]
    best_group = jnp.argmax(jnp.sum(grp_top, -1), axis=-1)
    in_group = jnp.repeat(jnp.arange(N_EXPERT_GROUPS)[None, :] == best_group[:, None], EXPERTS_PER_GROUP, axis=1)
    _, top_idx = lax.top_k(jnp.where(in_group, biased, NEG), TOP_K)
    top_s = jnp.take_along_axis(scores, top_idx, axis=-1)
    gates = top_s / jnp.sum(top_s, -1, keepdims=True)
    combine = jnp.sum(jax.nn.one_hot(top_idx, N_EXPERTS, dtype=jnp.float32) * gates[..., None], axis=1)
    out = jnp.zeros((n_tok, d), jnp.float32)
    for e in range(N_EXPERTS):
        act = jax.nn.silu(t @ w1[e]) * (t @ w3[e])
        out = out + combine[:, e:e + 1] * (act @ w2[e])
    return out.astype(h.dtype).reshape(b, l, d)


def setup_inputs(seed: int = 0) -> dict:
    key = jax.random.key(seed)
    ks = jax.random.split(key, 40)
    n = lambda i, shp, s=1.0: jax.random.normal(ks[i], shp, jnp.float32) * s
    L = DEPTH
    gw = GROUP_WIDTH
    return {
        "x": n(0, (BATCH, SEQ, D_MODEL)),
        "meta": n(1, (N_META, D_MODEL)),
        "ln0_g": 1.0 + n(2, (D_MODEL,), 0.02),
        "ln0_b": n(3, (D_MODEL,), 0.02),
        "w_in": n(4, (L, D_MODEL, IN_WIDTH), D_MODEL ** -0.5),
        "swa_sinks": n(5, (L, SWA_HEADS), 0.5),
        "rel_bias": n(6, (REL_BUCKETS, SWA_HEADS + DIFF_HEADS), 0.5),
        "diff_lq1": n(7, (L, DIFF_QK_DIM), 0.1),
        "diff_lk1": n(8, (L, DIFF_QK_DIM), 0.1),
        "diff_lq2": n(9, (L, DIFF_QK_DIM), 0.1),
        "diff_lk2": n(10, (L, DIFF_QK_DIM), 0.1),
        "diff_subln_g": 1.0 + n(11, (L, DIFF_V_DIM), 0.02),
        "rwkv_mu": jax.random.uniform(ks[12], (L, RWKV_IN), jnp.float32),
        "rwkv_w0": n(13, (L, gw)),
        "rwkv_w2": n(14, (L, DECAY_LORA, gw), DECAY_LORA ** -0.5),
        "rwkv_a0": n(15, (L, gw), 0.5),
        "rwkv_a2": n(16, (L, AAA_LORA, gw), AAA_LORA ** -0.5),
        "rwkv_g2": n(17, (L, GATE_LORA, gw), GATE_LORA ** -0.5),
        "rwkv_kk": 0.85 + n(18, (L, gw), 0.05),
        "rwkv_ka": 1.0 + n(19, (L, gw), 0.05),
        "rwkv_rk": n(20, (L, RWKV_HEADS, RWKV_HEAD_SIZE), 0.1),
        "rwkv_lnx_g": 1.0 + n(21, (L, gw), 0.02),
        "rwkv_lnx_b": n(22, (L, gw), 0.02),
        "conv_w": n(23, (L, CONV_WIDTH, CONV_CH), CONV_WIDTH ** -0.5),
        "conv_b": n(24, (L, CONV_CH), 0.02),
        "conv_gn_g": 1.0 + n(25, (L, CONV_CH), 0.02),
        "conv_gn_b": n(26, (L, CONV_CH), 0.02),
        "w_out": n(27, (L, D_MODEL, D_MODEL), BETA * D_MODEL ** -0.5),
        "ln1_g": 1.0 + n(28, (L, D_MODEL), 0.02),
        "ln1_b": n(29, (L, D_MODEL), 0.02),
        "router_w": n(30, (D_MODEL, N_EXPERTS), D_MODEL ** -0.5),
        "router_b": n(31, (N_EXPERTS,), 0.01),
        "exp_w1": n(32, (L, N_EXPERTS, D_MODEL, EXPERT_FF), D_MODEL ** -0.5),
        "exp_w3": n(33, (L, N_EXPERTS, D_MODEL, EXPERT_FF), D_MODEL ** -0.5),
        "exp_w2": n(34, (L, N_EXPERTS, EXPERT_FF, D_MODEL), BETA * EXPERT_FF ** -0.5),
        "ln2_g": 1.0 + n(35, (L, D_MODEL), 0.02),
        "ln2_b": n(36, (L, D_MODEL), 0.02),
    }


def reference(x, meta, ln0_g, ln0_b, w_in, swa_sinks, rel_bias, diff_lq1, diff_lk1, diff_lq2, diff_lk2,
              diff_subln_g, rwkv_mu, rwkv_w0, rwkv_w2, rwkv_a0, rwkv_a2, rwkv_g2, rwkv_kk, rwkv_ka, rwkv_rk,
              rwkv_lnx_g, rwkv_lnx_b, conv_w, conv_b, conv_gn_g, conv_gn_b, w_out, ln1_g, ln1_b,
              router_w, router_b, exp_w1, exp_w3, exp_w2, ln2_g, ln2_b):
    b = x.shape[0]
    f32 = jnp.float32
    h = jnp.concatenate([jnp.broadcast_to(meta[None].astype(x.dtype), (b, N_META, D_MODEL)), x], axis=1)
    h = _layernorm(h, ln0_g, ln0_b)
    lt = h.shape[1]
    rel_a = rel_bias[:, :SWA_HEADS]
    rel_b = rel_bias[:, SWA_HEADS:]
    for l in range(DEPTH):
        u = h @ w_in[l]
        ua, ub, uc, ud = _split(u, (SWA_IN, DIFF_IN, RWKV_IN, CONV_IN))
        qa, ka, va = _split(ua, (SWA_HEADS * HEAD_DIM, SWA_KV_HEADS * HEAD_DIM, SWA_KV_HEADS * HEAD_DIM))
        ya = _swa_attention(qa.reshape(b, lt, SWA_HEADS, HEAD_DIM),
                            ka.reshape(b, lt, SWA_KV_HEADS, HEAD_DIM),
                            va.reshape(b, lt, SWA_KV_HEADS, HEAD_DIM), swa_sinks[l], rel_a)
        qd, kd, vd = _split(ub, (DIFF_HEADS * 2 * DIFF_QK_DIM, DIFF_HEADS * 2 * DIFF_QK_DIM, DIFF_HEADS * DIFF_V_DIM))
        lam_init = 0.8 - 0.6 * math.exp(-0.3 * l)
        lam = (jnp.exp(jnp.sum(diff_lq1[l].astype(f32) * diff_lk1[l].astype(f32)))
               - jnp.exp(jnp.sum(diff_lq2[l].astype(f32) * diff_lk2[l].astype(f32))) + lam_init)
        yb = _diff_attention(qd.reshape(b, lt, DIFF_HEADS, 2, DIFF_QK_DIM),
                             kd.reshape(b, lt, DIFF_HEADS, 2, DIFF_QK_DIM),
                             vd.reshape(b, lt, DIFF_HEADS, DIFF_V_DIM), lam, lam_init, diff_subln_g[l], rel_b)
        yc = _rwkv7_mix(uc, rwkv_mu[l], rwkv_w0[l], rwkv_w2[l], rwkv_a0[l], rwkv_a2[l], rwkv_g2[l],
                        rwkv_kk[l], rwkv_ka[l], rwkv_rk[l], rwkv_lnx_g[l], rwkv_lnx_b[l])
        yd = _conv_module(ud, conv_w[l], conv_b[l], conv_gn_g[l], conv_gn_b[l])
        mix = jnp.concatenate([ya, yb.astype(ya.dtype), yc.astype(ya.dtype), yd.astype(ya.dtype)], axis=-1) @ w_out[l]
        h = _layernorm(ALPHA * h + mix, ln1_g[l], ln1_b[l])
        h = _layernorm(ALPHA * h + _moe(h, router_w, router_b, exp_w1[l], exp_w3[l], exp_w2[l]), ln2_g[l], ln2_b[l])
    return h[:, N_META:]
```

```python
import functools
import math

import jax
import jax.numpy as jnp
import numpy as np
from jax import lax
from jax.experimental import pallas as pl
from jax.experimental.pallas import tpu as pltpu

F32 = jnp.float32
BF16 = jnp.bfloat16
HI = lax.Precision.HIGHEST

D_MODEL = 1024
N_META = 16
BLK = 128
PAD = BLK - N_META
HEAD_DIM = 64
GROUP_WIDTH = 256
N_HEADS = 4
SWA_KV_HEADS = 2
DIFF_QK = 32
DECAY_LORA = 64
AAA_LORA = 64
GATE_LORA = 128
CONV_WIDTH = 31
REL_BUCKETS = 32
REL_MAX_DIST = 128
N_EXPERTS = 16
N_GROUPS = 4
EXPERT_FF = 512
DEPTH = 2
ALPHA = (2 * DEPTH) ** 0.25
SWA_IN = 512
DIFF_IN = 768
RWKV_IN = 1024
CONV_IN = 512
IN_WIDTH = SWA_IN + DIFF_IN + RWKV_IN + CONV_IN
NEG = -1e30
CHUNK = 64
VMEM_LIMIT = 56 * 1024 * 1024


def _dot(a, b, prec=None):
    return jnp.dot(a, b, preferred_element_type=F32, precision=prec)


def _dot_nt(a, b, prec=None):
    return lax.dot_general(a, b, (((1,), (1,)), ((), ())), preferred_element_type=F32, precision=prec)


def _dot_tn(a, b, prec=None):
    return lax.dot_general(a, b, (((0,), (0,)), ((), ())), preferred_element_type=F32, precision=prec)


def _ln(x, g, b, eps=1e-5):
    mu = jnp.mean(x, -1, keepdims=True)
    xc = x - mu
    var = jnp.mean(xc * xc, -1, keepdims=True)
    return xc * lax.rsqrt(var + eps) * g + b


def _sigmoid(x):
    return 1.0 / (1.0 + jnp.exp(-x))


def _pick_tile(n, candidates):
    for c in candidates:
        if n % c == 0:
            return c
    raise ValueError(f"no tile in {candidates} divides {n}")


def _params(*sem):
    return pltpu.CompilerParams(dimension_semantics=sem, vmem_limit_bytes=VMEM_LIMIT)


def _t5_bucket(dist):
    n = jnp.maximum(dist, 0)
    max_exact = REL_BUCKETS // 2
    log_ratio = jnp.log(jnp.maximum(n, 1).astype(F32) / max_exact) / math.log(REL_MAX_DIST / max_exact)
    large = jnp.minimum(max_exact + (log_ratio * (REL_BUCKETS - max_exact)).astype(jnp.int32), REL_BUCKETS - 1)
    return jnp.where(n < max_exact, n, large)


def _gather_bias(bkt, rel_ref, h, offset=0.0):
    acc = jnp.zeros(bkt.shape, F32)
    for b in range(REL_BUCKETS):
        acc = jnp.where(bkt == b, rel_ref[b, h] - offset, acc)
    return acc


def _prep_kernel(x_ref, meta_ref, g_ref, b_ref, o_ref):
    j = pl.program_id(1)

    @pl.when(j == 0)
    def _():
        o_ref[0, 0:PAD, :] = jnp.zeros((PAD, D_MODEL), F32)
        o_ref[0, PAD:BLK, :] = _ln(meta_ref[...], g_ref[...], b_ref[...])

    @pl.when(j > 0)
    def _():
        o_ref[0] = _ln(x_ref[0], g_ref[...], b_ref[...])


def _prep(x, meta, g, b):
    bsz, seq, d = x.shape
    nb = seq // BLK + 1
    return pl.pallas_call(
        _prep_kernel,
        grid=(bsz, nb),
        in_specs=[
            pl.BlockSpec((1, BLK, d), lambda i, j: (i, jnp.maximum(j - 1, 0), 0)),
            pl.BlockSpec((N_META, d), lambda i, j: (0, 0)),
            pl.BlockSpec((1, d), lambda i, j: (0, 0)),
            pl.BlockSpec((1, d), lambda i, j: (0, 0)),
        ],
        out_specs=pl.BlockSpec((1, BLK, d), lambda i, j: (i, j, 0)),
        out_shape=jax.ShapeDtypeStruct((bsz, nb * BLK, d), F32),
        compiler_params=_params("parallel", "parallel"),
        name="prep_ln",
    )(x, meta, g.reshape(1, d), b.reshape(1, d))


def _inproj_kernel(h_ref, w_ref, wvt_ref, ua_ref, qd_ref, kd_ref, vdt_ref, uc_ref, ud_ref, *, tm):
    j = pl.program_id(1)
    hb = h_ref[0].astype(BF16)
    row_ok = lax.broadcasted_iota(jnp.int32, (tm, 1), 0) + j * tm >= PAD
    col_ok = lax.broadcasted_iota(jnp.int32, (1, tm), 1) + j * tm >= PAD

    def proj(c0, c1):
        return jnp.where(row_ok, _dot(hb, w_ref[:, c0:c1]), 0.0)

    ua_ref[0, :, 0:256] = (proj(0, 256) * (HEAD_DIM ** -0.5)).astype(BF16)
    ua_ref[0, :, 256:512] = proj(256, 512).astype(BF16)
    qd_ref[0] = (proj(512, 768) * (DIFF_QK ** -0.5)).astype(BF16)
    kd_ref[0] = proj(768, 1024).astype(BF16)
    vt = jnp.where(col_ok, _dot_nt(wvt_ref[...], hb), 0.0).astype(BF16)
    for t in range(tm // BLK):
        vdt_ref[0, t] = vt[:, t * BLK:(t + 1) * BLK]
    uc_ref[0] = proj(1280, 2304)
    ud_ref[0] = proj(2304, 2816)


def _inproj(h, w_bf, wvt_bf):
    bsz, lp, d = h.shape
    tm = _pick_tile(lp, (384, 256, 128))
    nt = lp // tm
    row = lambda w, dt: jax.ShapeDtypeStruct((bsz, lp, w), dt)
    spec = lambda w: pl.BlockSpec((1, tm, w), lambda i, j: (i, j, 0))
    return pl.pallas_call(
        functools.partial(_inproj_kernel, tm=tm),
        grid=(bsz, nt),
        in_specs=[
            spec(d),
            pl.BlockSpec((d, IN_WIDTH), lambda i, j: (0, 0)),
            pl.BlockSpec((GROUP_WIDTH, d), lambda i, j: (0, 0)),
        ],
        out_specs=[
            spec(512), spec(256), spec(256),
            pl.BlockSpec((1, tm // BLK, GROUP_WIDTH, BLK), lambda i, j: (i, j, 0, 0)),
            spec(1024), spec(512),
        ],
        out_shape=[
            row(512, BF16), row(256, BF16), row(256, BF16),
            jax.ShapeDtypeStruct((bsz, lp // BLK, GROUP_WIDTH, BLK), BF16),
            row(1024, F32), row(512, F32),
        ],
        compiler_params=_params("parallel", "parallel"),
        name="in_proj",
    )(h, w_bf, wvt_bf)


def _swa_kernel(rel_ref, sink_ref, bband_ref, bmeta_ref, cur_ref, prev_ref, meta_ref, o_ref, band_sc, metab_sc):
    j = pl.program_id(1)

    @pl.when(j == 0)
    def _():
        for h in range(N_HEADS):
            band_sc[h] = _gather_bias(bband_ref[...], rel_ref, h)

    @pl.when(j <= 2)
    def _():
        bkt = bmeta_ref[jnp.minimum(j, 2)]
        for h in range(N_HEADS):
            metab_sc[h] = _gather_bias(bkt, rel_ref, h)

    r = lax.broadcasted_iota(jnp.int32, (BLK, 1), 0)
    qpos = j * BLK + r - PAD
    ok_meta = qpos >= lax.broadcasted_iota(jnp.int32, (1, N_META), 1)
    dq = r - lax.broadcasted_iota(jnp.int32, (1, 2 * BLK), 1) + BLK
    ok_band = (dq >= 0) & (dq < BLK) & (qpos - dq >= N_META)

    cur = cur_ref[0]
    prev = prev_ref[0]
    meta = meta_ref[0, PAD:BLK, :]
    outs = []
    for h in range(N_HEADS):
        g = h // (N_HEADS // SWA_KV_HEADS)
        kc, vc = 256 + HEAD_DIM * g, 384 + HEAD_DIM * g
        q = cur[:, HEAD_DIM * h:HEAD_DIM * (h + 1)]
        s_m = _dot_nt(q, meta[:, kc:kc + HEAD_DIM]) + metab_sc[h]
        s_p = _dot_nt(q, prev[:, kc:kc + HEAD_DIM]) + band_sc[h, :, 0:BLK]
        s_c = _dot_nt(q, cur[:, kc:kc + HEAD_DIM]) + band_sc[h, :, BLK:2 * BLK]
        s_m = jnp.where(ok_meta, s_m, NEG)
        s_p = jnp.where(ok_band[:, 0:BLK], s_p, NEG)
        s_c = jnp.where(ok_band[:, BLK:2 * BLK], s_c, NEG)
        sink = sink_ref[h]
        m = jnp.maximum(jnp.maximum(jnp.max(s_m, -1, keepdims=True), jnp.max(s_p, -1, keepdims=True)),
                        jnp.maximum(jnp.max(s_c, -1, keepdims=True), sink))
        p_m, p_p, p_c = jnp.exp(s_m - m), jnp.exp(s_p - m), jnp.exp(s_c - m)
        den = (jnp.sum(p_m, -1, keepdims=True) + jnp.sum(p_p, -1, keepdims=True)
               + jnp.sum(p_c, -1, keepdims=True) + jnp.exp(sink - m))
        o = (_dot(p_m.astype(BF16), meta[:, vc:vc + HEAD_DIM]) + _dot(p_p.astype(BF16), prev[:, vc:vc + HEAD_DIM])
             + _dot(p_c.astype(BF16), cur[:, vc:vc + HEAD_DIM]))
        outs.append(o / den)
    o_ref[0] = jnp.concatenate(outs, axis=1).astype(BF16)


def _swa(ua, rel_a, sinks, bkt_band, bkt_meta):
    bsz, lp, _ = ua.shape
    nb = lp // BLK
    smem = pl.BlockSpec(memory_space=pltpu.SMEM)
    return pl.pallas_call(
        _swa_kernel,
        grid=(bsz, nb),
        in_specs=[
            smem, smem,
            pl.BlockSpec((BLK, 2 * BLK), lambda i, j: (0, 0)),
            pl.BlockSpec((3, BLK, N_META), lambda i, j: (0, 0, 0)),
            pl.BlockSpec((1, BLK, SWA_IN), lambda i, j: (i, j, 0)),
            pl.BlockSpec((1, BLK, SWA_IN), lambda i, j: (i, jnp.maximum(j - 1, 0), 0)),
            pl.BlockSpec((1, BLK, SWA_IN), lambda i, j: (i, 0, 0)),
        ],
        out_specs=pl.BlockSpec((1, BLK, GROUP_WIDTH), lambda i, j: (i, j, 0)),
        out_shape=jax.ShapeDtypeStruct((bsz, lp, GROUP_WIDTH), BF16),
        scratch_shapes=[pltpu.VMEM((N_HEADS, BLK, 2 * BLK), F32), pltpu.VMEM((N_HEADS, BLK, N_META), F32)],
        compiler_params=_params("arbitrary", "arbitrary"),
        name="swa_attn",
    )(rel_a, sinks, bkt_band, bkt_meta, ua, ua, ua)


def _diff_kernel(lam_ref, rel_ref, bkt_ref, g_ref, q_ref, k_ref, vt_ref, o_ref, wt_sc, bias_sc, m_sc, l_sc, acc_sc):
    i = pl.program_id(1)

    @pl.when(i == 0)
    def _():
        for h in range(N_HEADS):
            for kind in range(2):
                bias_sc[h, kind] = _gather_bias(bkt_ref[kind], rel_ref, h, rel_ref[REL_BUCKETS - 1, h])

    q = q_ref[0]
    lane_grp = lax.broadcasted_iota(jnp.int32, (1, GROUP_WIDTH), 1) // DIFF_QK
    zero = jnp.zeros_like(q)
    for h in range(N_HEADS):
        wt_sc[h, 0:BLK, :] = jnp.where(lane_grp == 2 * h, q, zero)
        wt_sc[h, BLK:2 * BLK, :] = jnp.where(lane_grp == 2 * h + 1, q, zero)
    m_sc[...] = jnp.full(m_sc.shape, NEG, F32)
    l_sc[...] = jnp.zeros(l_sc.shape, F32)
    acc_sc[...] = jnp.zeros(acc_sc.shape, F32)

    krow = lax.broadcasted_iota(jnp.int32, (BLK, 1), 0)
    qcol = lax.broadcasted_iota(jnp.int32, (1, 2 * BLK), 1) % BLK
    causal = qcol >= krow
    key_ok = krow >= PAD

    def tile(j, kind, mask):
        kt = k_ref[0, pl.ds(pl.multiple_of(j * BLK, BLK), BLK), :]
        vt = vt_ref[0, j]
        for h in range(N_HEADS):
            s = _dot_nt(kt, wt_sc[h])
            if kind is not None:
                s = s + bias_sc[h, kind]
            if mask is not None:
                s = jnp.where(mask, s, NEG)
            m_old = m_sc[h]
            m_new = jnp.maximum(m_old, jnp.max(s, 0, keepdims=True))
            a = jnp.exp(m_old - m_new)
            p = jnp.exp(s - m_new)
            l_sc[h] = a * l_sc[h] + jnp.sum(p, 0, keepdims=True)
            pv = _dot(vt[HEAD_DIM * h:HEAD_DIM * (h + 1), :], p.astype(BF16))
            acc_sc[h] = a * acc_sc[h] + pv
            m_sc[h] = m_new

    @pl.when(i >= 2)
    def _():
        tile(0, None, key_ok)

    def far(j, c):
        tile(j, None, None)
        return c

    lax.fori_loop(1, i - 1, far, 0)

    @pl.when(i == 1)
    def _():
        tile(0, 1, key_ok)

    @pl.when(i >= 2)
    def _():
        tile(i - 1, 1, None)

    @pl.when(i == 0)
    def _():
        tile(0, 0, causal & key_ok)

    @pl.when(i >= 1)
    def _():
        tile(i, 0, causal)

    lam = lam_ref[0]
    post = lam_ref[1]
    outs = []
    for h in range(N_HEADS):
        o = acc_sc[h] / l_sc[h]
        o = o[:, 0:BLK] - lam * o[:, BLK:2 * BLK]
        ms = jnp.mean(o * o, 0, keepdims=True)
        outs.append(o * lax.rsqrt(ms + 1e-5) * g_ref[...] * post)
    ot = jnp.concatenate(outs, axis=0).astype(BF16)
    eye = (lax.broadcasted_iota(jnp.int32, (BLK, BLK), 0) == lax.broadcasted_iota(jnp.int32, (BLK, BLK), 1))
    o_ref[0] = _dot_nt(eye.astype(BF16), ot).astype(BF16)


def _diff(qd, kd, vdt, lam2, rel_b, bkt_d, g_b):
    bsz, lp, _ = qd.shape
    nb = lp // BLK
    smem = pl.BlockSpec(memory_space=pltpu.SMEM)
    return pl.pallas_call(
        _diff_kernel,
        grid=(bsz, nb),
        in_specs=[
            smem, smem,
            pl.BlockSpec((2, BLK, 2 * BLK), lambda b, i: (0, 0, 0)),
            pl.BlockSpec((HEAD_DIM, BLK), lambda b, i: (0, 0)),
            pl.BlockSpec((1, BLK, GROUP_WIDTH), lambda b, i: (b, i, 0)),
            pl.BlockSpec((1, lp, GROUP_WIDTH), lambda b, i: (b, 0, 0)),
            pl.BlockSpec((1, nb, GROUP_WIDTH, BLK), lambda b, i: (b, 0, 0, 0)),
        ],
        out_specs=pl.BlockSpec((1, BLK, GROUP_WIDTH), lambda b, i: (b, i, 0)),
        out_shape=jax.ShapeDtypeStruct((bsz, lp, GROUP_WIDTH), BF16),
        scratch_shapes=[
            pltpu.VMEM((N_HEADS, 2 * BLK, GROUP_WIDTH), BF16),
            pltpu.VMEM((N_HEADS, 2, BLK, 2 * BLK), F32),
            pltpu.VMEM((N_HEADS, 1, 2 * BLK), F32),
            pltpu.VMEM((N_HEADS, 1, 2 * BLK), F32),
            pltpu.VMEM((N_HEADS, HEAD_DIM, 2 * BLK), F32),
        ],
        compiler_params=_params("arbitrary", "arbitrary"),
        name="diff_attn",
    )(lam2, rel_b, bkt_d, g_b, qd, kd, vdt)


def _rwkv_kernel(x_ref, xp_ref, mu_ref, w0_ref, w2_ref, a0_ref, a2_ref, g2_ref, kk_ref, ka_ref, rk_ref,
                 lg_ref, lb_ref, o_ref, zt_sc):
    c = pl.program_id(1)
    n = CHUNK
    gw = GROUP_WIDTH

    @pl.when(c == 0)
    def _():
        zt_sc[...] = jnp.zeros(zt_sc.shape, F32)

    x = x_ref[0]
    row = lax.broadcasted_iota(jnp.int32, (n, 1), 0)
    last_prev = jnp.where(c == 0, 0.0, xp_ref[0, 7:8, :])
    xs = jnp.where(row == 0, last_prev, pltpu.roll(x, 1, 0))
    xm = x + mu_ref[...] * (xs - x)
    r, k, v = xm[:, 0:gw], xm[:, gw:2 * gw], xm[:, 2 * gw:3 * gw]
    xw = xm[:, 768:768 + DECAY_LORA]
    xa = xm[:, 832:832 + AAA_LORA]
    xg = xm[:, 896:896 + GATE_LORA]
    logw = -math.exp(-0.5) * _sigmoid(w0_ref[...] + _dot(jnp.tanh(xw), w2_ref[...], HI))
    a = _sigmoid(a0_ref[...] + _dot(xa, a2_ref[...], HI))
    g = _dot(_sigmoid(xg), g2_ref[...], HI)

    lane_head = lax.broadcasted_iota(jnp.int32, (1, gw), 1) // HEAD_DIM
    sub_head = lax.broadcasted_iota(jnp.int32, (gw, 1), 0) // HEAD_DIM
    head_ones = (sub_head == lane_head).astype(F32)

    kk = k * kk_ref[...]
    kk = kk / jnp.maximum(jnp.sqrt(_dot(kk * kk, head_ones, HI)), 1e-12)
    k2 = k * (1.0 + (a - 1.0) * ka_ref[...])
    bvec = kk * a

    tri = (lax.broadcasted_iota(jnp.int32, (n, n), 0) >= lax.broadcasted_iota(jnp.int32, (n, n), 1)).astype(F32)
    cum = _dot(tri, logw, HI)
    cl = cum[n - 1:n, :]
    e_neg = jnp.exp(-cum)
    e_last = jnp.exp(cl - cum)
    a_t = -kk * jnp.exp(cum - logw)
    r_t = r * jnp.exp(cum)
    b_t, k_t = bvec * e_neg, k2 * e_neg
    b_h, k_h = bvec * e_last, k2 * e_last

    def stack(t):
        return jnp.concatenate([jnp.where(lane_head == h, t, 0.0) for h in range(N_HEADS)], axis=0)

    ast, rst, bst, kst, bhs, khs, vst = (stack(t) for t in (a_t, r_t, b_t, k_t, b_h, k_h, v))
    gmat = _dot_nt(jnp.concatenate([ast, rst], axis=0), jnp.concatenate([bst, kst], axis=0), HI)
    rr = lax.broadcasted_iota(jnp.int32, (gw, gw), 0)
    cc = lax.broadcasted_iota(jnp.int32, (gw, gw), 1)
    low = jnp.where(rr > cc, gmat[0:gw, 0:gw], 0.0)
    aak = jnp.where(rr > cc, gmat[0:gw, gw:2 * gw], 0.0)
    arb = jnp.where(rr >= cc, gmat[gw:2 * gw, 0:gw], 0.0)
    ark = jnp.where(rr >= cc, gmat[gw:2 * gw, gw:2 * gw], 0.0)

    pw = low
    tinv = jnp.where(rr == cc, 1.0, low)
    for _ in range(5):
        pw = _dot(pw, pw, HI)
        tinv = tinv + _dot(tinv, pw, HI)

    zt = zt_sc[...]
    u = _dot(tinv, _dot_nt(ast, zt, HI) + _dot(aak, vst, HI), HI)
    ybd = _dot_nt(rst, zt, HI) + _dot(arb, u, HI) + _dot(ark, vst, HI)
    zt_sc[...] = zt * jnp.exp(cl) + _dot_tn(u, bhs, HI) + _dot_tn(vst, khs, HI)
    y = ybd[0:n] + ybd[n:2 * n] + ybd[2 * n:3 * n] + ybd[3 * n:4 * n]

    mean = _dot(y, head_ones, HI) * (1.0 / HEAD_DIM)
    yc = y - mean
    var = _dot(yc * yc, head_ones, HI) * (1.0 / HEAD_DIM)
    yn = yc * lax.rsqrt(var + 64e-5) * lg_ref[...] + lb_ref[...]
    bonus = _dot(r * k2 * rk_ref[...], head_ones, HI) * v
    o_ref[0] = ((yn + bonus) * g).astype(BF16)


def _rwkv(uc, mu, w0, w2, a0, a2, g2, k_k, k_a, r_k, lnx_g, lnx_b):
    bsz, lp, w = uc.shape
    nc = lp // CHUNK
    vec = lambda t: t.reshape(1, -1)
    full = lambda t: pl.BlockSpec(t.shape, lambda b, c: (0,) * t.ndim)
    args = [vec(mu), vec(w0), w2, vec(a0), a2, g2, vec(k_k), vec(k_a), vec(r_k), vec(lnx_g), vec(lnx_b)]
    return pl.pallas_call(
        _rwkv_kernel,
        grid=(bsz, nc),
        in_specs=[
            pl.BlockSpec((1, CHUNK, w), lambda b, c: (b, c, 0)),
            pl.BlockSpec((1, 8, w), lambda b, c: (b, jnp.maximum(c * (CHUNK // 8) - 1, 0), 0)),
        ] + [full(t) for t in args],
        out_specs=pl.BlockSpec((1, CHUNK, GROUP_WIDTH), lambda b, c: (b, c, 0)),
        out_shape=jax.ShapeDtypeStruct((bsz, lp, GROUP_WIDTH), BF16),
        scratch_shapes=[pltpu.VMEM((GROUP_WIDTH, GROUP_WIDTH), F32)],
        compiler_params=_params("arbitrary", "arbitrary"),
        name="rwkv7",
    )(uc, uc, *args)


def _conv_kernel(cur_ref, prev_ref, w_ref, b_ref, g_ref, gb_ref, o_ref, hcat_sc):
    j = pl.program_id(1)
    ch = GROUP_WIDTH

    def glu(t):
        return t[:, 0:ch] * _sigmoid(t[:, ch:2 * ch])

    hcat_sc[0:BLK, :] = jnp.where(j == 0, 0.0, glu(prev_ref[0]))
    hcat_sc[BLK:2 * BLK, :] = glu(cur_ref[0])
    acc = jnp.zeros((BLK, ch), F32) + b_ref[...]
    for t in range(CONV_WIDTH):
        lo = BLK - (CONV_WIDTH - 1) + t
        acc = acc + hcat_sc[lo:lo + BLK, :] * w_ref[t:t + 1, :]
    lane_grp = lax.broadcasted_iota(jnp.int32, (1, ch), 1) // HEAD_DIM
    sub_grp = lax.broadcasted_iota(jnp.int32, (ch, 1), 0) // HEAD_DIM
    grp_ones = (sub_grp == lane_grp).astype(F32)
    mean = _dot(acc, grp_ones, HI) * (1.0 / HEAD_DIM)
    xc = acc - mean
    var = _dot(xc * xc, grp_ones, HI) * (1.0 / HEAD_DIM)
    y = xc * lax.rsqrt(var + 1e-5) * g_ref[...] + gb_ref[...]
    o_ref[0] = (y * _sigmoid(y)).astype(BF16)


def _conv(ud, conv_w, conv_b, gn_g, gn_b):
    bsz, lp, w = ud.shape
    nb = lp // BLK
    vec = lambda t: t.reshape(1, -1)
    const = lambda shape: pl.BlockSpec(shape, lambda i, j: (0, 0))
    return pl.pallas_call(
        _conv_kernel,
        grid=(bsz, nb),
        in_specs=[
            pl.BlockSpec((1, BLK, w), lambda i, j: (i, j, 0)),
            pl.BlockSpec((1, BLK, w), lambda i, j: (i, jnp.maximum(j - 1, 0), 0)),
            const((CONV_WIDTH, GROUP_WIDTH)), const((1, GROUP_WIDTH)), const((1, GROUP_WIDTH)),
            const((1, GROUP_WIDTH)),
        ],
        out_specs=pl.BlockSpec((1, BLK, GROUP_WIDTH), lambda i, j: (i, j, 0)),
        out_shape=jax.ShapeDtypeStruct((bsz, lp, GROUP_WIDTH), BF16),
        scratch_shapes=[pltpu.VMEM((2 * BLK, GROUP_WIDTH), F32)],
        compiler_params=_params("parallel", "arbitrary"),
        name="conv_module",
    )(ud, ud, conv_w, vec(conv_b), vec(gn_g), vec(gn_b))


def _router(logit_t, rb_ref):
    s = [_sigmoid(logit_t[e:e + 1, :]) for e in range(N_EXPERTS)]
    bz = [s[e] + rb_ref[e:e + 1, 0:1] for e in range(N_EXPERTS)]
    per = N_EXPERTS // N_GROUPS
    gsum = []
    for gi in range(N_GROUPS):
        a, b, c, d = bz[per * gi:per * gi + per]
        hi1, lo1, hi2, lo2 = jnp.maximum(a, b), jnp.minimum(a, b), jnp.maximum(c, d), jnp.minimum(c, d)
        gsum.append(jnp.maximum(hi1, hi2) + jnp.maximum(jnp.minimum(hi1, hi2), jnp.maximum(lo1, lo2)))
    best = jnp.zeros_like(gsum[0], dtype=jnp.int32)
    bval = gsum[0]
    for gi in range(1, N_GROUPS):
        take = gsum[gi] > bval
        best = jnp.where(take, gi, best)
        bval = jnp.where(take, gsum[gi], bval)

    def pick(vals, i):
        out = vals[i]
        for gi in range(1, N_GROUPS):
            out = jnp.where(best == gi, vals[per * gi + i], out)
        return out

    bv = [pick(bz, i) for i in range(per)]
    sv = [pick(s, i) for i in range(per)]
    i1 = jnp.zeros_like(best)
    v1 = bv[0]
    for i in range(1, per):
        take = bv[i] > v1
        i1 = jnp.where(take, i, i1)
        v1 = jnp.where(take, bv[i], v1)
    i2 = jnp.full_like(best, -1)
    v2 = jnp.full_like(v1, -jnp.inf)
    for i in range(per):
        take = (i1 != i) & (bv[i] > v2)
        i2 = jnp.where(take, i, i2)
        v2 = jnp.where(take, bv[i], v2)
    s1 = sv[0]
    s2 = sv[0]
    for i in range(1, per):
        s1 = jnp.where(i1 == i, sv[i], s1)
        s2 = jnp.where(i2 == i, sv[i], s2)
    tot = s1 + s2
    g1, g2 = s1 / tot, s2 / tot
    rows = []
    for e in range(N_EXPERTS):
        gi, i = divmod(e, per)
        hit = jnp.where(best == gi, jnp.where(i1 == i, g1, jnp.where(i2 == i, g2, 0.0)), 0.0)
        rows.append(hit)
    return jnp.concatenate(rows, axis=0)


def _outproj_kernel(ya_ref, yb_ref, yc_ref, yd_ref, w_ref, h_ref, g_ref, b_ref, rwt_ref, rb_ref, o_ref, comb_ref, *,
                    tm):
    gw = GROUP_WIDTH
    mix = (_dot(ya_ref[0], w_ref[0:gw, :]) + _dot(yb_ref[0], w_ref[gw:2 * gw, :])
           + _dot(yc_ref[0], w_ref[2 * gw:3 * gw, :]) + _dot(yd_ref[0], w_ref[3 * gw:4 * gw, :]))
    h1 = _ln(ALPHA * h_ref[0] + mix, g_ref[...], b_ref[...])
    o_ref[0] = h1
    comb_t = _router(_dot_nt(rwt_ref[...], h1, HI), rb_ref)
    eye = (lax.broadcasted_iota(jnp.int32, (tm, tm), 0) == lax.broadcasted_iota(jnp.int32, (tm, tm), 1))
    comb_ref[0] = _dot_nt(eye.astype(F32), comb_t, HI)


def _outproj(ya, yb, yc, yd, w_bf, h, g, b, rwt, rb_b):
    bsz, lp, d = h.shape
    tm = _pick_tile(lp, (384, 256, 128))
    spec = lambda w: pl.BlockSpec((1, tm, w), lambda i, j: (i, j, 0))
    const = lambda shape: pl.BlockSpec(shape, lambda i, j: (0, 0))
    return pl.pallas_call(
        functools.partial(_outproj_kernel, tm=tm),
        grid=(bsz, lp // tm),
        in_specs=[spec(GROUP_WIDTH)] * 4 + [const((d, d)), spec(d), const((1, d)), const((1, d)),
                                            const((N_EXPERTS, d)), const((N_EXPERTS, BLK))],
        out_specs=[spec(d), spec(N_EXPERTS)],
        out_shape=[jax.ShapeDtypeStruct((bsz, lp, d), F32), jax.ShapeDtypeStruct((bsz, lp, N_EXPERTS), F32)],
        compiler_params=_params("parallel", "parallel"),
        name="out_proj_ln_router",
    )(ya, yb, yc, yd, w_bf, h, g.reshape(1, d), b.reshape(1, d), rwt, rb_b)


def _moe_kernel(h_ref, comb_ref, w1_ref, w3_ref, w2_ref, g_ref, b_ref, o_ref, tb_sc, acc_sc):
    e = pl.program_id(1)

    @pl.when(e == 0)
    def _():
        tb_sc[...] = h_ref[...].astype(BF16)
        acc_sc[...] = jnp.zeros(acc_sc.shape, F32)

    t = tb_sc[...]
    x1 = _dot(t, w1_ref[0])
    act = (x1 * _sigmoid(x1)) * _dot(t, w3_ref[0])
    lane = lax.broadcasted_iota(jnp.int32, (1, N_EXPERTS), 1)
    ce = jnp.sum(jnp.where(lane == e, comb_ref[...], 0.0), axis=1, keepdims=True)
    acc_sc[...] += ce * _dot(act.astype(BF16), w2_ref[0])

    @pl.when(e == N_EXPERTS - 1)
    def _():
        o_ref[...] = _ln(ALPHA * h_ref[...] + acc_sc[...], g_ref[...], b_ref[...])


def _moe(h1, comb, w1, w3, w2, g, b):
    r, d = h1.shape
    tm = _pick_tile(r, (1024, 512, 256, 128))
    return pl.pallas_call(
        _moe_kernel,
        grid=(r // tm, N_EXPERTS),
        in_specs=[
            pl.BlockSpec((tm, d), lambda i, e: (i, 0)),
            pl.BlockSpec((tm, N_EXPERTS), lambda i, e: (i, 0)),
            pl.BlockSpec((1, d, EXPERT_FF), lambda i, e: (e, 0, 0)),
            pl.BlockSpec((1, d, EXPERT_FF), lambda i, e: (e, 0, 0)),
            pl.BlockSpec((1, EXPERT_FF, d), lambda i, e: (e, 0, 0)),
            pl.BlockSpec((1, d), lambda i, e: (0, 0)),
            pl.BlockSpec((1, d), lambda i, e: (0, 0)),
        ],
        out_specs=pl.BlockSpec((tm, d), lambda i, e: (i, 0)),
        out_shape=jax.ShapeDtypeStruct((r, d), F32),
        scratch_shapes=[pltpu.VMEM((tm, d), BF16), pltpu.VMEM((tm, d), F32)],
        compiler_params=_params("parallel", "arbitrary"),
        name="moe_ln",
    )(h1, comb, w1, w3, w2, g.reshape(1, d), b.reshape(1, d))


def kernel(x, meta, ln0_g, ln0_b, w_in, swa_sinks, rel_bias, diff_lq1, diff_lk1, diff_lq2, diff_lk2, diff_subln_g,
           rwkv_mu, rwkv_w0, rwkv_w2, rwkv_a0, rwkv_a2, rwkv_g2, rwkv_kk, rwkv_ka, rwkv_rk, rwkv_lnx_g,
           rwkv_lnx_b, conv_w, conv_b, conv_gn_g, conv_gn_b, w_out, ln1_g, ln1_b, router_w, router_b, exp_w1,
           exp_w3, exp_w2, ln2_g, ln2_b):
    bsz, seq, d = x.shape
    assert d == D_MODEL and seq % BLK == 0
    lp = seq + BLK

    r = jnp.arange(BLK)
    bkt_band = _t5_bucket(r[:, None] - jnp.arange(2 * BLK)[None, :] + BLK)
    qpos3 = jnp.arange(3 * BLK).reshape(3, BLK) - PAD
    bkt_meta = _t5_bucket(qpos3[:, :, None] - jnp.arange(N_META)[None, None, :])
    dq0 = jnp.tile(r[None, :] - r[:, None], (1, 2))
    bkt_d = jnp.stack([_t5_bucket(dq0), _t5_bucket(dq0 + BLK)])
    rel_a, rel_b = rel_bias[:, :N_HEADS], rel_bias[:, N_HEADS:]
    rwt = router_w.T
    rb_b = jnp.broadcast_to(router_b.astype(F32)[:, None], (N_EXPERTS, BLK))
    vcols = slice(SWA_IN + 512, SWA_IN + 768)

    h = _prep(x, meta, ln0_g, ln0_b)
    for l in range(DEPTH):
        w_l = w_in[l]
        ua, qd, kd, vdt, uc, ud = _inproj(h, w_l.astype(BF16), w_l[:, vcols].T.astype(BF16))
        ya = _swa(ua, rel_a, swa_sinks[l], bkt_band, bkt_meta)
        lam_init = 0.8 - 0.6 * math.exp(-0.3 * l)
        lam = (jnp.exp(jnp.sum(diff_lq1[l] * diff_lk1[l])) - jnp.exp(jnp.sum(diff_lq2[l] * diff_lk2[l])) + lam_init)
        lam2 = jnp.stack([lam, jnp.asarray(1.0 - lam_init, F32)]).astype(F32)
        g_b = jnp.broadcast_to(diff_subln_g[l][:, None], (HEAD_DIM, BLK))
        yb = _diff(qd, kd, vdt, lam2, rel_b, bkt_d, g_b)
        yc = _rwkv(uc, rwkv_mu[l], rwkv_w0[l], rwkv_w2[l], rwkv_a0[l], rwkv_a2[l], rwkv_g2[l], rwkv_kk[l],
                   rwkv_ka[l], rwkv_rk[l], rwkv_lnx_g[l], rwkv_lnx_b[l])
        yd = _conv(ud, conv_w[l], conv_b[l], conv_gn_g[l], conv_gn_b[l])
        h1, comb = _outproj(ya, yb, yc, yd, w_out[l].astype(BF16), h, ln1_g[l], ln1_b[l], rwt, rb_b)
        h2 = _moe(h1.reshape(bsz * lp, d), comb.reshape(bsz * lp, N_EXPERTS), exp_w1[l].astype(BF16),
                  exp_w3[l].astype(BF16), exp_w2[l].astype(BF16), ln2_g[l], ln2_b[l])
        h = h2.reshape(bsz, lp, d)
    return h[:, BLK:]
```

```python
import functools
import math

import jax
import jax.numpy as jnp
import numpy as np
from jax import lax
from jax.experimental import pallas as pl
from jax.experimental.pallas import tpu as pltpu

F32 = jnp.float32
BF16 = jnp.bfloat16
HI = lax.Precision.HIGHEST

D_MODEL = 1024
N_META = 16
BLK = 128
TQ = 3 * BLK
HEAD_DIM = 64
GROUP_WIDTH = 256
N_HEADS = 4
SWA_KV_HEADS = 2
DIFF_QK = 32
DECAY_LORA = 64
AAA_LORA = 64
GATE_LORA = 128
CONV_WIDTH = 31
REL_BUCKETS = 32
REL_MAX_DIST = 128
N_EXPERTS = 16
N_GROUPS = 4
EXPERT_FF = 512
DEPTH = 2
ALPHA = (2 * DEPTH) ** 0.25
SWA_IN = 512
DIFF_IN = 768
RWKV_IN = 1024
CONV_IN = 512
IN_WIDTH = SWA_IN + DIFF_IN + RWKV_IN + CONV_IN
NEG = -1e30
LOG2E = math.log2(math.e)
CHUNK = 64
VMEM_LIMIT = 56 * 1024 * 1024


def _dot(a, b, prec=None):
    return jnp.dot(a, b, preferred_element_type=F32, precision=prec)


def _dot_nt(a, b, prec=None):
    return lax.dot_general(a, b, (((1,), (1,)), ((), ())), preferred_element_type=F32, precision=prec)


def _dot_tn(a, b, prec=None):
    return lax.dot_general(a, b, (((0,), (0,)), ((), ())), preferred_element_type=F32, precision=prec)


def _ln(x, g, b, eps=1e-5):
    mu = jnp.mean(x, -1, keepdims=True)
    xc = x - mu
    var = jnp.mean(xc * xc, -1, keepdims=True)
    return xc * lax.rsqrt(var + eps) * g + b


def _sigmoid(x):
    return 1.0 / (1.0 + jnp.exp(-x))


def _pick_tile(n, candidates):
    for c in candidates:
        if n % c == 0:
            return c
    raise ValueError(f"no tile in {candidates} divides {n}")


def _params(*sem):
    return pltpu.CompilerParams(dimension_semantics=sem, vmem_limit_bytes=VMEM_LIMIT)


def _t5_bucket(dist):
    n = jnp.maximum(dist, 0)
    max_exact = REL_BUCKETS // 2
    log_ratio = jnp.log(jnp.maximum(n, 1).astype(F32) / max_exact) / math.log(REL_MAX_DIST / max_exact)
    large = jnp.minimum(max_exact + (log_ratio * (REL_BUCKETS - max_exact)).astype(jnp.int32), REL_BUCKETS - 1)
    return jnp.where(n < max_exact, n, large)


def _gather_bias(bkt, rel_ref, h, offset=0.0):
    acc = jnp.zeros(bkt.shape, F32)
    for b in range(REL_BUCKETS):
        acc = jnp.where(bkt == b, rel_ref[b, h] - offset, acc)
    return acc


def _prep_kernel(xa_ref, xb_ref, meta_ref, g_ref, b_ref, o_ref, *, nbx):
    j = pl.program_id(1)
    head = jnp.where(j == 0, meta_ref[...], xa_ref[0])
    body = _ln(xb_ref[0, 0:BLK - N_META, :], g_ref[...], b_ref[...])
    o_ref[0, 0:N_META, :] = jnp.where(j <= nbx, _ln(head, g_ref[...], b_ref[...]), 0.0)
    o_ref[0, N_META:BLK, :] = jnp.where(j < nbx, body, 0.0)


def _prep(x, meta, g, b, lp):
    bsz, seq, d = x.shape
    nbx = seq // BLK
    per = BLK // N_META
    return pl.pallas_call(
        functools.partial(_prep_kernel, nbx=nbx),
        grid=(bsz, lp // BLK),
        in_specs=[
            pl.BlockSpec((1, N_META, d), lambda i, j: (i, jnp.clip(per * j - 1, 0, per * nbx - 1), 0)),
            pl.BlockSpec((1, BLK, d), lambda i, j: (i, jnp.minimum(j, nbx - 1), 0)),
            pl.BlockSpec((N_META, d), lambda i, j: (0, 0)),
            pl.BlockSpec((1, d), lambda i, j: (0, 0)),
            pl.BlockSpec((1, d), lambda i, j: (0, 0)),
        ],
        out_specs=pl.BlockSpec((1, BLK, d), lambda i, j: (i, j, 0)),
        out_shape=jax.ShapeDtypeStruct((bsz, lp, d), F32),
        compiler_params=_params("parallel", "parallel"),
        name="prep_ln",
    )(x, x, meta, g.reshape(1, d), b.reshape(1, d))


def _inproj_kernel(h_ref, w_ref, wvt_ref, ua_ref, qd_ref, kd_ref, vdt_ref, uc_ref, ud_ref):
    hb = h_ref[0].astype(BF16)

    def proj(c0, c1):
        return _dot(hb, w_ref[:, c0:c1])

    ua_ref[0, :, 0:256] = (proj(0, 256) * (HEAD_DIM ** -0.5)).astype(BF16)
    ua_ref[0, :, 256:512] = proj(256, 512).astype(BF16)
    qd_ref[0] = (proj(512, 768) * (DIFF_QK ** -0.5 * LOG2E)).astype(BF16)
    kd_ref[0] = proj(768, 1024).astype(BF16)
    vdt_ref[0, 0] = _dot_nt(wvt_ref[...], hb).astype(BF16)
    uc_ref[0] = proj(1280, 2304)
    ud_ref[0] = proj(2304, 2816)


def _inproj(h, w_bf, wvt_bf):
    bsz, lp, d = h.shape
    tm = TQ
    nt = lp // tm
    row = lambda w, dt: jax.ShapeDtypeStruct((bsz, lp, w), dt)
    spec = lambda w: pl.BlockSpec((1, tm, w), lambda i, j: (i, j, 0))
    return pl.pallas_call(
        _inproj_kernel,
        grid=(bsz, nt),
        in_specs=[
            spec(d),
            pl.BlockSpec((d, IN_WIDTH), lambda i, j: (0, 0)),
            pl.BlockSpec((GROUP_WIDTH, d), lambda i, j: (0, 0)),
        ],
        out_specs=[
            spec(512), spec(256), spec(256),
            pl.BlockSpec((1, 1, GROUP_WIDTH, tm), lambda i, j: (i, j, 0, 0)),
            spec(1024), spec(512),
        ],
        out_shape=[
            row(512, BF16), row(256, BF16), row(256, BF16),
            jax.ShapeDtypeStruct((bsz, nt, GROUP_WIDTH, tm), BF16),
            row(1024, F32), row(512, F32),
        ],
        compiler_params=_params("parallel", "parallel"),
        name="in_proj",
    )(h, w_bf, wvt_bf)


def _swa_kernel(rel_ref, sink_ref, bband_ref, bmeta_ref, cur_ref, prev_ref, meta_ref, o_ref, band_sc, metab_sc):
    j = pl.program_id(1)

    @pl.when(j == 0)
    def _():
        for h in range(N_HEADS):
            band_sc[h] = _gather_bias(bband_ref[...], rel_ref, h)

    @pl.when(j <= 2)
    def _():
        bkt = bmeta_ref[jnp.minimum(j, 2)]
        for h in range(N_HEADS):
            metab_sc[h] = _gather_bias(bkt, rel_ref, h)

    r = lax.broadcasted_iota(jnp.int32, (BLK, 1), 0)
    qpos = j * BLK + r
    ok_meta = qpos >= lax.broadcasted_iota(jnp.int32, (1, N_META), 1)
    dq = r - lax.broadcasted_iota(jnp.int32, (1, 2 * BLK), 1) + BLK
    ok_band = (dq >= 0) & (dq < BLK) & (qpos - dq >= N_META)

    cur = cur_ref[0]
    prev = prev_ref[0]
    meta = meta_ref[0, 0:N_META, :]
    outs = []
    for h in range(N_HEADS):
        g = h // (N_HEADS // SWA_KV_HEADS)
        kc, vc = 256 + HEAD_DIM * g, 384 + HEAD_DIM * g
        q = cur[:, HEAD_DIM * h:HEAD_DIM * (h + 1)]
        s_m = _dot_nt(q, meta[:, kc:kc + HEAD_DIM]) + metab_sc[h]
        s_p = _dot_nt(q, prev[:, kc:kc + HEAD_DIM]) + band_sc[h, :, 0:BLK]
        s_c = _dot_nt(q, cur[:, kc:kc + HEAD_DIM]) + band_sc[h, :, BLK:2 * BLK]
        s_m = jnp.where(ok_meta, s_m, NEG)
        s_p = jnp.where(ok_band[:, 0:BLK], s_p, NEG)
        s_c = jnp.where(ok_band[:, BLK:2 * BLK], s_c, NEG)
        sink = sink_ref[h]
        m = jnp.maximum(jnp.maximum(jnp.max(s_m, -1, keepdims=True), jnp.max(s_p, -1, keepdims=True)),
                        jnp.maximum(jnp.max(s_c, -1, keepdims=True), sink))
        p_m, p_p, p_c = jnp.exp(s_m - m), jnp.exp(s_p - m), jnp.exp(s_c - m)
        den = (jnp.sum(p_m, -1, keepdims=True) + jnp.sum(p_p, -1, keepdims=True)
               + jnp.sum(p_c, -1, keepdims=True) + jnp.exp(sink - m))
        o = (_dot(p_m.astype(BF16), meta[:, vc:vc + HEAD_DIM]) + _dot(p_p.astype(BF16), prev[:, vc:vc + HEAD_DIM])
             + _dot(p_c.astype(BF16), cur[:, vc:vc + HEAD_DIM]))
        outs.append(o / den)
    o_ref[0] = jnp.concatenate(outs, axis=1).astype(BF16)


def _swa(ua, rel_a, sinks, bkt_band, bkt_meta):
    bsz, lp, _ = ua.shape
    nb = lp // BLK
    smem = pl.BlockSpec(memory_space=pltpu.SMEM)
    return pl.pallas_call(
        _swa_kernel,
        grid=(bsz, nb),
        in_specs=[
            smem, smem,
            pl.BlockSpec((BLK, 2 * BLK), lambda i, j: (0, 0)),
            pl.BlockSpec((3, BLK, N_META), lambda i, j: (0, 0, 0)),
            pl.BlockSpec((1, BLK, SWA_IN), lambda i, j: (i, j, 0)),
            pl.BlockSpec((1, BLK, SWA_IN), lambda i, j: (i, jnp.maximum(j - 1, 0), 0)),
            pl.BlockSpec((1, BLK, SWA_IN), lambda i, j: (i, 0, 0)),
        ],
        out_specs=pl.BlockSpec((1, BLK, GROUP_WIDTH), lambda i, j: (i, j, 0)),
        out_shape=jax.ShapeDtypeStruct((bsz, lp, GROUP_WIDTH), BF16),
        scratch_shapes=[pltpu.VMEM((N_HEADS, BLK, 2 * BLK), F32), pltpu.VMEM((N_HEADS, BLK, N_META), F32)],
        compiler_params=_params("arbitrary", "arbitrary"),
        name="swa_attn",
    )(rel_a, sinks, bkt_band, bkt_meta, ua, ua, ua)


def _diff_kernel(lam_ref, rel_ref, bkt_ref, g_ref, q_ref, k_ref, vt_ref, o_ref, wt_sc, b1_sc, addm_sc, m_sc, l_sc,
                 acc_sc):
    i = pl.program_id(1)
    nsub = TQ // BLK

    @pl.when(i == 0)
    def _():
        kr = lax.broadcasted_iota(jnp.int32, (BLK, 1), 0)
        qc = lax.broadcasted_iota(jnp.int32, (1, BLK), 1)
        for h in range(N_HEADS):
            far = rel_ref[REL_BUCKETS - 1, h]
            b0 = jnp.where(qc >= kr, _gather_bias(bkt_ref[0], rel_ref, h, far) * LOG2E, NEG)
            b1 = _gather_bias(bkt_ref[1], rel_ref, h, far) * LOG2E
            b1_sc[h] = b1
            blocks = {0: b0, 1: b1, 2: jnp.zeros((BLK, BLK), F32)}
            masked = jnp.full((BLK, BLK), NEG, F32)
            for u in range(nsub):
                row = [blocks[w - u] if w >= u else masked for w in range(nsub)]
                addm_sc[h, u * BLK:(u + 1) * BLK, :] = jnp.concatenate(row + row, axis=1)

    q = q_ref[0]
    lane_grp = lax.broadcasted_iota(jnp.int32, (1, GROUP_WIDTH), 1) // DIFF_QK
    zero = jnp.zeros_like(q)
    for h in range(N_HEADS):
        wt_sc[h, 0:TQ, :] = jnp.where(lane_grp == 2 * h, q, zero)
        wt_sc[h, TQ:2 * TQ, :] = jnp.where(lane_grp == 2 * h + 1, q, zero)
    m_sc[...] = jnp.full(m_sc.shape, NEG, F32)
    l_sc[...] = jnp.zeros(l_sc.shape, F32)
    acc_sc[...] = jnp.zeros(acc_sc.shape, F32)

    def tile(t, kind):
        kt = k_ref[0, pl.ds(pl.multiple_of(t * TQ, TQ), TQ), :]
        vt = vt_ref[0, t]
        for h in range(N_HEADS):
            s = _dot_nt(kt, wt_sc[h])
            if kind == "diag":
                s = s + addm_sc[h]
            elif kind == "near":
                top, bot = s[0:TQ - BLK], s[TQ - BLK:TQ]
                b1 = b1_sc[h]
                bot = jnp.concatenate([bot[:, 0:BLK] + b1, bot[:, BLK:TQ], bot[:, TQ:TQ + BLK] + b1,
                                       bot[:, TQ + BLK:2 * TQ]], axis=1)
                s = jnp.concatenate([top, bot], axis=0)
            m_old = m_sc[h]
            m_new = jnp.maximum(m_old, jnp.max(s, 0, keepdims=True))
            a = jnp.exp2(m_old - m_new)
            p = jnp.exp2(s - m_new)
            l_sc[h] = a * l_sc[h] + jnp.sum(p, 0, keepdims=True)
            pv = _dot(vt[HEAD_DIM * h:HEAD_DIM * (h + 1), :], p.astype(BF16))
            acc_sc[h] = a * acc_sc[h] + pv
            m_sc[h] = m_new

    def far(t, c):
        tile(t, None)
        return c

    lax.fori_loop(0, i - 1, far, 0)

    @pl.when(i >= 1)
    def _():
        tile(i - 1, "near")

    tile(i, "diag")

    lam = lam_ref[0]
    post = lam_ref[1]
    outs = []
    for h in range(N_HEADS):
        o = acc_sc[h] / l_sc[h]
        o = o[:, 0:TQ] - lam * o[:, TQ:2 * TQ]
        ms = jnp.mean(o * o, 0, keepdims=True)
        outs.append(o * lax.rsqrt(ms + 1e-5) * g_ref[...] * post)
    ot = jnp.concatenate(outs, axis=0).astype(BF16)
    eye = (lax.broadcasted_iota(jnp.int32, (TQ, TQ), 0) == lax.broadcasted_iota(jnp.int32, (TQ, TQ), 1))
    o_ref[0] = _dot_nt(eye.astype(BF16), ot).astype(BF16)


def _diff(qd, kd, vdt, lam2, rel_b, bkt_d, g_b):
    bsz, lp, _ = qd.shape
    nq = lp // TQ
    smem = pl.BlockSpec(memory_space=pltpu.SMEM)
    return pl.pallas_call(
        _diff_kernel,
        grid=(bsz, nq),
        in_specs=[
            smem, smem,
            pl.BlockSpec((2, BLK, BLK), lambda b, i: (0, 0, 0)),
            pl.BlockSpec((HEAD_DIM, 1), lambda b, i: (0, 0)),
            pl.BlockSpec((1, TQ, GROUP_WIDTH), lambda b, i: (b, i, 0)),
            pl.BlockSpec((1, lp, GROUP_WIDTH), lambda b, i: (b, 0, 0)),
            pl.BlockSpec((1, nq, GROUP_WIDTH, TQ), lambda b, i: (b, 0, 0, 0)),
        ],
        out_specs=pl.BlockSpec((1, TQ, GROUP_WIDTH), lambda b, i: (b, i, 0)),
        out_shape=jax.ShapeDtypeStruct((bsz, lp, GROUP_WIDTH), BF16),
        scratch_shapes=[
            pltpu.VMEM((N_HEADS, 2 * TQ, GROUP_WIDTH), BF16),
            pltpu.VMEM((N_HEADS, BLK, BLK), F32),
            pltpu.VMEM((N_HEADS, TQ, 2 * TQ), F32),
            pltpu.VMEM((N_HEADS, 1, 2 * TQ), F32),
            pltpu.VMEM((N_HEADS, 1, 2 * TQ), F32),
            pltpu.VMEM((N_HEADS, HEAD_DIM, 2 * TQ), F32),
        ],
        compiler_params=_params("arbitrary", "arbitrary"),
        name="diff_attn",
    )(lam2, rel_b, bkt_d, g_b, qd, kd, vdt)


def _rwkv_kernel(x_ref, xp_ref, mu_ref, w0_ref, w2_ref, a0_ref, a2_ref, g2_ref, kk_ref, ka_ref, rk_ref,
                 lg_ref, lb_ref, o_ref, zt_sc):
    c = pl.program_id(1)
    n = CHUNK
    gw = GROUP_WIDTH

    @pl.when(c == 0)
    def _():
        zt_sc[...] = jnp.zeros(zt_sc.shape, F32)

    x = x_ref[0]
    row = lax.broadcasted_iota(jnp.int32, (n, 1), 0)
    last_prev = jnp.where(c == 0, 0.0, xp_ref[0, 7:8, :])
    xs = jnp.where(row == 0, last_prev, pltpu.roll(x, 1, 0))
    xm = x + mu_ref[...] * (xs - x)
    r, k, v = xm[:, 0:gw], xm[:, gw:2 * gw], xm[:, 2 * gw:3 * gw]
    xw = xm[:, 768:768 + DECAY_LORA]
    xa = xm[:, 832:832 + AAA_LORA]
    xg = xm[:, 896:896 + GATE_LORA]
    logw = -math.exp(-0.5) * _sigmoid(w0_ref[...] + _dot(jnp.tanh(xw), w2_ref[...], HI))
    a = _sigmoid(a0_ref[...] + _dot(xa, a2_ref[...], HI))
    g = _dot(_sigmoid(xg), g2_ref[...], HI)

    lane_head = lax.broadcasted_iota(jnp.int32, (1, gw), 1) // HEAD_DIM
    sub_head = lax.broadcasted_iota(jnp.int32, (gw, 1), 0) // HEAD_DIM
    head_ones = (sub_head == lane_head).astype(F32)

    kk = k * kk_ref[...]
    kk = kk / jnp.maximum(jnp.sqrt(_dot(kk * kk, head_ones, HI)), 1e-12)
    k2 = k * (1.0 + (a - 1.0) * ka_ref[...])
    bvec = kk * a

    tri = (lax.broadcasted_iota(jnp.int32, (n, n), 0) >= lax.broadcasted_iota(jnp.int32, (n, n), 1)).astype(F32)
    cum = _dot(tri, logw, HI)
    cl = cum[n - 1:n, :]
    e_neg = jnp.exp(-cum)
    e_last = jnp.exp(cl - cum)
    a_t = -kk * jnp.exp(cum - logw)
    r_t = r * jnp.exp(cum)
    b_t, k_t = bvec * e_neg, k2 * e_neg
    b_h, k_h = bvec * e_last, k2 * e_last

    def stack(t):
        return jnp.concatenate([jnp.where(lane_head == h, t, 0.0) for h in range(N_HEADS)], axis=0)

    ast, rst, bst, kst, bhs, khs, vst = (stack(t) for t in (a_t, r_t, b_t, k_t, b_h, k_h, v))
    gmat = _dot_nt(jnp.concatenate([ast, rst], axis=0), jnp.concatenate([bst, kst], axis=0), HI)
    rr = lax.broadcasted_iota(jnp.int32, (gw, gw), 0)
    cc = lax.broadcasted_iota(jnp.int32, (gw, gw), 1)
    low = jnp.where(rr > cc, gmat[0:gw, 0:gw], 0.0)
    aak = jnp.where(rr > cc, gmat[0:gw, gw:2 * gw], 0.0)
    arb = jnp.where(rr >= cc, gmat[gw:2 * gw, 0:gw], 0.0)
    ark = jnp.where(rr >= cc, gmat[gw:2 * gw, gw:2 * gw], 0.0)

    pw = low
    tinv = jnp.where(rr == cc, 1.0, low)
    for _ in range(5):
        pw = _dot(pw, pw, HI)
        tinv = tinv + _dot(tinv, pw, HI)

    zt = zt_sc[...]
    u = _dot(tinv, _dot_nt(ast, zt, HI) + _dot(aak, vst, HI), HI)
    ybd = _dot_nt(rst, zt, HI) + _dot(arb, u, HI) + _dot(ark, vst, HI)
    zt_sc[...] = zt * jnp.exp(cl) + _dot_tn(u, bhs, HI) + _dot_tn(vst, khs, HI)
    y = ybd[0:n] + ybd[n:2 * n] + ybd[2 * n:3 * n] + ybd[3 * n:4 * n]

    mean = _dot(y, head_ones, HI) * (1.0 / HEAD_DIM)
    yc = y - mean
    var = _dot(yc * yc, head_ones, HI) * (1.0 / HEAD_DIM)
    yn = yc * lax.rsqrt(var + 64e-5) * lg_ref[...] + lb_ref[...]
    bonus = _dot(r * k2 * rk_ref[...], head_ones, HI) * v
    o_ref[0] = ((yn + bonus) * g).astype(BF16)


def _rwkv(uc, mu, w0, w2, a0, a2, g2, k_k, k_a, r_k, lnx_g, lnx_b):
    bsz, lp, w = uc.shape
    nc = lp // CHUNK
    vec = lambda t: t.reshape(1, -1)
    full = lambda t: pl.BlockSpec(t.shape, lambda b, c: (0,) * t.ndim)
    args = [vec(mu), vec(w0), w2, vec(a0), a2, g2, vec(k_k), vec(k_a), vec(r_k), vec(lnx_g), vec(lnx_b)]
    return pl.pallas_call(
        _rwkv_kernel,
        grid=(bsz, nc),
        in_specs=[
            pl.BlockSpec((1, CHUNK, w), lambda b, c: (b, c, 0)),
            pl.BlockSpec((1, 8, w), lambda b, c: (b, jnp.maximum(c * (CHUNK // 8) - 1, 0), 0)),
        ] + [full(t) for t in args],
        out_specs=pl.BlockSpec((1, CHUNK, GROUP_WIDTH), lambda b, c: (b, c, 0)),
        out_shape=jax.ShapeDtypeStruct((bsz, lp, GROUP_WIDTH), BF16),
        scratch_shapes=[pltpu.VMEM((GROUP_WIDTH, GROUP_WIDTH), F32)],
        compiler_params=_params("arbitrary", "arbitrary"),
        name="rwkv7",
    )(uc, uc, *args)


def _conv_kernel(cur_ref, prev_ref, w_ref, b_ref, g_ref, gb_ref, o_ref, hcat_sc):
    j = pl.program_id(1)
    ch = GROUP_WIDTH

    def glu(t):
        return t[:, 0:ch] * _sigmoid(t[:, ch:2 * ch])

    hcat_sc[0:BLK, :] = jnp.where(j == 0, 0.0, glu(prev_ref[0]))
    hcat_sc[BLK:2 * BLK, :] = glu(cur_ref[0])
    acc = jnp.zeros((BLK, ch), F32) + b_ref[...]
    for t in range(CONV_WIDTH):
        lo = BLK - (CONV_WIDTH - 1) + t
        acc = acc + hcat_sc[lo:lo + BLK, :] * w_ref[t:t + 1, :]
    lane_grp = lax.broadcasted_iota(jnp.int32, (1, ch), 1) // HEAD_DIM
    sub_grp = lax.broadcasted_iota(jnp.int32, (ch, 1), 0) // HEAD_DIM
    grp_ones = (sub_grp == lane_grp).astype(F32)
    mean = _dot(acc, grp_ones, HI) * (1.0 / HEAD_DIM)
    xc = acc - mean
    var = _dot(xc * xc, grp_ones, HI) * (1.0 / HEAD_DIM)
    y = xc * lax.rsqrt(var + 1e-5) * g_ref[...] + gb_ref[...]
    o_ref[0] = (y * _sigmoid(y)).astype(BF16)


def _conv(ud, conv_w, conv_b, gn_g, gn_b):
    bsz, lp, w = ud.shape
    nb = lp // BLK
    vec = lambda t: t.reshape(1, -1)
    const = lambda shape: pl.BlockSpec(shape, lambda i, j: (0, 0))
    return pl.pallas_call(
        _conv_kernel,
        grid=(bsz, nb),
        in_specs=[
            pl.BlockSpec((1, BLK, w), lambda i, j: (i, j, 0)),
            pl.BlockSpec((1, BLK, w), lambda i, j: (i, jnp.maximum(j - 1, 0), 0)),
            const((CONV_WIDTH, GROUP_WIDTH)), const((1, GROUP_WIDTH)), const((1, GROUP_WIDTH)),
            const((1, GROUP_WIDTH)),
        ],
        out_specs=pl.BlockSpec((1, BLK, GROUP_WIDTH), lambda i, j: (i, j, 0)),
        out_shape=jax.ShapeDtypeStruct((bsz, lp, GROUP_WIDTH), BF16),
        scratch_shapes=[pltpu.VMEM((2 * BLK, GROUP_WIDTH), F32)],
        compiler_params=_params("parallel", "arbitrary"),
        name="conv_module",
    )(ud, ud, conv_w, vec(conv_b), vec(gn_g), vec(gn_b))


def _router(logit_t, rb_ref):
    s = [_sigmoid(logit_t[e:e + 1, :]) for e in range(N_EXPERTS)]
    bz = [s[e] + rb_ref[e:e + 1, 0:1] for e in range(N_EXPERTS)]
    per = N_EXPERTS // N_GROUPS
    gsum = []
    for gi in range(N_GROUPS):
        a, b, c, d = bz[per * gi:per * gi + per]
        hi1, lo1, hi2, lo2 = jnp.maximum(a, b), jnp.minimum(a, b), jnp.maximum(c, d), jnp.minimum(c, d)
        gsum.append(jnp.maximum(hi1, hi2) + jnp.maximum(jnp.minimum(hi1, hi2), jnp.maximum(lo1, lo2)))
    best = jnp.zeros_like(gsum[0], dtype=jnp.int32)
    bval = gsum[0]
    for gi in range(1, N_GROUPS):
        take = gsum[gi] > bval
        best = jnp.where(take, gi, best)
        bval = jnp.where(take, gsum[gi], bval)

    def pick(vals, i):
        out = vals[i]
        for gi in range(1, N_GROUPS):
            out = jnp.where(best == gi, vals[per * gi + i], out)
        return out

    bv = [pick(bz, i) for i in range(per)]
    sv = [pick(s, i) for i in range(per)]
    i1 = jnp.zeros_like(best)
    v1 = bv[0]
    for i in range(1, per):
        take = bv[i] > v1
        i1 = jnp.where(take, i, i1)
        v1 = jnp.where(take, bv[i], v1)
    i2 = jnp.full_like(best, -1)
    v2 = jnp.full_like(v1, -jnp.inf)
    for i in range(per):
        take = (i1 != i) & (bv[i] > v2)
        i2 = jnp.where(take, i, i2)
        v2 = jnp.where(take, bv[i], v2)
    s1 = sv[0]
    s2 = sv[0]
    for i in range(1, per):
        s1 = jnp.where(i1 == i, sv[i], s1)
        s2 = jnp.where(i2 == i, sv[i], s2)
    tot = s1 + s2
    g1, g2 = s1 / tot, s2 / tot
    rows = []
    for e in range(N_EXPERTS):
        gi, i = divmod(e, per)
        hit = jnp.where(best == gi, jnp.where(i1 == i, g1, jnp.where(i2 == i, g2, 0.0)), 0.0)
        rows.append(hit)
    return jnp.concatenate(rows, axis=0)


def _outproj_kernel(ya_ref, yb_ref, yc_ref, yd_ref, w_ref, h_ref, g_ref, b_ref, rwt_ref, rb_ref, o_ref, comb_ref, *,
                    tm):
    gw = GROUP_WIDTH
    mix = (_dot(ya_ref[0], w_ref[0:gw, :]) + _dot(yb_ref[0], w_ref[gw:2 * gw, :])
           + _dot(yc_ref[0], w_ref[2 * gw:3 * gw, :]) + _dot(yd_ref[0], w_ref[3 * gw:4 * gw, :]))
    h1 = _ln(ALPHA * h_ref[0] + mix, g_ref[...], b_ref[...])
    o_ref[0] = h1
    comb_t = _router(_dot_nt(rwt_ref[...], h1, HI), rb_ref)
    eye = (lax.broadcasted_iota(jnp.int32, (tm, tm), 0) == lax.broadcasted_iota(jnp.int32, (tm, tm), 1))
    comb_ref[0] = _dot_nt(eye.astype(F32), comb_t, HI)


def _outproj(ya, yb, yc, yd, w_bf, h, g, b, rwt, rb_b):
    bsz, lp, d = h.shape
    tm = _pick_tile(lp, (384, 256, 128))
    spec = lambda w: pl.BlockSpec((1, tm, w), lambda i, j: (i, j, 0))
    const = lambda shape: pl.BlockSpec(shape, lambda i, j: (0, 0))
    return pl.pallas_call(
        functools.partial(_outproj_kernel, tm=tm),
        grid=(bsz, lp // tm),
        in_specs=[spec(GROUP_WIDTH)] * 4 + [const((d, d)), spec(d), const((1, d)), const((1, d)),
                                            const((N_EXPERTS, d)), const((N_EXPERTS, BLK))],
        out_specs=[spec(d), spec(N_EXPERTS)],
        out_shape=[jax.ShapeDtypeStruct((bsz, lp, d), F32), jax.ShapeDtypeStruct((bsz, lp, N_EXPERTS), F32)],
        compiler_params=_params("parallel", "parallel"),
        name="out_proj_ln_router",
    )(ya, yb, yc, yd, w_bf, h, g.reshape(1, d), b.reshape(1, d), rwt, rb_b)


def _moe_kernel(h_ref, comb_ref, w1_ref, w3_ref, w2_ref, g_ref, b_ref, o_ref, tb_sc, acc_sc):
    e = pl.program_id(1)

    @pl.when(e == 0)
    def _():
        tb_sc[...] = h_ref[...].astype(BF16)
        acc_sc[...] = jnp.zeros(acc_sc.shape, F32)

    t = tb_sc[...]
    x1 = _dot(t, w1_ref[0])
    act = (x1 * _sigmoid(x1)) * _dot(t, w3_ref[0])
    lane = lax.broadcasted_iota(jnp.int32, (1, N_EXPERTS), 1)
    ce = jnp.sum(jnp.where(lane == e, comb_ref[...], 0.0), axis=1, keepdims=True)
    acc_sc[...] += ce * _dot(act.astype(BF16), w2_ref[0])

    @pl.when(e == N_EXPERTS - 1)
    def _():
        o_ref[...] = _ln(ALPHA * h_ref[...] + acc_sc[...], g_ref[...], b_ref[...])


def _moe(h1, comb, w1, w3, w2, g, b):
    r, d = h1.shape
    tm = _pick_tile(r, (1024, 512, 256, 128))
    return pl.pallas_call(
        _moe_kernel,
        grid=(r // tm, N_EXPERTS),
        in_specs=[
            pl.BlockSpec((tm, d), lambda i, e: (i, 0)),
            pl.BlockSpec((tm, N_EXPERTS), lambda i, e: (i, 0)),
            pl.BlockSpec((1, d, EXPERT_FF), lambda i, e: (e, 0, 0)),
            pl.BlockSpec((1, d, EXPERT_FF), lambda i, e: (e, 0, 0)),
            pl.BlockSpec((1, EXPERT_FF, d), lambda i, e: (e, 0, 0)),
            pl.BlockSpec((1, d), lambda i, e: (0, 0)),
            pl.BlockSpec((1, d), lambda i, e: (0, 0)),
        ],
        out_specs=pl.BlockSpec((tm, d), lambda i, e: (i, 0)),
        out_shape=jax.ShapeDtypeStruct((r, d), F32),
        scratch_shapes=[pltpu.VMEM((tm, d), BF16), pltpu.VMEM((tm, d), F32)],
        compiler_params=_params("parallel", "arbitrary"),
        name="moe_ln",
    )(h1, comb, w1, w3, w2, g.reshape(1, d), b.reshape(1, d))


def kernel(x, meta, ln0_g, ln0_b, w_in, swa_sinks, rel_bias, diff_lq1, diff_lk1, diff_lq2, diff_lk2, diff_subln_g,
           rwkv_mu, rwkv_w0, rwkv_w2, rwkv_a0, rwkv_a2, rwkv_g2, rwkv_kk, rwkv_ka, rwkv_rk, rwkv_lnx_g,
           rwkv_lnx_b, conv_w, conv_b, conv_gn_g, conv_gn_b, w_out, ln1_g, ln1_b, router_w, router_b, exp_w1,
           exp_w3, exp_w2, ln2_g, ln2_b):
    bsz, seq, d = x.shape
    assert d == D_MODEL and seq % BLK == 0
    lp = pl.cdiv(seq + N_META, TQ) * TQ

    r = jnp.arange(BLK)
    bkt_band = _t5_bucket(r[:, None] - jnp.arange(2 * BLK)[None, :] + BLK)
    qpos3 = jnp.arange(3 * BLK).reshape(3, BLK)
    bkt_meta = _t5_bucket(qpos3[:, :, None] - jnp.arange(N_META)[None, None, :])
    dq0 = r[None, :] - r[:, None]
    bkt_d = jnp.stack([_t5_bucket(dq0), _t5_bucket(dq0 + BLK)])
    rel_a, rel_b = rel_bias[:, :N_HEADS], rel_bias[:, N_HEADS:]
    rwt = router_w.T
    rb_b = jnp.broadcast_to(router_b.astype(F32)[:, None], (N_EXPERTS, BLK))
    vcols = slice(SWA_IN + 512, SWA_IN + 768)

    h = _prep(x, meta, ln0_g, ln0_b, lp)
    for l in range(DEPTH):
        w_l = w_in[l]
        ua, qd, kd, vdt, uc, ud = _inproj(h, w_l.astype(BF16), w_l[:, vcols].T.astype(BF16))
        ya = _swa(ua, rel_a, swa_sinks[l], bkt_band, bkt_meta)
        lam_init = 0.8 - 0.6 * math.exp(-0.3 * l)
        lam = (jnp.exp(jnp.sum(diff_lq1[l] * diff_lk1[l])) - jnp.exp(jnp.sum(diff_lq2[l] * diff_lk2[l])) + lam_init)
        lam2 = jnp.stack([lam, jnp.asarray(1.0 - lam_init, F32)]).astype(F32)
        yb = _diff(qd, kd, vdt, lam2, rel_b, bkt_d, diff_subln_g[l].reshape(HEAD_DIM, 1))
        yc = _rwkv(uc, rwkv_mu[l], rwkv_w0[l], rwkv_w2[l], rwkv_a0[l], rwkv_a2[l], rwkv_g2[l], rwkv_kk[l],
                   rwkv_ka[l], rwkv_rk[l], rwkv_lnx_g[l], rwkv_lnx_b[l])
        yd = _conv(ud, conv_w[l], conv_b[l], conv_gn_g[l], conv_gn_b[l])
        h1, comb = _outproj(ya, yb, yc, yd, w_out[l].astype(BF16), h, ln1_g[l], ln1_b[l], rwt, rb_b)
        h2 = _moe(h1.reshape(bsz * lp, d), comb.reshape(bsz * lp, N_EXPERTS), exp_w1[l].astype(BF16),
                  exp_w3[l].astype(BF16), exp_w2[l].astype(BF16), ln2_g[l], ln2_b[l])
        h = h2.reshape(bsz, lp, d)
    return h[:, N_META:N_META + seq]
```

```python
import functools
import math

import jax
import jax.numpy as jnp
import numpy as np
from jax import lax
from jax.experimental import pallas as pl
from jax.experimental.pallas import tpu as pltpu

F32 = jnp.float32
BF16 = jnp.bfloat16
HI = lax.Precision.HIGHEST

D_MODEL = 1024
N_META = 16
BLK = 128
TQ = 3 * BLK
HEAD_DIM = 64
GROUP_WIDTH = 256
N_HEADS = 4
SWA_KV_HEADS = 2
DIFF_QK = 32
DECAY_LORA = 64
AAA_LORA = 64
GATE_LORA = 128
CONV_WIDTH = 31
REL_BUCKETS = 32
REL_MAX_DIST = 128
N_EXPERTS = 16
N_GROUPS = 4
EXPERT_FF = 512
DEPTH = 2
ALPHA = (2 * DEPTH) ** 0.25
SWA_IN = 512
DIFF_IN = 768
RWKV_IN = 1024
CONV_IN = 512
IN_WIDTH = SWA_IN + DIFF_IN + RWKV_IN + CONV_IN
NEG = -1e30
LOG2E = math.log2(math.e)
CHUNK = 64
RSTEP = TQ
VMEM_LIMIT = 56 * 1024 * 1024


def _dot(a, b, prec=None):
    return jnp.dot(a, b, preferred_element_type=F32, precision=prec)


def _dot_nt(a, b, prec=None):
    return lax.dot_general(a, b, (((1,), (1,)), ((), ())), preferred_element_type=F32, precision=prec)


def _dot_tn(a, b, prec=None):
    return lax.dot_general(a, b, (((0,), (0,)), ((), ())), preferred_element_type=F32, precision=prec)


def _split2(x):
    hi = x.astype(BF16)
    return hi, (x - hi.astype(F32)).astype(BF16)


def _dot_exact_rhs(x, m_bf):
    hi, lo = _split2(x)
    return _dot(hi, m_bf) + _dot(lo, m_bf)


def _dot_exact_lhs(m_bf, x):
    hi, lo = _split2(x)
    return _dot(m_bf, hi) + _dot(m_bf, lo)


def _dot3(a, b):
    ah, al = _split2(a)
    bh, bl = _split2(b)
    return _dot(ah, bh) + _dot(ah, bl) + _dot(al, bh)


def _ln(x, g, b, eps=1e-5):
    mu = jnp.mean(x, -1, keepdims=True)
    xc = x - mu
    var = jnp.mean(xc * xc, -1, keepdims=True)
    return xc * lax.rsqrt(var + eps) * g + b


def _sigmoid(x):
    return 1.0 / (1.0 + jnp.exp(-x))


def _pick_tile(n, candidates):
    for c in candidates:
        if n % c == 0:
            return c
    raise ValueError(f"no tile in {candidates} divides {n}")


def _params(*sem):
    return pltpu.CompilerParams(dimension_semantics=sem, vmem_limit_bytes=VMEM_LIMIT)


def _t5_bucket(dist):
    n = jnp.maximum(dist, 0)
    max_exact = REL_BUCKETS // 2
    log_ratio = jnp.log(jnp.maximum(n, 1).astype(F32) / max_exact) / math.log(REL_MAX_DIST / max_exact)
    large = jnp.minimum(max_exact + (log_ratio * (REL_BUCKETS - max_exact)).astype(jnp.int32), REL_BUCKETS - 1)
    return jnp.where(n < max_exact, n, large)


def _gather_bias(bkt, rel_ref, h, offset=0.0):
    acc = jnp.zeros(bkt.shape, F32)
    for b in range(REL_BUCKETS):
        acc = jnp.where(bkt == b, rel_ref[b, h] - offset, acc)
    return acc


def _prep_kernel(xa_ref, xb_ref, meta_ref, g_ref, b_ref, o_ref, *, nbx):
    j = pl.program_id(1)
    head = jnp.where(j == 0, meta_ref[...], xa_ref[0])
    body = _ln(xb_ref[0, 0:BLK - N_META, :], g_ref[...], b_ref[...])
    o_ref[0, 0:N_META, :] = jnp.where(j <= nbx, _ln(head, g_ref[...], b_ref[...]), 0.0)
    o_ref[0, N_META:BLK, :] = jnp.where(j < nbx, body, 0.0)


def _prep(x, meta, g, b, lp):
    bsz, seq, d = x.shape
    nbx = seq // BLK
    per = BLK // N_META
    return pl.pallas_call(
        functools.partial(_prep_kernel, nbx=nbx),
        grid=(bsz, lp // BLK),
        in_specs=[
            pl.BlockSpec((1, N_META, d), lambda i, j: (i, jnp.clip(per * j - 1, 0, per * nbx - 1), 0)),
            pl.BlockSpec((1, BLK, d), lambda i, j: (i, jnp.minimum(j, nbx - 1), 0)),
            pl.BlockSpec((N_META, d), lambda i, j: (0, 0)),
            pl.BlockSpec((1, d), lambda i, j: (0, 0)),
            pl.BlockSpec((1, d), lambda i, j: (0, 0)),
        ],
        out_specs=pl.BlockSpec((1, BLK, d), lambda i, j: (i, j, 0)),
        out_shape=jax.ShapeDtypeStruct((bsz, lp, d), F32),
        compiler_params=_params("parallel", "parallel"),
        name="prep_ln",
    )(x, x, meta, g.reshape(1, d), b.reshape(1, d))


def _inproj_kernel(h_ref, w_ref, wvt_ref, ua_ref, qd_ref, kd_ref, vdt_ref, uc_ref, ud_ref):
    hb = h_ref[0].astype(BF16)

    def proj(c0, c1):
        return _dot(hb, w_ref[:, c0:c1])

    ua_ref[0, :, 0:256] = (proj(0, 256) * (HEAD_DIM ** -0.5)).astype(BF16)
    ua_ref[0, :, 256:512] = proj(256, 512).astype(BF16)
    qd_ref[0] = (proj(512, 768) * (DIFF_QK ** -0.5 * LOG2E)).astype(BF16)
    kd_ref[0] = proj(768, 1024).astype(BF16)
    vdt_ref[0, 0] = _dot_nt(wvt_ref[...], hb).astype(BF16)
    uc_ref[0] = proj(1280, 2304)
    ud_ref[0] = proj(2304, 2816)


def _inproj(h, w_bf, wvt_bf):
    bsz, lp, d = h.shape
    tm = TQ
    nt = lp // tm
    row = lambda w, dt: jax.ShapeDtypeStruct((bsz, lp, w), dt)
    spec = lambda w: pl.BlockSpec((1, tm, w), lambda i, j: (i, j, 0))
    return pl.pallas_call(
        _inproj_kernel,
        grid=(bsz, nt),
        in_specs=[
            spec(d),
            pl.BlockSpec((d, IN_WIDTH), lambda i, j: (0, 0)),
            pl.BlockSpec((GROUP_WIDTH, d), lambda i, j: (0, 0)),
        ],
        out_specs=[
            spec(512), spec(256), spec(256),
            pl.BlockSpec((1, 1, GROUP_WIDTH, tm), lambda i, j: (i, j, 0, 0)),
            spec(1024), spec(512),
        ],
        out_shape=[
            row(512, BF16), row(256, BF16), row(256, BF16),
            jax.ShapeDtypeStruct((bsz, nt, GROUP_WIDTH, tm), BF16),
            row(1024, F32), row(512, F32),
        ],
        compiler_params=_params("parallel", "parallel"),
        name="in_proj",
    )(h, w_bf, wvt_bf)


def _swa_kernel(rel_ref, sink_ref, bband_ref, bmeta_ref, cur_ref, prev_ref, meta_ref, o_ref, band_sc, metab_sc):
    j = pl.program_id(1)

    @pl.when(j == 0)
    def _():
        for h in range(N_HEADS):
            band_sc[h] = _gather_bias(bband_ref[...], rel_ref, h)

    @pl.when(j <= 2)
    def _():
        bkt = bmeta_ref[jnp.minimum(j, 2)]
        for h in range(N_HEADS):
            metab_sc[h] = _gather_bias(bkt, rel_ref, h)

    r = lax.broadcasted_iota(jnp.int32, (BLK, 1), 0)
    qpos = j * BLK + r
    ok_meta = qpos >= lax.broadcasted_iota(jnp.int32, (1, N_META), 1)
    dq = r - lax.broadcasted_iota(jnp.int32, (1, 2 * BLK), 1) + BLK
    ok_band = (dq >= 0) & (dq < BLK) & (qpos - dq >= N_META)

    cur = cur_ref[0]
    prev = prev_ref[0]
    meta = meta_ref[0, 0:N_META, :]
    outs = []
    for h in range(N_HEADS):
        g = h // (N_HEADS // SWA_KV_HEADS)
        kc, vc = 256 + HEAD_DIM * g, 384 + HEAD_DIM * g
        q = cur[:, HEAD_DIM * h:HEAD_DIM * (h + 1)]
        s_m = _dot_nt(q, meta[:, kc:kc + HEAD_DIM]) + metab_sc[h]
        s_p = _dot_nt(q, prev[:, kc:kc + HEAD_DIM]) + band_sc[h, :, 0:BLK]
        s_c = _dot_nt(q, cur[:, kc:kc + HEAD_DIM]) + band_sc[h, :, BLK:2 * BLK]
        s_m = jnp.where(ok_meta, s_m, NEG)
        s_p = jnp.where(ok_band[:, 0:BLK], s_p, NEG)
        s_c = jnp.where(ok_band[:, BLK:2 * BLK], s_c, NEG)
        sink = sink_ref[h]
        m = jnp.maximum(jnp.maximum(jnp.max(s_m, -1, keepdims=True), jnp.max(s_p, -1, keepdims=True)),
                        jnp.maximum(jnp.max(s_c, -1, keepdims=True), sink))
        p_m, p_p, p_c = jnp.exp(s_m - m), jnp.exp(s_p - m), jnp.exp(s_c - m)
        den = (jnp.sum(p_m, -1, keepdims=True) + jnp.sum(p_p, -1, keepdims=True)
               + jnp.sum(p_c, -1, keepdims=True) + jnp.exp(sink - m))
        o = (_dot(p_m.astype(BF16), meta[:, vc:vc + HEAD_DIM]) + _dot(p_p.astype(BF16), prev[:, vc:vc + HEAD_DIM])
             + _dot(p_c.astype(BF16), cur[:, vc:vc + HEAD_DIM]))
        outs.append(o / den)
    o_ref[0] = jnp.concatenate(outs, axis=1).astype(BF16)


def _swa(ua, rel_a, sinks, bkt_band, bkt_meta):
    bsz, lp, _ = ua.shape
    nb = lp // BLK
    smem = pl.BlockSpec(memory_space=pltpu.SMEM)
    return pl.pallas_call(
        _swa_kernel,
        grid=(bsz, nb),
        in_specs=[
            smem, smem,
            pl.BlockSpec((BLK, 2 * BLK), lambda i, j: (0, 0)),
            pl.BlockSpec((3, BLK, N_META), lambda i, j: (0, 0, 0)),
            pl.BlockSpec((1, BLK, SWA_IN), lambda i, j: (i, j, 0)),
            pl.BlockSpec((1, BLK, SWA_IN), lambda i, j: (i, jnp.maximum(j - 1, 0), 0)),
            pl.BlockSpec((1, BLK, SWA_IN), lambda i, j: (i, 0, 0)),
        ],
        out_specs=pl.BlockSpec((1, BLK, GROUP_WIDTH), lambda i, j: (i, j, 0)),
        out_shape=jax.ShapeDtypeStruct((bsz, lp, GROUP_WIDTH), BF16),
        scratch_shapes=[pltpu.VMEM((N_HEADS, BLK, 2 * BLK), F32), pltpu.VMEM((N_HEADS, BLK, N_META), F32)],
        compiler_params=_params("arbitrary", "arbitrary"),
        name="swa_attn",
    )(rel_a, sinks, bkt_band, bkt_meta, ua, ua, ua)


def _diff_kernel(lam_ref, rel_ref, bkt_ref, g_ref, q_ref, k_ref, vt_ref, o_ref, wt_sc, b1_sc, addm_sc, m_sc, l_sc,
                 acc_sc):
    i = pl.program_id(1)
    nsub = TQ // BLK

    @pl.when(i == 0)
    def _():
        kr = lax.broadcasted_iota(jnp.int32, (BLK, 1), 0)
        qc = lax.broadcasted_iota(jnp.int32, (1, BLK), 1)
        for h in range(N_HEADS):
            far = rel_ref[REL_BUCKETS - 1, h]
            b0 = jnp.where(qc >= kr, _gather_bias(bkt_ref[0], rel_ref, h, far) * LOG2E, NEG)
            b1 = _gather_bias(bkt_ref[1], rel_ref, h, far) * LOG2E
            b1_sc[h] = b1
            blocks = {0: b0, 1: b1, 2: jnp.zeros((BLK, BLK), F32)}
            masked = jnp.full((BLK, BLK), NEG, F32)
            for u in range(nsub):
                row = [blocks[w - u] if w >= u else masked for w in range(nsub)]
                addm_sc[h, u * BLK:(u + 1) * BLK, :] = jnp.concatenate(row + row, axis=1)

    q = q_ref[0]
    lane_grp = lax.broadcasted_iota(jnp.int32, (1, GROUP_WIDTH), 1) // DIFF_QK
    zero = jnp.zeros_like(q)
    for h in range(N_HEADS):
        wt_sc[h, 0:TQ, :] = jnp.where(lane_grp == 2 * h, q, zero)
        wt_sc[h, TQ:2 * TQ, :] = jnp.where(lane_grp == 2 * h + 1, q, zero)
    m_sc[...] = jnp.full(m_sc.shape, NEG, F32)
    l_sc[...] = jnp.zeros(l_sc.shape, F32)
    acc_sc[...] = jnp.zeros(acc_sc.shape, F32)

    def tile(t, kind):
        kt = k_ref[0, pl.ds(pl.multiple_of(t * TQ, TQ), TQ), :]
        vt = vt_ref[0, t]
        for h in range(N_HEADS):
            s = _dot_nt(kt, wt_sc[h])
            if kind == "diag":
                s = s + addm_sc[h]
            elif kind == "near":
                top, bot = s[0:TQ - BLK], s[TQ - BLK:TQ]
                b1 = b1_sc[h]
                bot = jnp.concatenate([bot[:, 0:BLK] + b1, bot[:, BLK:TQ], bot[:, TQ:TQ + BLK] + b1,
                                       bot[:, TQ + BLK:2 * TQ]], axis=1)
                s = jnp.concatenate([top, bot], axis=0)
            m_old = m_sc[h]
            m_new = jnp.maximum(m_old, jnp.max(s, 0, keepdims=True))
            a = jnp.exp2(m_old - m_new)
            p = jnp.exp2(s - m_new)
            l_sc[h] = a * l_sc[h] + jnp.sum(p, 0, keepdims=True)
            pv = _dot(vt[HEAD_DIM * h:HEAD_DIM * (h + 1), :], p.astype(BF16))
            acc_sc[h] = a * acc_sc[h] + pv
            m_sc[h] = m_new

    def far(t, c):
        tile(t, None)
        return c

    lax.fori_loop(0, i - 1, far, 0)

    @pl.when(i >= 1)
    def _():
        tile(i - 1, "near")

    tile(i, "diag")

    lam = lam_ref[0]
    post = lam_ref[1]
    outs = []
    for h in range(N_HEADS):
        o = acc_sc[h] / l_sc[h]
        o = o[:, 0:TQ] - lam * o[:, TQ:2 * TQ]
        ms = jnp.mean(o * o, 0, keepdims=True)
        outs.append(o * lax.rsqrt(ms + 1e-5) * g_ref[...] * post)
    ot = jnp.concatenate(outs, axis=0).astype(BF16)
    eye = (lax.broadcasted_iota(jnp.int32, (TQ, TQ), 0) == lax.broadcasted_iota(jnp.int32, (TQ, TQ), 1))
    o_ref[0] = _dot_nt(eye.astype(BF16), ot).astype(BF16)


def _diff(qd, kd, vdt, lam2, rel_b, bkt_d, g_b):
    bsz, lp, _ = qd.shape
    nq = lp // TQ
    smem = pl.BlockSpec(memory_space=pltpu.SMEM)
    return pl.pallas_call(
        _diff_kernel,
        grid=(bsz, nq),
        in_specs=[
            smem, smem,
            pl.BlockSpec((2, BLK, BLK), lambda b, i: (0, 0, 0)),
            pl.BlockSpec((HEAD_DIM, 1), lambda b, i: (0, 0)),
            pl.BlockSpec((1, TQ, GROUP_WIDTH), lambda b, i: (b, i, 0)),
            pl.BlockSpec((1, lp, GROUP_WIDTH), lambda b, i: (b, 0, 0)),
            pl.BlockSpec((1, nq, GROUP_WIDTH, TQ), lambda b, i: (b, 0, 0, 0)),
        ],
        out_specs=pl.BlockSpec((1, TQ, GROUP_WIDTH), lambda b, i: (b, i, 0)),
        out_shape=jax.ShapeDtypeStruct((bsz, lp, GROUP_WIDTH), BF16),
        scratch_shapes=[
            pltpu.VMEM((N_HEADS, 2 * TQ, GROUP_WIDTH), BF16),
            pltpu.VMEM((N_HEADS, BLK, BLK), F32),
            pltpu.VMEM((N_HEADS, TQ, 2 * TQ), F32),
            pltpu.VMEM((N_HEADS, 1, 2 * TQ), F32),
            pltpu.VMEM((N_HEADS, 1, 2 * TQ), F32),
            pltpu.VMEM((N_HEADS, HEAD_DIM, 2 * TQ), F32),
        ],
        compiler_params=_params("arbitrary", "arbitrary"),
        name="diff_attn",
    )(lam2, rel_b, bkt_d, g_b, qd, kd, vdt)


def _rwkv_kernel(x_ref, xp_ref, mu_ref, w0_ref, w2_ref, a0_ref, a2_ref, g2_ref, kk_ref, ka_ref, rk_ref,
                 lg_ref, lb_ref, o_ref, zt_sc):
    c = pl.program_id(1)
    n = CHUNK
    ns = RSTEP // CHUNK
    gw = GROUP_WIDTH

    @pl.when(c == 0)
    def _():
        zt_sc[...] = jnp.zeros(zt_sc.shape, F32)

    x = x_ref[0]
    row = lax.broadcasted_iota(jnp.int32, (RSTEP, 1), 0)
    last_prev = jnp.where(c == 0, 0.0, xp_ref[0, 7:8, :])
    xs = jnp.where(row == 0, last_prev, pltpu.roll(x, 1, 0))
    xm = x + mu_ref[...] * (xs - x)
    r, k, v = xm[:, 0:gw], xm[:, gw:2 * gw], xm[:, 2 * gw:3 * gw]
    xw = xm[:, 768:768 + DECAY_LORA]
    xa = xm[:, 832:832 + AAA_LORA]
    xg = xm[:, 896:896 + GATE_LORA]
    logw = -math.exp(-0.5) * _sigmoid(w0_ref[...] + _dot3(jnp.tanh(xw), w2_ref[...]))
    a = _sigmoid(a0_ref[...] + _dot3(xa, a2_ref[...]))
    g = _dot3(_sigmoid(xg), g2_ref[...])

    lane_head = lax.broadcasted_iota(jnp.int32, (1, gw), 1) // HEAD_DIM
    sub_head = lax.broadcasted_iota(jnp.int32, (gw, 1), 0) // HEAD_DIM
    head_ones = (sub_head == lane_head).astype(BF16)

    kk = k * kk_ref[...]
    kk = kk / jnp.maximum(jnp.sqrt(_dot_exact_rhs(kk * kk, head_ones)), 1e-12)
    k2 = k * (1.0 + (a - 1.0) * ka_ref[...])
    bvec = kk * a

    ri = lax.broadcasted_iota(jnp.int32, (RSTEP, RSTEP), 0)
    ci = lax.broadcasted_iota(jnp.int32, (RSTEP, RSTEP), 1)
    tri = (((ri // n) == (ci // n)) & (ri >= ci)).astype(BF16)
    cum = _dot_exact_lhs(tri, logw)
    tot = jnp.concatenate([jnp.broadcast_to(cum[(s + 1) * n - 1:(s + 1) * n, :], (n, gw)) for s in range(ns)], axis=0)
    e_neg = jnp.exp(-cum)
    e_last = jnp.exp(tot - cum)
    a_t = (-kk * jnp.exp(cum - logw)).astype(BF16)
    r_t = (r * jnp.exp(cum)).astype(BF16)
    b_t, k_t = (bvec * e_neg).astype(BF16), (k2 * e_neg).astype(BF16)
    b_h, k_h = (bvec * e_last).astype(BF16), (k2 * e_last).astype(BF16)
    g_c = jnp.exp(tot)
    vb = v.astype(BF16)

    def stack(t, s):
        ts = t[s * n:(s + 1) * n]
        return jnp.concatenate([jnp.where(lane_head == h, ts, jnp.zeros_like(ts)) for h in range(N_HEADS)], axis=0)

    rr = lax.broadcasted_iota(jnp.int32, (gw, gw), 0)
    cc = lax.broadcasted_iota(jnp.int32, (gw, gw), 1)
    parts = []
    for s in range(ns):
        ast, rst, bst, kst = stack(a_t, s), stack(r_t, s), stack(b_t, s), stack(k_t, s)
        gmat = _dot_nt(jnp.concatenate([ast, rst], axis=0), jnp.concatenate([bst, kst], axis=0))
        low = jnp.where(rr > cc, gmat[0:gw, 0:gw], 0.0)
        aak = jnp.where(rr > cc, gmat[0:gw, gw:2 * gw], 0.0).astype(BF16)
        arb = jnp.where(rr >= cc, gmat[gw:2 * gw, 0:gw], 0.0).astype(BF16)
        ark = jnp.where(rr >= cc, gmat[gw:2 * gw, gw:2 * gw], 0.0).astype(BF16)
        pw = low
        tinv = jnp.where(rr == cc, 1.0, low)
        for _ in range(5):
            pwb = pw.astype(BF16)
            pw = _dot(pwb, pwb)
            tinv = tinv + _dot(tinv.astype(BF16), pw.astype(BF16))
        parts.append((ast, rst, aak, arb, ark, tinv.astype(BF16), stack(b_h, s), stack(k_h, s), stack(vb, s)))

    zt = zt_sc[...]
    ys = []
    for s in range(ns):
        ast, rst, aak, arb, ark, tinv, bhs, khs, vst = parts[s]
        ztb = zt.astype(BF16)
        u = _dot(tinv, (_dot_nt(ast, ztb) + _dot(aak, vst)).astype(BF16)).astype(BF16)
        ybd = _dot_nt(rst, ztb) + _dot(arb, u) + _dot(ark, vst)
        zt = zt * g_c[s * n:s * n + 1, :] + _dot_tn(u, bhs) + _dot_tn(vst, khs)
        ys.append(ybd[0:n] + ybd[n:2 * n] + ybd[2 * n:3 * n] + ybd[3 * n:4 * n])
    zt_sc[...] = zt
    y = jnp.concatenate(ys, axis=0)

    mean = _dot_exact_rhs(y, head_ones) * (1.0 / HEAD_DIM)
    yc = y - mean
    var = _dot_exact_rhs(yc * yc, head_ones) * (1.0 / HEAD_DIM)
    yn = yc * lax.rsqrt(var + 64e-5) * lg_ref[...] + lb_ref[...]
    bonus = _dot_exact_rhs(r * k2 * rk_ref[...], head_ones) * v
    o_ref[0] = ((yn + bonus) * g).astype(BF16)


def _rwkv(uc, mu, w0, w2, a0, a2, g2, k_k, k_a, r_k, lnx_g, lnx_b):
    bsz, lp, w = uc.shape
    nc = lp // RSTEP
    vec = lambda t: t.reshape(1, -1)
    full = lambda t: pl.BlockSpec(t.shape, lambda b, c: (0,) * t.ndim)
    args = [vec(mu), vec(w0), w2, vec(a0), a2, g2, vec(k_k), vec(k_a), vec(r_k), vec(lnx_g), vec(lnx_b)]
    return pl.pallas_call(
        _rwkv_kernel,
        grid=(bsz, nc),
        in_specs=[
            pl.BlockSpec((1, RSTEP, w), lambda b, c: (b, c, 0)),
            pl.BlockSpec((1, 8, w), lambda b, c: (b, jnp.maximum(c * (RSTEP // 8) - 1, 0), 0)),
        ] + [full(t) for t in args],
        out_specs=pl.BlockSpec((1, RSTEP, GROUP_WIDTH), lambda b, c: (b, c, 0)),
        out_shape=jax.ShapeDtypeStruct((bsz, lp, GROUP_WIDTH), BF16),
        scratch_shapes=[pltpu.VMEM((GROUP_WIDTH, GROUP_WIDTH), F32)],
        compiler_params=_params("arbitrary", "arbitrary"),
        name="rwkv7",
    )(uc, uc, *args)


def _conv_kernel(cur_ref, prev_ref, w_ref, b_ref, g_ref, gb_ref, o_ref, hcat_sc):
    j = pl.program_id(1)
    ch = GROUP_WIDTH

    def glu(t):
        return t[:, 0:ch] * _sigmoid(t[:, ch:2 * ch])

    hcat_sc[0:BLK, :] = jnp.where(j == 0, 0.0, glu(prev_ref[0]))
    hcat_sc[BLK:2 * BLK, :] = glu(cur_ref[0])
    acc = jnp.zeros((BLK, ch), F32) + b_ref[...]
    for t in range(CONV_WIDTH):
        lo = BLK - (CONV_WIDTH - 1) + t
        acc = acc + hcat_sc[lo:lo + BLK, :] * w_ref[t:t + 1, :]
    lane_grp = lax.broadcasted_iota(jnp.int32, (1, ch), 1) // HEAD_DIM
    sub_grp = lax.broadcasted_iota(jnp.int32, (ch, 1), 0) // HEAD_DIM
    grp_ones = (sub_grp == lane_grp).astype(BF16)
    mean = _dot_exact_rhs(acc, grp_ones) * (1.0 / HEAD_DIM)
    xc = acc - mean
    var = _dot_exact_rhs(xc * xc, grp_ones) * (1.0 / HEAD_DIM)
    y = xc * lax.rsqrt(var + 1e-5) * g_ref[...] + gb_ref[...]
    o_ref[0] = (y * _sigmoid(y)).astype(BF16)


def _conv(ud, conv_w, conv_b, gn_g, gn_b):
    bsz, lp, w = ud.shape
    nb = lp // BLK
    vec = lambda t: t.reshape(1, -1)
    const = lambda shape: pl.BlockSpec(shape, lambda i, j: (0, 0))
    return pl.pallas_call(
        _conv_kernel,
        grid=(bsz, nb),
        in_specs=[
            pl.BlockSpec((1, BLK, w), lambda i, j: (i, j, 0)),
            pl.BlockSpec((1, BLK, w), lambda i, j: (i, jnp.maximum(j - 1, 0), 0)),
            const((CONV_WIDTH, GROUP_WIDTH)), const((1, GROUP_WIDTH)), const((1, GROUP_WIDTH)),
            const((1, GROUP_WIDTH)),
        ],
        out_specs=pl.BlockSpec((1, BLK, GROUP_WIDTH), lambda i, j: (i, j, 0)),
        out_shape=jax.ShapeDtypeStruct((bsz, lp, GROUP_WIDTH), BF16),
        scratch_shapes=[pltpu.VMEM((2 * BLK, GROUP_WIDTH), F32)],
        compiler_params=_params("parallel", "arbitrary"),
        name="conv_module",
    )(ud, ud, conv_w, vec(conv_b), vec(gn_g), vec(gn_b))


def _router(logit_t, rb_ref):
    s = [_sigmoid(logit_t[e:e + 1, :]) for e in range(N_EXPERTS)]
    bz = [s[e] + rb_ref[e:e + 1, 0:1] for e in range(N_EXPERTS)]
    per = N_EXPERTS // N_GROUPS
    gsum = []
    for gi in range(N_GROUPS):
        a, b, c, d = bz[per * gi:per * gi + per]
        hi1, lo1, hi2, lo2 = jnp.maximum(a, b), jnp.minimum(a, b), jnp.maximum(c, d), jnp.minimum(c, d)
        gsum.append(jnp.maximum(hi1, hi2) + jnp.maximum(jnp.minimum(hi1, hi2), jnp.maximum(lo1, lo2)))
    best = jnp.zeros_like(gsum[0], dtype=jnp.int32)
    bval = gsum[0]
    for gi in range(1, N_GROUPS):
        take = gsum[gi] > bval
        best = jnp.where(take, gi, best)
        bval = jnp.where(take, gsum[gi], bval)

    def pick(vals, i):
        out = vals[i]
        for gi in range(1, N_GROUPS):
            out = jnp.where(best == gi, vals[per * gi + i], out)
        return out

    bv = [pick(bz, i) for i in range(per)]
    sv = [pick(s, i) for i in range(per)]
    i1 = jnp.zeros_like(best)
    v1 = bv[0]
    for i in range(1, per):
        take = bv[i] > v1
        i1 = jnp.where(take, i, i1)
        v1 = jnp.where(take, bv[i], v1)
    i2 = jnp.full_like(best, -1)
    v2 = jnp.full_like(v1, -jnp.inf)
    for i in range(per):
        take = (i1 != i) & (bv[i] > v2)
        i2 = jnp.where(take, i, i2)
        v2 = jnp.where(take, bv[i], v2)
    s1 = sv[0]
    s2 = sv[0]
    for i in range(1, per):
        s1 = jnp.where(i1 == i, sv[i], s1)
        s2 = jnp.where(i2 == i, sv[i], s2)
    tot = s1 + s2
    g1, g2 = s1 / tot, s2 / tot
    rows = []
    for e in range(N_EXPERTS):
        gi, i = divmod(e, per)
        hit = jnp.where(best == gi, jnp.where(i1 == i, g1, jnp.where(i2 == i, g2, 0.0)), 0.0)
        rows.append(hit)
    return jnp.concatenate(rows, axis=0)


def _outproj_kernel(ya_ref, yb_ref, yc_ref, yd_ref, w_ref, h_ref, g_ref, b_ref, rwt_ref, rb_ref, o_ref, comb_ref, *,
                    tm):
    gw = GROUP_WIDTH
    mix = (_dot(ya_ref[0], w_ref[0:gw, :]) + _dot(yb_ref[0], w_ref[gw:2 * gw, :])
           + _dot(yc_ref[0], w_ref[2 * gw:3 * gw, :]) + _dot(yd_ref[0], w_ref[3 * gw:4 * gw, :]))
    h1 = _ln(ALPHA * h_ref[0] + mix, g_ref[...], b_ref[...])
    o_ref[0] = h1
    comb_t = _router(_dot_nt(rwt_ref[...], h1, HI), rb_ref)
    eye = (lax.broadcasted_iota(jnp.int32, (tm, tm), 0) == lax.broadcasted_iota(jnp.int32, (tm, tm), 1))
    comb_ref[0] = _dot_nt(eye.astype(F32), comb_t, HI)


def _outproj(ya, yb, yc, yd, w_bf, h, g, b, rwt, rb_b):
    bsz, lp, d = h.shape
    tm = _pick_tile(lp, (384, 256, 128))
    spec = lambda w: pl.BlockSpec((1, tm, w), lambda i, j: (i, j, 0))
    const = lambda shape: pl.BlockSpec(shape, lambda i, j: (0, 0))
    return pl.pallas_call(
        functools.partial(_outproj_kernel, tm=tm),
        grid=(bsz, lp // tm),
        in_specs=[spec(GROUP_WIDTH)] * 4 + [const((d, d)), spec(d), const((1, d)), const((1, d)),
                                            const((N_EXPERTS, d)), const((N_EXPERTS, BLK))],
        out_specs=[spec(d), spec(N_EXPERTS)],
        out_shape=[jax.ShapeDtypeStruct((bsz, lp, d), F32), jax.ShapeDtypeStruct((bsz, lp, N_EXPERTS), F32)],
        compiler_params=_params("parallel", "parallel"),
        name="out_proj_ln_router",
    )(ya, yb, yc, yd, w_bf, h, g.reshape(1, d), b.reshape(1, d), rwt, rb_b)


def _moe_kernel(h_ref, comb_ref, w1_ref, w3_ref, w2_ref, g_ref, b_ref, o_ref, tb_sc, acc_sc):
    e = pl.program_id(1)

    @pl.when(e == 0)
    def _():
        tb_sc[...] = h_ref[...].astype(BF16)
        acc_sc[...] = jnp.zeros(acc_sc.shape, F32)

    t = tb_sc[...]
    x1 = _dot(t, w1_ref[0])
    act = (x1 * _sigmoid(x1)) * _dot(t, w3_ref[0])
    lane = lax.broadcasted_iota(jnp.int32, (1, N_EXPERTS), 1)
    ce = jnp.sum(jnp.where(lane == e, comb_ref[...], 0.0), axis=1, keepdims=True)
    acc_sc[...] += ce * _dot(act.astype(BF16), w2_ref[0])

    @pl.when(e == N_EXPERTS - 1)
    def _():
        o_ref[...] = _ln(ALPHA * h_ref[...] + acc_sc[...], g_ref[...], b_ref[...])


def _moe(h1, comb, w1, w3, w2, g, b):
    r, d = h1.shape
    tm = _pick_tile(r, (1024, 512, 256, 128))
    return pl.pallas_call(
        _moe_kernel,
        grid=(r // tm, N_EXPERTS),
        in_specs=[
            pl.BlockSpec((tm, d), lambda i, e: (i, 0)),
            pl.BlockSpec((tm, N_EXPERTS), lambda i, e: (i, 0)),
            pl.BlockSpec((1, d, EXPERT_FF), lambda i, e: (e, 0, 0)),
            pl.BlockSpec((1, d, EXPERT_FF), lambda i, e: (e, 0, 0)),
            pl.BlockSpec((1, EXPERT_FF, d), lambda i, e: (e, 0, 0)),
            pl.BlockSpec((1, d), lambda i, e: (0, 0)),
            pl.BlockSpec((1, d), lambda i, e: (0, 0)),
        ],
        out_specs=pl.BlockSpec((tm, d), lambda i, e: (i, 0)),
        out_shape=jax.ShapeDtypeStruct((r, d), F32),
        scratch_shapes=[pltpu.VMEM((tm, d), BF16), pltpu.VMEM((tm, d), F32)],
        compiler_params=_params("parallel", "arbitrary"),
        name="moe_ln",
    )(h1, comb, w1, w3, w2, g.reshape(1, d), b.reshape(1, d))


def kernel(x, meta, ln0_g, ln0_b, w_in, swa_sinks, rel_bias, diff_lq1, diff_lk1, diff_lq2, diff_lk2, diff_subln_g,
           rwkv_mu, rwkv_w0, rwkv_w2, rwkv_a0, rwkv_a2, rwkv_g2, rwkv_kk, rwkv_ka, rwkv_rk, rwkv_lnx_g,
           rwkv_lnx_b, conv_w, conv_b, conv_gn_g, conv_gn_b, w_out, ln1_g, ln1_b, router_w, router_b, exp_w1,
           exp_w3, exp_w2, ln2_g, ln2_b):
    bsz, seq, d = x.shape
    assert d == D_MODEL and seq % BLK == 0
    lp = pl.cdiv(seq + N_META, TQ) * TQ

    r = jnp.arange(BLK)
    bkt_band = _t5_bucket(r[:, None] - jnp.arange(2 * BLK)[None, :] + BLK)
    qpos3 = jnp.arange(3 * BLK).reshape(3, BLK)
    bkt_meta = _t5_bucket(qpos3[:, :, None] - jnp.arange(N_META)[None, None, :])
    dq0 = r[None, :] - r[:, None]
    bkt_d = jnp.stack([_t5_bucket(dq0), _t5_bucket(dq0 + BLK)])
    rel_a, rel_b = rel_bias[:, :N_HEADS], rel_bias[:, N_HEADS:]
    rwt = router_w.T
    rb_b = jnp.broadcast_to(router_b.astype(F32)[:, None], (N_EXPERTS, BLK))
    vcols = slice(SWA_IN + 512, SWA_IN + 768)

    h = _prep(x, meta, ln0_g, ln0_b, lp)
    for l in range(DEPTH):
        w_l = w_in[l]
        ua, qd, kd, vdt, uc, ud = _inproj(h, w_l.astype(BF16), w_l[:, vcols].T.astype(BF16))
        ya = _swa(ua, rel_a, swa_sinks[l], bkt_band, bkt_meta)
        lam_init = 0.8 - 0.6 * math.exp(-0.3 * l)
        lam = (jnp.exp(jnp.sum(diff_lq1[l] * diff_lk1[l])) - jnp.exp(jnp.sum(diff_lq2[l] * diff_lk2[l])) + lam_init)
        lam2 = jnp.stack([lam, jnp.asarray(1.0 - lam_init, F32)]).astype(F32)
        yb = _diff(qd, kd, vdt, lam2, rel_b, bkt_d, diff_subln_g[l].reshape(HEAD_DIM, 1))
        yc = _rwkv(uc, rwkv_mu[l], rwkv_w0[l], rwkv_w2[l], rwkv_a0[l], rwkv_a2[l], rwkv_g2[l], rwkv_kk[l],
                   rwkv_ka[l], rwkv_rk[l], rwkv_lnx_g[l], rwkv_lnx_b[l])
        yd = _conv(ud, conv_w[l], conv_b[l], conv_gn_g[l], conv_gn_b[l])
        h1, comb = _outproj(ya, yb, yc, yd, w_out[l].astype(BF16), h, ln1_g[l], ln1_b[l], rwt, rb_b)
        h2 = _moe(h1.reshape(bsz * lp, d), comb.reshape(bsz * lp, N_EXPERTS), exp_w1[l].astype(BF16),
                  exp_w3[l].astype(BF16), exp_w2[l].astype(BF16), ln2_g[l], ln2_b[l])
        h = h2.reshape(bsz, lp, d)
    return h[:, N_META:N_META + seq]
```

```python
import functools
import math

import jax
import jax.numpy as jnp
import numpy as np
from jax import lax
from jax.experimental import pallas as pl
from jax.experimental.pallas import tpu as pltpu

F32 = jnp.float32
BF16 = jnp.bfloat16
HI = lax.Precision.HIGHEST

D_MODEL = 1024
N_META = 16
BLK = 128
TQ = 3 * BLK
HEAD_DIM = 64
GROUP_WIDTH = 256
N_HEADS = 4
SWA_KV_HEADS = 2
DIFF_QK = 32
DECAY_LORA = 64
AAA_LORA = 64
GATE_LORA = 128
CONV_WIDTH = 31
REL_BUCKETS = 32
REL_MAX_DIST = 128
N_EXPERTS = 16
N_GROUPS = 4
EXPERT_FF = 512
DEPTH = 2
ALPHA = (2 * DEPTH) ** 0.25
SWA_IN = 512
DIFF_IN = 768
RWKV_IN = 1024
CONV_IN = 512
IN_WIDTH = SWA_IN + DIFF_IN + RWKV_IN + CONV_IN
NEG = -1e30
LOG2E = math.log2(math.e)
CHUNK = 64
RSTEP = TQ
MOE_TM = 256
VMEM_LIMIT = 56 * 1024 * 1024


def _dot(a, b, prec=None):
    return jnp.dot(a, b, preferred_element_type=F32, precision=prec)


def _dot_nt(a, b, prec=None):
    return lax.dot_general(a, b, (((1,), (1,)), ((), ())), preferred_element_type=F32, precision=prec)


def _dot_tn(a, b, prec=None):
    return lax.dot_general(a, b, (((0,), (0,)), ((), ())), preferred_element_type=F32, precision=prec)


def _split2(x):
    hi = x.astype(BF16)
    return hi, (x - hi.astype(F32)).astype(BF16)


def _dot_exact_rhs(x, m_bf):
    hi, lo = _split2(x)
    return _dot(hi, m_bf) + _dot(lo, m_bf)


def _dot_exact_lhs(m_bf, x):
    hi, lo = _split2(x)
    return _dot(m_bf, hi) + _dot(m_bf, lo)


def _dot3(a, b):
    ah, al = _split2(a)
    bh, bl = _split2(b)
    return _dot(ah, bh) + _dot(ah, bl) + _dot(al, bh)


def _ln(x, g, b, eps=1e-5):
    mu = jnp.mean(x, -1, keepdims=True)
    xc = x - mu
    var = jnp.mean(xc * xc, -1, keepdims=True)
    return xc * lax.rsqrt(var + eps) * g + b


def _sigmoid(x):
    return 1.0 / (1.0 + jnp.exp(-x))


def _pick_tile(n, candidates):
    for c in candidates:
        if n % c == 0:
            return c
    raise ValueError(f"no tile in {candidates} divides {n}")


def _params(*sem):
    return pltpu.CompilerParams(dimension_semantics=sem, vmem_limit_bytes=VMEM_LIMIT)


def _t5_bucket(dist):
    n = jnp.maximum(dist, 0)
    max_exact = REL_BUCKETS // 2
    log_ratio = jnp.log(jnp.maximum(n, 1).astype(F32) / max_exact) / math.log(REL_MAX_DIST / max_exact)
    large = jnp.minimum(max_exact + (log_ratio * (REL_BUCKETS - max_exact)).astype(jnp.int32), REL_BUCKETS - 1)
    return jnp.where(n < max_exact, n, large)


def _gather_bias(bkt, rel_ref, h, offset=0.0):
    acc = jnp.zeros(bkt.shape, F32)
    for b in range(REL_BUCKETS):
        acc = jnp.where(bkt == b, rel_ref[b, h] - offset, acc)
    return acc


def _prep_kernel(xa_ref, xb_ref, meta_ref, g_ref, b_ref, o_ref, *, nbx):
    j = pl.program_id(1)
    head = jnp.where(j == 0, meta_ref[...], xa_ref[0])
    body = _ln(xb_ref[0, 0:BLK - N_META, :], g_ref[...], b_ref[...])
    o_ref[0, 0:N_META, :] = jnp.where(j <= nbx, _ln(head, g_ref[...], b_ref[...]), 0.0)
    o_ref[0, N_META:BLK, :] = jnp.where(j < nbx, body, 0.0)


def _prep(x, meta, g, b, lp):
    bsz, seq, d = x.shape
    nbx = seq // BLK
    per = BLK // N_META
    return pl.pallas_call(
        functools.partial(_prep_kernel, nbx=nbx),
        grid=(bsz, lp // BLK),
        in_specs=[
            pl.BlockSpec((1, N_META, d), lambda i, j: (i, jnp.clip(per * j - 1, 0, per * nbx - 1), 0)),
            pl.BlockSpec((1, BLK, d), lambda i, j: (i, jnp.minimum(j, nbx - 1), 0)),
            pl.BlockSpec((N_META, d), lambda i, j: (0, 0)),
            pl.BlockSpec((1, d), lambda i, j: (0, 0)),
            pl.BlockSpec((1, d), lambda i, j: (0, 0)),
        ],
        out_specs=pl.BlockSpec((1, BLK, d), lambda i, j: (i, j, 0)),
        out_shape=jax.ShapeDtypeStruct((bsz, lp, d), F32),
        compiler_params=_params("parallel", "parallel"),
        name="prep_ln",
    )(x, x, meta, g.reshape(1, d), b.reshape(1, d))


def _inproj_kernel(h_ref, w_ref, wvt_ref, ua_ref, qd_ref, kd_ref, vdt_ref, uc_ref, ud_ref):
    hb = h_ref[...].astype(BF16)

    def proj(c0, c1):
        return _dot(hb, w_ref[:, c0:c1])

    ua_ref[0, :, 0:256] = (proj(0, 256) * (HEAD_DIM ** -0.5)).astype(BF16)
    ua_ref[0, :, 256:512] = proj(256, 512).astype(BF16)
    qd_ref[0] = (proj(512, 768) * (DIFF_QK ** -0.5 * LOG2E)).astype(BF16)
    kd_ref[0] = proj(768, 1024).astype(BF16)
    vdt_ref[0, 0] = _dot_nt(wvt_ref[...], hb).astype(BF16)
    uc_ref[0] = proj(1280, 2304)
    ud_ref[0] = proj(2304, 2816)


def _inproj(h, w_bf, wvt_bf, bsz, lp):
    d = h.shape[1]
    tm = TQ
    nt = lp // tm
    row = lambda w, dt: jax.ShapeDtypeStruct((bsz, lp, w), dt)
    spec = lambda w: pl.BlockSpec((1, tm, w), lambda i, j: (i, j, 0))
    return pl.pallas_call(
        _inproj_kernel,
        grid=(bsz, nt),
        in_specs=[
            pl.BlockSpec((tm, d), lambda i, j: (i * nt + j, 0)),
            pl.BlockSpec((d, IN_WIDTH), lambda i, j: (0, 0)),
            pl.BlockSpec((GROUP_WIDTH, d), lambda i, j: (0, 0)),
        ],
        out_specs=[
            spec(512), spec(256), spec(256),
            pl.BlockSpec((1, 1, GROUP_WIDTH, tm), lambda i, j: (i, j, 0, 0)),
            spec(1024), spec(512),
        ],
        out_shape=[
            row(512, BF16), row(256, BF16), row(256, BF16),
            jax.ShapeDtypeStruct((bsz, nt, GROUP_WIDTH, tm), BF16),
            row(1024, F32), row(512, F32),
        ],
        compiler_params=_params("parallel", "parallel"),
        name="in_proj",
    )(h, w_bf, wvt_bf)


def _swa_kernel(rel_ref, sink_ref, bband_ref, bmeta_ref, cur_ref, prev_ref, meta_ref, o_ref, band_sc, metab_sc):
    j = pl.program_id(1)

    @pl.when(j == 0)
    def _():
        for h in range(N_HEADS):
            band_sc[h] = _gather_bias(bband_ref[...], rel_ref, h)

    @pl.when(j <= 2)
    def _():
        bkt = bmeta_ref[jnp.minimum(j, 2)]
        for h in range(N_HEADS):
            metab_sc[h] = _gather_bias(bkt, rel_ref, h)

    r = lax.broadcasted_iota(jnp.int32, (BLK, 1), 0)
    qpos = j * BLK + r
    ok_meta = qpos >= lax.broadcasted_iota(jnp.int32, (1, N_META), 1)
    dq = r - lax.broadcasted_iota(jnp.int32, (1, 2 * BLK), 1) + BLK
    ok_band = (dq >= 0) & (dq < BLK) & (qpos - dq >= N_META)

    cur = cur_ref[0]
    prev = prev_ref[0]
    meta = meta_ref[0, 0:N_META, :]
    outs = []
    for h in range(N_HEADS):
        g = h // (N_HEADS // SWA_KV_HEADS)
        kc, vc = 256 + HEAD_DIM * g, 384 + HEAD_DIM * g
        q = cur[:, HEAD_DIM * h:HEAD_DIM * (h + 1)]
        s_m = _dot_nt(q, meta[:, kc:kc + HEAD_DIM]) + metab_sc[h]
        s_p = _dot_nt(q, prev[:, kc:kc + HEAD_DIM]) + band_sc[h, :, 0:BLK]
        s_c = _dot_nt(q, cur[:, kc:kc + HEAD_DIM]) + band_sc[h, :, BLK:2 * BLK]
        s_m = jnp.where(ok_meta, s_m, NEG)
        s_p = jnp.where(ok_band[:, 0:BLK], s_p, NEG)
        s_c = jnp.where(ok_band[:, BLK:2 * BLK], s_c, NEG)
        sink = sink_ref[h]
        m = jnp.maximum(jnp.maximum(jnp.max(s_m, -1, keepdims=True), jnp.max(s_p, -1, keepdims=True)),
                        jnp.maximum(jnp.max(s_c, -1, keepdims=True), sink))
        p_m, p_p, p_c = jnp.exp(s_m - m), jnp.exp(s_p - m), jnp.exp(s_c - m)
        den = (jnp.sum(p_m, -1, keepdims=True) + jnp.sum(p_p, -1, keepdims=True)
               + jnp.sum(p_c, -1, keepdims=True) + jnp.exp(sink - m))
        o = (_dot(p_m.astype(BF16), meta[:, vc:vc + HEAD_DIM]) + _dot(p_p.astype(BF16), prev[:, vc:vc + HEAD_DIM])
             + _dot(p_c.astype(BF16), cur[:, vc:vc + HEAD_DIM]))
        outs.append(o / den)
    o_ref[0] = jnp.concatenate(outs, axis=1).astype(BF16)


def _swa(ua, rel_a, sinks, bkt_band, bkt_meta):
    bsz, lp, _ = ua.shape
    nb = lp // BLK
    smem = pl.BlockSpec(memory_space=pltpu.SMEM)
    return pl.pallas_call(
        _swa_kernel,
        grid=(bsz, nb),
        in_specs=[
            smem, smem,
            pl.BlockSpec((BLK, 2 * BLK), lambda i, j: (0, 0)),
            pl.BlockSpec((3, BLK, N_META), lambda i, j: (0, 0, 0)),
            pl.BlockSpec((1, BLK, SWA_IN), lambda i, j: (i, j, 0)),
            pl.BlockSpec((1, BLK, SWA_IN), lambda i, j: (i, jnp.maximum(j - 1, 0), 0)),
            pl.BlockSpec((1, BLK, SWA_IN), lambda i, j: (i, 0, 0)),
        ],
        out_specs=pl.BlockSpec((1, BLK, GROUP_WIDTH), lambda i, j: (i, j, 0)),
        out_shape=jax.ShapeDtypeStruct((bsz, lp, GROUP_WIDTH), BF16),
        scratch_shapes=[pltpu.VMEM((N_HEADS, BLK, 2 * BLK), F32), pltpu.VMEM((N_HEADS, BLK, N_META), F32)],
        compiler_params=_params("arbitrary", "arbitrary"),
        name="swa_attn",
    )(rel_a, sinks, bkt_band, bkt_meta, ua, ua, ua)


def _diff_kernel(lam_ref, rel_ref, bkt_ref, g_ref, q_ref, k_ref, vt_ref, o_ref, wt_sc, b1_sc, addm_sc, m_sc, l_sc,
                 acc_sc):
    i = pl.program_id(1)
    nsub = TQ // BLK

    @pl.when(i == 0)
    def _():
        kr = lax.broadcasted_iota(jnp.int32, (BLK, 1), 0)
        qc = lax.broadcasted_iota(jnp.int32, (1, BLK), 1)
        for h in range(N_HEADS):
            far = rel_ref[REL_BUCKETS - 1, h]
            b0 = jnp.where(qc >= kr, _gather_bias(bkt_ref[0], rel_ref, h, far) * LOG2E, NEG)
            b1 = _gather_bias(bkt_ref[1], rel_ref, h, far) * LOG2E
            b1_sc[h] = b1
            blocks = {0: b0, 1: b1, 2: jnp.zeros((BLK, BLK), F32)}
            masked = jnp.full((BLK, BLK), NEG, F32)
            for u in range(nsub):
                row = [blocks[w - u] if w >= u else masked for w in range(nsub)]
                addm_sc[h, u * BLK:(u + 1) * BLK, :] = jnp.concatenate(row + row, axis=1)

    q = q_ref[0]
    lane_grp = lax.broadcasted_iota(jnp.int32, (1, GROUP_WIDTH), 1) // DIFF_QK
    zero = jnp.zeros_like(q)
    for h in range(N_HEADS):
        wt_sc[h, 0:TQ, :] = jnp.where(lane_grp == 2 * h, q, zero)
        wt_sc[h, TQ:2 * TQ, :] = jnp.where(lane_grp == 2 * h + 1, q, zero)
    m_sc[...] = jnp.full(m_sc.shape, NEG, F32)
    l_sc[...] = jnp.zeros(l_sc.shape, F32)
    acc_sc[...] = jnp.zeros(acc_sc.shape, F32)

    def tile(t, kind):
        kt = k_ref[0, pl.ds(pl.multiple_of(t * TQ, TQ), TQ), :]
        vt = vt_ref[0, t]
        for h in range(N_HEADS):
            s = _dot_nt(kt, wt_sc[h])
            if kind == "diag":
                s = s + addm_sc[h]
            elif kind == "near":
                top, bot = s[0:TQ - BLK], s[TQ - BLK:TQ]
                b1 = b1_sc[h]
                bot = jnp.concatenate([bot[:, 0:BLK] + b1, bot[:, BLK:TQ], bot[:, TQ:TQ + BLK] + b1,
                                       bot[:, TQ + BLK:2 * TQ]], axis=1)
                s = jnp.concatenate([top, bot], axis=0)
            m_old = m_sc[h]
            m_new = jnp.maximum(m_old, jnp.max(s, 0, keepdims=True))
            a = jnp.exp2(m_old - m_new)
            p = jnp.exp2(s - m_new)
            l_sc[h] = a * l_sc[h] + jnp.sum(p, 0, keepdims=True)
            pv = _dot(vt[HEAD_DIM * h:HEAD_DIM * (h + 1), :], p.astype(BF16))
            acc_sc[h] = a * acc_sc[h] + pv
            m_sc[h] = m_new

    def far(t, c):
        tile(t, None)
        return c

    lax.fori_loop(0, i - 1, far, 0)

    @pl.when(i >= 1)
    def _():
        tile(i - 1, "near")

    tile(i, "diag")

    lam = lam_ref[0]
    post = lam_ref[1]
    outs = []
    for h in range(N_HEADS):
        o = acc_sc[h] / l_sc[h]
        o = o[:, 0:TQ] - lam * o[:, TQ:2 * TQ]
        ms = jnp.mean(o * o, 0, keepdims=True)
        outs.append(o * lax.rsqrt(ms + 1e-5) * g_ref[...] * post)
    ot = jnp.concatenate(outs, axis=0).astype(BF16)
    eye = (lax.broadcasted_iota(jnp.int32, (TQ, TQ), 0) == lax.broadcasted_iota(jnp.int32, (TQ, TQ), 1))
    o_ref[0] = _dot_nt(eye.astype(BF16), ot).astype(BF16)


def _diff(qd, kd, vdt, lam2, rel_b, bkt_d, g_b):
    bsz, lp, _ = qd.shape
    nq = lp // TQ
    smem = pl.BlockSpec(memory_space=pltpu.SMEM)
    return pl.pallas_call(
        _diff_kernel,
        grid=(bsz, nq),
        in_specs=[
            smem, smem,
            pl.BlockSpec((2, BLK, BLK), lambda b, i: (0, 0, 0)),
            pl.BlockSpec((HEAD_DIM, 1), lambda b, i: (0, 0)),
            pl.BlockSpec((1, TQ, GROUP_WIDTH), lambda b, i: (b, i, 0)),
            pl.BlockSpec((1, lp, GROUP_WIDTH), lambda b, i: (b, 0, 0)),
            pl.BlockSpec((1, nq, GROUP_WIDTH, TQ), lambda b, i: (b, 0, 0, 0)),
        ],
        out_specs=pl.BlockSpec((1, TQ, GROUP_WIDTH), lambda b, i: (b, i, 0)),
        out_shape=jax.ShapeDtypeStruct((bsz, lp, GROUP_WIDTH), BF16),
        scratch_shapes=[
            pltpu.VMEM((N_HEADS, 2 * TQ, GROUP_WIDTH), BF16),
            pltpu.VMEM((N_HEADS, BLK, BLK), F32),
            pltpu.VMEM((N_HEADS, TQ, 2 * TQ), F32),
            pltpu.VMEM((N_HEADS, 1, 2 * TQ), F32),
            pltpu.VMEM((N_HEADS, 1, 2 * TQ), F32),
            pltpu.VMEM((N_HEADS, HEAD_DIM, 2 * TQ), F32),
        ],
        compiler_params=_params("arbitrary", "arbitrary"),
        name="diff_attn",
    )(lam2, rel_b, bkt_d, g_b, qd, kd, vdt)


def _rwkv_kernel(x_ref, xp_ref, mu_ref, w0_ref, w2_ref, a0_ref, a2_ref, g2_ref, kk_ref, ka_ref, rk_ref,
                 lg_ref, lb_ref, o_ref, zt_sc):
    c = pl.program_id(1)
    n = CHUNK
    ns = RSTEP // CHUNK
    gw = GROUP_WIDTH

    @pl.when(c == 0)
    def _():
        zt_sc[...] = jnp.zeros(zt_sc.shape, F32)

    x = x_ref[0]
    row = lax.broadcasted_iota(jnp.int32, (RSTEP, 1), 0)
    last_prev = jnp.where(c == 0, 0.0, xp_ref[0, 7:8, :])
    xs = jnp.where(row == 0, last_prev, pltpu.roll(x, 1, 0))
    xm = x + mu_ref[...] * (xs - x)
    r, k, v = xm[:, 0:gw], xm[:, gw:2 * gw], xm[:, 2 * gw:3 * gw]
    xw = xm[:, 768:768 + DECAY_LORA]
    xa = xm[:, 832:832 + AAA_LORA]
    xg = xm[:, 896:896 + GATE_LORA]
    logw = -math.exp(-0.5) * _sigmoid(w0_ref[...] + _dot3(jnp.tanh(xw), w2_ref[...]))
    a = _sigmoid(a0_ref[...] + _dot3(xa, a2_ref[...]))
    g = _dot3(_sigmoid(xg), g2_ref[...])

    lane_head = lax.broadcasted_iota(jnp.int32, (1, gw), 1) // HEAD_DIM
    sub_head = lax.broadcasted_iota(jnp.int32, (gw, 1), 0) // HEAD_DIM
    head_ones = (sub_head == lane_head).astype(BF16)

    kk = k * kk_ref[...]
    kk = kk / jnp.maximum(jnp.sqrt(_dot_exact_rhs(kk * kk, head_ones)), 1e-12)
    k2 = k * (1.0 + (a - 1.0) * ka_ref[...])
    bvec = kk * a

    ri = lax.broadcasted_iota(jnp.int32, (RSTEP, RSTEP), 0)
    ci = lax.broadcasted_iota(jnp.int32, (RSTEP, RSTEP), 1)
    tri = (((ri // n) == (ci // n)) & (ri >= ci)).astype(BF16)
    cum = _dot_exact_lhs(tri, logw)
    tot = jnp.concatenate([jnp.broadcast_to(cum[(s + 1) * n - 1:(s + 1) * n, :], (n, gw)) for s in range(ns)], axis=0)
    e_neg = jnp.exp(-cum)
    e_last = jnp.exp(tot - cum)
    a_t = (-kk * jnp.exp(cum - logw)).astype(BF16)
    r_t = (r * jnp.exp(cum)).astype(BF16)
    b_t, k_t = (bvec * e_neg).astype(BF16), (k2 * e_neg).astype(BF16)
    b_h, k_h = (bvec * e_last).astype(BF16), (k2 * e_last).astype(BF16)
    g_c = jnp.exp(tot)
    vb = v.astype(BF16)

    def stack(t, s):
        ts = t[s * n:(s + 1) * n]
        return jnp.concatenate([jnp.where(lane_head == h, ts, jnp.zeros_like(ts)) for h in range(N_HEADS)], axis=0)

    rr = lax.broadcasted_iota(jnp.int32, (gw, gw), 0)
    cc = lax.broadcasted_iota(jnp.int32, (gw, gw), 1)
    parts = []
    for s in range(ns):
        ast, rst, bst, kst = stack(a_t, s), stack(r_t, s), stack(b_t, s), stack(k_t, s)
        gmat = _dot_nt(jnp.concatenate([ast, rst], axis=0), jnp.concatenate([bst, kst], axis=0))
        low = jnp.where(rr > cc, gmat[0:gw, 0:gw], 0.0)
        aak = jnp.where(rr > cc, gmat[0:gw, gw:2 * gw], 0.0).astype(BF16)
        arb = jnp.where(rr >= cc, gmat[gw:2 * gw, 0:gw], 0.0).astype(BF16)
        ark = jnp.where(rr >= cc, gmat[gw:2 * gw, gw:2 * gw], 0.0).astype(BF16)
        pw = low
        tinv = jnp.where(rr == cc, 1.0, low)
        for _ in range(5):
            pwb = pw.astype(BF16)
            pw = _dot(pwb, pwb)
            tinv = tinv + _dot(tinv.astype(BF16), pw.astype(BF16))
        parts.append((ast, rst, aak, arb, ark, tinv.astype(BF16), stack(b_h, s), stack(k_h, s), stack(vb, s)))

    zt = zt_sc[...]
    ys = []
    for s in range(ns):
        ast, rst, aak, arb, ark, tinv, bhs, khs, vst = parts[s]
        ztb = zt.astype(BF16)
        u = _dot(tinv, (_dot_nt(ast, ztb) + _dot(aak, vst)).astype(BF16)).astype(BF16)
        ybd = _dot_nt(rst, ztb) + _dot(arb, u) + _dot(ark, vst)
        zt = zt * g_c[s * n:s * n + 1, :] + _dot_tn(u, bhs) + _dot_tn(vst, khs)
        ys.append(ybd[0:n] + ybd[n:2 * n] + ybd[2 * n:3 * n] + ybd[3 * n:4 * n])
    zt_sc[...] = zt
    y = jnp.concatenate(ys, axis=0)

    mean = _dot_exact_rhs(y, head_ones) * (1.0 / HEAD_DIM)
    yc = y - mean
    var = _dot_exact_rhs(yc * yc, head_ones) * (1.0 / HEAD_DIM)
    yn = yc * lax.rsqrt(var + 64e-5) * lg_ref[...] + lb_ref[...]
    bonus = _dot_exact_rhs(r * k2 * rk_ref[...], head_ones) * v
    o_ref[0] = ((yn + bonus) * g).astype(BF16)


def _rwkv(uc, mu, w0, w2, a0, a2, g2, k_k, k_a, r_k, lnx_g, lnx_b):
    bsz, lp, w = uc.shape
    nc = lp // RSTEP
    vec = lambda t: t.reshape(1, -1)
    full = lambda t: pl.BlockSpec(t.shape, lambda b, c: (0,) * t.ndim)
    args = [vec(mu), vec(w0), w2, vec(a0), a2, g2, vec(k_k), vec(k_a), vec(r_k), vec(lnx_g), vec(lnx_b)]
    return pl.pallas_call(
        _rwkv_kernel,
        grid=(bsz, nc),
        in_specs=[
            pl.BlockSpec((1, RSTEP, w), lambda b, c: (b, c, 0)),
            pl.BlockSpec((1, 8, w), lambda b, c: (b, jnp.maximum(c * (RSTEP // 8) - 1, 0), 0)),
        ] + [full(t) for t in args],
        out_specs=pl.BlockSpec((1, RSTEP, GROUP_WIDTH), lambda b, c: (b, c, 0)),
        out_shape=jax.ShapeDtypeStruct((bsz, lp, GROUP_WIDTH), BF16),
        scratch_shapes=[pltpu.VMEM((GROUP_WIDTH, GROUP_WIDTH), F32)],
        compiler_params=_params("arbitrary", "arbitrary"),
        name="rwkv7",
    )(uc, uc, *args)


def _conv_kernel(cur_ref, prev_ref, w_ref, b_ref, g_ref, gb_ref, o_ref, hcat_sc):
    j = pl.program_id(1)
    ch = GROUP_WIDTH

    def glu(t):
        return t[:, 0:ch] * _sigmoid(t[:, ch:2 * ch])

    hcat_sc[0:BLK, :] = jnp.where(j == 0, 0.0, glu(prev_ref[0]))
    hcat_sc[BLK:2 * BLK, :] = glu(cur_ref[0])
    acc = jnp.zeros((BLK, ch), F32) + b_ref[...]
    for t in range(CONV_WIDTH):
        lo = BLK - (CONV_WIDTH - 1) + t
        acc = acc + hcat_sc[lo:lo + BLK, :] * w_ref[t:t + 1, :]
    lane_grp = lax.broadcasted_iota(jnp.int32, (1, ch), 1) // HEAD_DIM
    sub_grp = lax.broadcasted_iota(jnp.int32, (ch, 1), 0) // HEAD_DIM
    grp_ones = (sub_grp == lane_grp).astype(BF16)
    mean = _dot_exact_rhs(acc, grp_ones) * (1.0 / HEAD_DIM)
    xc = acc - mean
    var = _dot_exact_rhs(xc * xc, grp_ones) * (1.0 / HEAD_DIM)
    y = xc * lax.rsqrt(var + 1e-5) * g_ref[...] + gb_ref[...]
    o_ref[0] = (y * _sigmoid(y)).astype(BF16)


def _conv(ud, conv_w, conv_b, gn_g, gn_b):
    bsz, lp, w = ud.shape
    nb = lp // BLK
    vec = lambda t: t.reshape(1, -1)
    const = lambda shape: pl.BlockSpec(shape, lambda i, j: (0, 0))
    return pl.pallas_call(
        _conv_kernel,
        grid=(bsz, nb),
        in_specs=[
            pl.BlockSpec((1, BLK, w), lambda i, j: (i, j, 0)),
            pl.BlockSpec((1, BLK, w), lambda i, j: (i, jnp.maximum(j - 1, 0), 0)),
            const((CONV_WIDTH, GROUP_WIDTH)), const((1, GROUP_WIDTH)), const((1, GROUP_WIDTH)),
            const((1, GROUP_WIDTH)),
        ],
        out_specs=pl.BlockSpec((1, BLK, GROUP_WIDTH), lambda i, j: (i, j, 0)),
        out_shape=jax.ShapeDtypeStruct((bsz, lp, GROUP_WIDTH), BF16),
        scratch_shapes=[pltpu.VMEM((2 * BLK, GROUP_WIDTH), F32)],
        compiler_params=_params("parallel", "arbitrary"),
        name="conv_module",
    )(ud, ud, conv_w, vec(conv_b), vec(gn_g), vec(gn_b))


def _router(logit_t, rb_ref):
    s = [_sigmoid(logit_t[e:e + 1, :]) for e in range(N_EXPERTS)]
    bz = [s[e] + rb_ref[e:e + 1, 0:1] for e in range(N_EXPERTS)]
    per = N_EXPERTS // N_GROUPS
    gsum = []
    for gi in range(N_GROUPS):
        a, b, c, d = bz[per * gi:per * gi + per]
        hi1, lo1, hi2, lo2 = jnp.maximum(a, b), jnp.minimum(a, b), jnp.maximum(c, d), jnp.minimum(c, d)
        gsum.append(jnp.maximum(hi1, hi2) + jnp.maximum(jnp.minimum(hi1, hi2), jnp.maximum(lo1, lo2)))
    best = jnp.zeros_like(gsum[0], dtype=jnp.int32)
    bval = gsum[0]
    for gi in range(1, N_GROUPS):
        take = gsum[gi] > bval
        best = jnp.where(take, gi, best)
        bval = jnp.where(take, gsum[gi], bval)

    def pick(vals, i):
        out = vals[i]
        for gi in range(1, N_GROUPS):
            out = jnp.where(best == gi, vals[per * gi + i], out)
        return out

    bv = [pick(bz, i) for i in range(per)]
    sv = [pick(s, i) for i in range(per)]
    i1 = jnp.zeros_like(best)
    v1 = bv[0]
    for i in range(1, per):
        take = bv[i] > v1
        i1 = jnp.where(take, i, i1)
        v1 = jnp.where(take, bv[i], v1)
    i2 = jnp.full_like(best, -1)
    v2 = jnp.full_like(v1, -jnp.inf)
    for i in range(per):
        take = (i1 != i) & (bv[i] > v2)
        i2 = jnp.where(take, i, i2)
        v2 = jnp.where(take, bv[i], v2)
    s1 = sv[0]
    s2 = sv[0]
    for i in range(1, per):
        s1 = jnp.where(i1 == i, sv[i], s1)
        s2 = jnp.where(i2 == i, sv[i], s2)
    tot = s1 + s2
    g1, g2 = s1 / tot, s2 / tot
    rows = []
    for e in range(N_EXPERTS):
        gi, i = divmod(e, per)
        hit = jnp.where(best == gi, jnp.where(i1 == i, g1, jnp.where(i2 == i, g2, 0.0)), 0.0)
        rows.append(hit)
    return jnp.concatenate(rows, axis=0)


def _outproj_kernel(ya_ref, yb_ref, yc_ref, yd_ref, w_ref, h_ref, g_ref, b_ref, rwt_ref, rb_ref, o_ref, comb_ref, *,
                    tm):
    gw = GROUP_WIDTH
    mix = (_dot(ya_ref[0], w_ref[0:gw, :]) + _dot(yb_ref[0], w_ref[gw:2 * gw, :])
           + _dot(yc_ref[0], w_ref[2 * gw:3 * gw, :]) + _dot(yd_ref[0], w_ref[3 * gw:4 * gw, :]))
    h1 = _ln(ALPHA * h_ref[...] + mix, g_ref[...], b_ref[...])
    o_ref[...] = h1
    comb_t = _router(_dot_nt(rwt_ref[...], h1, HI), rb_ref)
    eye = (lax.broadcasted_iota(jnp.int32, (tm, tm), 0) == lax.broadcasted_iota(jnp.int32, (tm, tm), 1))
    comb_ref[...] = _dot_nt(eye.astype(F32), comb_t, HI)


def _outproj(ya, yb, yc, yd, w_bf, h, g, b, rwt, rb_b):
    bsz, lp, _ = ya.shape
    d = h.shape[1]
    tm = TQ
    nt = lp // tm
    spec = lambda w: pl.BlockSpec((1, tm, w), lambda i, j: (i, j, 0))
    flat = lambda w: pl.BlockSpec((tm, w), lambda i, j: (i * nt + j, 0))
    const = lambda shape: pl.BlockSpec(shape, lambda i, j: (0, 0))
    return pl.pallas_call(
        functools.partial(_outproj_kernel, tm=tm),
        grid=(bsz, nt),
        in_specs=[spec(GROUP_WIDTH)] * 4 + [const((d, d)), flat(d), const((1, d)), const((1, d)),
                                            const((N_EXPERTS, d)), const((N_EXPERTS, BLK))],
        out_specs=[flat(d), flat(N_EXPERTS)],
        out_shape=[jax.ShapeDtypeStruct((bsz * lp, d), F32), jax.ShapeDtypeStruct((bsz * lp, N_EXPERTS), F32)],
        compiler_params=_params("parallel", "parallel"),
        name="out_proj_ln_router",
    )(ya, yb, yc, yd, w_bf, h, g.reshape(1, d), b.reshape(1, d), rwt, rb_b)


def _route_meta(comb, tm):
    t = comb.shape[0]
    per = N_EXPERTS // N_GROUPS
    cg = comb.reshape(t, N_GROUPS, per)
    grp = jnp.argmax(jnp.sum(cg, -1), axis=-1).astype(jnp.int32)
    comb4 = jnp.take_along_axis(cg, grp[:, None, None], axis=1)[:, 0]
    order = jnp.argsort(grp, stable=True).astype(jnp.int32)
    counts = jnp.sum((grp[:, None] == jnp.arange(N_GROUPS)[None, :]).astype(jnp.int32), axis=0)
    start = jnp.cumsum(counts) - counts
    pcounts = (counts + tm - 1) // tm * tm
    pend = jnp.cumsum(pcounts)
    pstart = pend - pcounts
    nt = t // tm + N_GROUPS
    s = jnp.arange(nt * tm, dtype=jnp.int32)
    gs = jnp.minimum(jnp.searchsorted(pend, s, side="right"), N_GROUPS - 1).astype(jnp.int32)
    r = s - pstart[gs]
    valid = r < counts[gs]
    tok = order[jnp.clip(start[gs] + r, 0, t - 1)]
    src = jnp.where(valid, tok, -1)
    c4 = jnp.where(valid[:, None], comb4[tok], 0.0)
    return src, gs[::tm], c4, nt


def _moe_kernel(src_ref, tg_ref, h_hbm, c_ref, w13_ref, w2_ref, g_ref, b_ref, o_hbm, xbuf, obuf, gsem, ssem, *,
                tm, nt, t_rows):
    i = pl.program_id(0)
    slot = lax.rem(i, 2)
    ff = (N_EXPERTS // N_GROUPS) * EXPERT_FF

    def gather(tile, sl):
        base = tile * tm
        for r in range(tm):
            tok = jnp.maximum(src_ref[base + r], 0)
            pltpu.make_async_copy(h_hbm.at[pl.ds(tok, 1)], xbuf.at[sl, pl.ds(r, 1)], gsem.at[sl]).start()

    def scatter(sl):
        base = i * tm
        for r in range(tm):
            s = src_ref[base + r]
            dst = jnp.where(s >= 0, s, t_rows + sl * tm + r)
            pltpu.make_async_copy(obuf.at[sl, pl.ds(r, 1)], o_hbm.at[pl.ds(dst, 1)], ssem.at[sl]).start()

    def wait_gather(sl):
        pltpu.make_async_copy(h_hbm.at[pl.ds(0, tm)], xbuf.at[sl], gsem.at[sl]).wait()

    def wait_scatter(sl):
        pltpu.make_async_copy(obuf.at[sl], o_hbm.at[pl.ds(0, tm)], ssem.at[sl]).wait()

    @pl.when(i == 0)
    def _():
        gather(0, 0)
        obuf[...] = jnp.zeros(obuf.shape, F32)
        for sl in range(2):
            spare = pltpu.make_async_copy(obuf.at[sl], o_hbm.at[pl.ds(t_rows + sl * tm, tm)], ssem.at[sl])
            spare.start()
            spare.wait()

    for sl in range(2):
        @pl.when((i + 1 < nt) & (slot == 1 - sl))
        def _(sl=sl):
            gather(i + 1, sl)

    wait_gather(slot)

    @pl.when(i >= 2)
    def _():
        wait_scatter(slot)

    x = xbuf[slot]
    hh = _dot(x.astype(BF16), w13_ref[0])
    a1, a3 = hh[:, 0:ff], hh[:, ff:2 * ff]
    c = c_ref[...]
    cexp = jnp.concatenate([jnp.broadcast_to(c[:, e:e + 1], (tm, EXPERT_FF)) for e in range(N_EXPERTS // N_GROUPS)],
                           axis=1)
    act = (a1 * _sigmoid(a1)) * a3 * cexp
    y = _dot(act.astype(BF16), w2_ref[0])
    obuf[slot] = _ln(ALPHA * x + y, g_ref[...], b_ref[...])

    for sl in range(2):
        @pl.when(slot == sl)
        def _(sl=sl):
            scatter(sl)

    @pl.when(i == nt - 1)
    def _():
        wait_scatter(slot)
        if nt >= 2:
            wait_scatter(1 - slot)


def _moe(h1, comb, w13g, w2g, g, b):
    t, d = h1.shape
    tm = MOE_TM
    src, tile_grp, c4, nt = _route_meta(comb, tm)
    ff = (N_EXPERTS // N_GROUPS) * EXPERT_FF
    grid_spec = pltpu.PrefetchScalarGridSpec(
        num_scalar_prefetch=2,
        grid=(nt,),
        in_specs=[
            pl.BlockSpec(memory_space=pl.ANY),
            pl.BlockSpec((tm, N_EXPERTS // N_GROUPS), lambda i, src, tg: (i, 0)),
            pl.BlockSpec((1, d, 2 * ff), lambda i, src, tg: (tg[i], 0, 0)),
            pl.BlockSpec((1, ff, d), lambda i, src, tg: (tg[i], 0, 0)),
            pl.BlockSpec((1, d), lambda i, src, tg: (0, 0)),
            pl.BlockSpec((1, d), lambda i, src, tg: (0, 0)),
        ],
        out_specs=pl.BlockSpec(memory_space=pl.ANY),
        scratch_shapes=[
            pltpu.VMEM((2, tm, d), F32), pltpu.VMEM((2, tm, d), F32),
            pltpu.SemaphoreType.DMA((2,)), pltpu.SemaphoreType.DMA((2,)),
        ],
    )
    return pl.pallas_call(
        functools.partial(_moe_kernel, tm=tm, nt=nt, t_rows=t),
        grid_spec=grid_spec,
        out_shape=jax.ShapeDtypeStruct((t + 2 * tm, d), F32),
        compiler_params=_params("arbitrary"),
        name="moe_ln",
    )(src, tile_grp, h1, c4, w13g, w2g, g.reshape(1, d), b.reshape(1, d))


def _group_weights(w1, w3, w2):
    e, d, ff = w1.shape
    per = e // N_GROUPS
    up = lambda w: w.reshape(N_GROUPS, per, d, ff).transpose(0, 2, 1, 3).reshape(N_GROUPS, d, per * ff)
    w13 = jnp.concatenate([up(w1), up(w3)], axis=-1).astype(BF16)
    return w13, w2.reshape(N_GROUPS, per * ff, d).astype(BF16)


def kernel(x, meta, ln0_g, ln0_b, w_in, swa_sinks, rel_bias, diff_lq1, diff_lk1, diff_lq2, diff_lk2, diff_subln_g,
           rwkv_mu, rwkv_w0, rwkv_w2, rwkv_a0, rwkv_a2, rwkv_g2, rwkv_kk, rwkv_ka, rwkv_rk, rwkv_lnx_g,
           rwkv_lnx_b, conv_w, conv_b, conv_gn_g, conv_gn_b, w_out, ln1_g, ln1_b, router_w, router_b, exp_w1,
           exp_w3, exp_w2, ln2_g, ln2_b):
    bsz, seq, d = x.shape
    assert d == D_MODEL and seq % BLK == 0
    lp = pl.cdiv(seq + N_META, TQ) * TQ

    r = jnp.arange(BLK)
    bkt_band = _t5_bucket(r[:, None] - jnp.arange(2 * BLK)[None, :] + BLK)
    qpos3 = jnp.arange(3 * BLK).reshape(3, BLK)
    bkt_meta = _t5_bucket(qpos3[:, :, None] - jnp.arange(N_META)[None, None, :])
    dq0 = r[None, :] - r[:, None]
    bkt_d = jnp.stack([_t5_bucket(dq0), _t5_bucket(dq0 + BLK)])
    rel_a, rel_b = rel_bias[:, :N_HEADS], rel_bias[:, N_HEADS:]
    rwt = router_w.T
    rb_b = jnp.broadcast_to(router_b.astype(F32)[:, None], (N_EXPERTS, BLK))
    vcols = slice(SWA_IN + 512, SWA_IN + 768)

    assert (bsz * lp) % MOE_TM == 0
    h = _prep(x, meta, ln0_g, ln0_b, lp).reshape(bsz * lp, d)
    for l in range(DEPTH):
        w_l = w_in[l]
        ua, qd, kd, vdt, uc, ud = _inproj(h, w_l.astype(BF16), w_l[:, vcols].T.astype(BF16), bsz, lp)
        ya = _swa(ua, rel_a, swa_sinks[l], bkt_band, bkt_meta)
        lam_init = 0.8 - 0.6 * math.exp(-0.3 * l)
        lam = (jnp.exp(jnp.sum(diff_lq1[l] * diff_lk1[l])) - jnp.exp(jnp.sum(diff_lq2[l] * diff_lk2[l])) + lam_init)
        lam2 = jnp.stack([lam, jnp.asarray(1.0 - lam_init, F32)]).astype(F32)
        yb = _diff(qd, kd, vdt, lam2, rel_b, bkt_d, diff_subln_g[l].reshape(HEAD_DIM, 1))
        yc = _rwkv(uc, rwkv_mu[l], rwkv_w0[l], rwkv_w2[l], rwkv_a0[l], rwkv_a2[l], rwkv_g2[l], rwkv_kk[l],
                   rwkv_ka[l], rwkv_rk[l], rwkv_lnx_g[l], rwkv_lnx_b[l])
        yd = _conv(ud, conv_w[l], conv_b[l], conv_gn_g[l], conv_gn_b[l])
        h1, comb = _outproj(ya, yb, yc, yd, w_out[l].astype(BF16), h, ln1_g[l], ln1_b[l], rwt, rb_b)
        w13g, w2g = _group_weights(exp_w1[l], exp_w3[l], exp_w2[l])
        h = _moe(h1, comb, w13g, w2g, ln2_g[l], ln2_b[l])
    return h[:bsz * lp].reshape(bsz, lp, d)[:, N_META:N_META + seq]
```

```python
import functools
import math

import jax
import jax.numpy as jnp
import numpy as np
from jax import lax
from jax.experimental import pallas as pl
from jax.experimental.pallas import tpu as pltpu

F32 = jnp.float32
BF16 = jnp.bfloat16

D_MODEL = 1024
N_META = 16
BLK = 128
TQ = 3 * BLK
HEAD_DIM = 64
GROUP_WIDTH = 256
N_HEADS = 4
SWA_KV_HEADS = 2
DIFF_QK = 32
DECAY_LORA = 64
AAA_LORA = 64
GATE_LORA = 128
CONV_WIDTH = 31
CONV_HIST = 32
REL_BUCKETS = 32
REL_MAX_DIST = 128
N_EXPERTS = 16
N_GROUPS = 4
EXPERT_FF = 512
DEPTH = 2
ALPHA = (2 * DEPTH) ** 0.25
SWA_IN = 512
DIFF_IN = 768
RWKV_IN = 1024
CONV_IN = 512
IN_WIDTH = SWA_IN + DIFF_IN + RWKV_IN + CONV_IN
NEG = -1e30
LOG2E = math.log2(math.e)
CHUNK = 64
RSTEP = TQ
MOE_TM = 256
VMEM_LIMIT = 56 * 1024 * 1024


def _dot(a, b, prec=None):
    return jnp.dot(a, b, preferred_element_type=F32, precision=prec)


def _dot_nt(a, b, prec=None):
    return lax.dot_general(a, b, (((1,), (1,)), ((), ())), preferred_element_type=F32, precision=prec)


def _dot_tn(a, b, prec=None):
    return lax.dot_general(a, b, (((0,), (0,)), ((), ())), preferred_element_type=F32, precision=prec)


def _split2(x):
    hi = x.astype(BF16)
    return hi, (x - hi.astype(F32)).astype(BF16)


def _dot_exact_rhs(x, m_bf):
    hi, lo = _split2(x)
    return _dot(hi, m_bf) + _dot(lo, m_bf)


def _dot_exact_lhs(m_bf, x):
    hi, lo = _split2(x)
    return _dot(m_bf, hi) + _dot(m_bf, lo)


def _dot3(a, b):
    ah, al = _split2(a)
    bh, bl = _split2(b)
    return _dot(ah, bh) + _dot(ah, bl) + _dot(al, bh)


def _ln(x, g, b, eps=1e-5):
    mu = jnp.mean(x, -1, keepdims=True)
    xc = x - mu
    var = jnp.mean(xc * xc, -1, keepdims=True)
    return xc * lax.rsqrt(var + eps) * g + b


def _sigmoid(x):
    return 1.0 / (1.0 + jnp.exp(-x))


def _pick_tile(n, candidates):
    for c in candidates:
        if n % c == 0:
            return c
    raise ValueError(f"no tile in {candidates} divides {n}")


def _params(*sem):
    return pltpu.CompilerParams(dimension_semantics=sem, vmem_limit_bytes=VMEM_LIMIT)


def _t5_bucket(dist):
    n = jnp.maximum(dist, 0)
    max_exact = REL_BUCKETS // 2
    log_ratio = jnp.log(jnp.maximum(n, 1).astype(F32) / max_exact) / math.log(REL_MAX_DIST / max_exact)
    large = jnp.minimum(max_exact + (log_ratio * (REL_BUCKETS - max_exact)).astype(jnp.int32), REL_BUCKETS - 1)
    return jnp.where(n < max_exact, n, large)


def _gather_bias(bkt, rel_ref, h, offset=0.0):
    acc = jnp.zeros(bkt.shape, F32)
    for b in range(REL_BUCKETS):
        acc = jnp.where(bkt == b, rel_ref[b, h] - offset, acc)
    return acc


def _prep_kernel(xa_ref, xb_ref, meta_ref, g_ref, b_ref, o_ref, *, nbx):
    j = pl.program_id(1)
    head = jnp.where(j == 0, meta_ref[...], xa_ref[0])
    body = _ln(xb_ref[0, 0:BLK - N_META, :], g_ref[...], b_ref[...])
    o_ref[0, 0:N_META, :] = jnp.where(j <= nbx, _ln(head, g_ref[...], b_ref[...]), 0.0)
    o_ref[0, N_META:BLK, :] = jnp.where(j < nbx, body, 0.0)


def _prep(x, meta, g, b, lp):
    bsz, seq, d = x.shape
    nbx = seq // BLK
    per = BLK // N_META
    return pl.pallas_call(
        functools.partial(_prep_kernel, nbx=nbx),
        grid=(bsz, lp // BLK),
        in_specs=[
            pl.BlockSpec((1, N_META, d), lambda i, j: (i, jnp.clip(per * j - 1, 0, per * nbx - 1), 0)),
            pl.BlockSpec((1, BLK, d), lambda i, j: (i, jnp.minimum(j, nbx - 1), 0)),
            pl.BlockSpec((N_META, d), lambda i, j: (0, 0)),
            pl.BlockSpec((1, d), lambda i, j: (0, 0)),
            pl.BlockSpec((1, d), lambda i, j: (0, 0)),
        ],
        out_specs=pl.BlockSpec((1, BLK, d), lambda i, j: (i, j, 0)),
        out_shape=jax.ShapeDtypeStruct((bsz, lp, d), F32),
        compiler_params=_params("parallel", "parallel"),
        name="prep_ln",
    )(x, x, meta, g.reshape(1, d), b.reshape(1, d))


def _inproj_kernel(h_ref, w_ref, wvt_ref, ua_ref, qd_ref, kd_ref, vdt_ref, uc_ref, ud_ref):
    hb = h_ref[...].astype(BF16)

    def proj(c0, c1):
        return _dot(hb, w_ref[:, c0:c1])

    ua_ref[0, :, 0:256] = (proj(0, 256) * (HEAD_DIM ** -0.5)).astype(BF16)
    ua_ref[0, :, 256:512] = proj(256, 512).astype(BF16)
    qd_ref[0] = (proj(512, 768) * (DIFF_QK ** -0.5 * LOG2E)).astype(BF16)
    kd_ref[0] = proj(768, 1024).astype(BF16)
    vdt_ref[0, 0] = _dot_nt(wvt_ref[...], hb).astype(BF16)
    uc_ref[0] = proj(1280, 2304)
    ud_ref[0] = proj(2304, 2816)


def _inproj(h, w_bf, wvt_bf, bsz, lp):
    d = h.shape[1]
    tm = TQ
    nt = lp // tm
    row = lambda w, dt: jax.ShapeDtypeStruct((bsz, lp, w), dt)
    spec = lambda w: pl.BlockSpec((1, tm, w), lambda i, j: (i, j, 0))
    return pl.pallas_call(
        _inproj_kernel,
        grid=(bsz, nt),
        in_specs=[
            pl.BlockSpec((tm, d), lambda i, j: (i * nt + j, 0)),
            pl.BlockSpec((d, IN_WIDTH), lambda i, j: (0, 0)),
            pl.BlockSpec((GROUP_WIDTH, d), lambda i, j: (0, 0)),
        ],
        out_specs=[
            spec(512), spec(256), spec(256),
            pl.BlockSpec((1, 1, GROUP_WIDTH, tm), lambda i, j: (i, j, 0, 0)),
            spec(1024), spec(512),
        ],
        out_shape=[
            row(512, BF16), row(256, BF16), row(256, BF16),
            jax.ShapeDtypeStruct((bsz, nt, GROUP_WIDTH, tm), BF16),
            row(1024, F32), row(512, F32),
        ],
        compiler_params=_params("parallel", "parallel"),
        name="in_proj",
    )(h, w_bf, wvt_bf)


def _swa_kernel(rel_ref, sink_ref, bband_ref, bmeta_ref, cur_ref, prev_ref, meta_ref, o_ref, band_sc, metab_sc):
    j = pl.program_id(1)

    @pl.when(j == 0)
    def _():
        for h in range(N_HEADS):
            band_sc[h] = _gather_bias(bband_ref[...], rel_ref, h)

        for n in range(3):
            for h in range(N_HEADS):
                metab_sc[n, h] = _gather_bias(bmeta_ref[n], rel_ref, h)

    nsub = TQ // BLK
    r = lax.broadcasted_iota(jnp.int32, (BLK, 1), 0)
    dq = r - lax.broadcasted_iota(jnp.int32, (1, 2 * BLK), 1) + BLK
    in_window = (dq >= 0) & (dq < BLK)
    meta = meta_ref[0, 0:N_META, :]
    for u in range(nsub):
        cur = cur_ref[0, u * BLK:(u + 1) * BLK, :]
        prev = prev_ref[0] if u == 0 else cur_ref[0, (u - 1) * BLK:u * BLK, :]
        qpos = (j * nsub + u) * BLK + r
        ok_meta = qpos >= lax.broadcasted_iota(jnp.int32, (1, N_META), 1)
        ok_band = in_window & (qpos - dq >= N_META)
        mtab = jnp.where(j == 0, u, 2)
        outs = []
        for h in range(N_HEADS):
            g = h // (N_HEADS // SWA_KV_HEADS)
            kc, vc = 256 + HEAD_DIM * g, 384 + HEAD_DIM * g
            q = cur[:, HEAD_DIM * h:HEAD_DIM * (h + 1)]
            kcat = jnp.concatenate([prev[:, kc:kc + HEAD_DIM], cur[:, kc:kc + HEAD_DIM]], axis=0)
            vcat = jnp.concatenate([prev[:, vc:vc + HEAD_DIM], cur[:, vc:vc + HEAD_DIM]], axis=0)
            s_m = jnp.where(ok_meta, _dot_nt(q, meta[:, kc:kc + HEAD_DIM]) + metab_sc[mtab, h], NEG)
            s_b = jnp.where(ok_band, _dot_nt(q, kcat) + band_sc[h], NEG)
            sink = sink_ref[h]
            m = jnp.maximum(jnp.maximum(jnp.max(s_m, -1, keepdims=True), jnp.max(s_b, -1, keepdims=True)), sink)
            p_m, p_b = jnp.exp(s_m - m), jnp.exp(s_b - m)
            den = jnp.sum(p_m, -1, keepdims=True) + jnp.sum(p_b, -1, keepdims=True) + jnp.exp(sink - m)
            o = _dot(p_m.astype(BF16), meta[:, vc:vc + HEAD_DIM]) + _dot(p_b.astype(BF16), vcat)
            outs.append(o / den)
        o_ref[0, u * BLK:(u + 1) * BLK, :] = jnp.concatenate(outs, axis=1).astype(BF16)


def _swa(ua, rel_a, sinks, bkt_band, bkt_meta):
    bsz, lp, _ = ua.shape
    nsub = TQ // BLK
    smem = pl.BlockSpec(memory_space=pltpu.SMEM)
    return pl.pallas_call(
        _swa_kernel,
        grid=(bsz, lp // TQ),
        in_specs=[
            smem, smem,
            pl.BlockSpec((BLK, 2 * BLK), lambda i, j: (0, 0)),
            pl.BlockSpec((3, BLK, N_META), lambda i, j: (0, 0, 0)),
            pl.BlockSpec((1, TQ, SWA_IN), lambda i, j: (i, j, 0)),
            pl.BlockSpec((1, BLK, SWA_IN), lambda i, j: (i, jnp.maximum(j * nsub - 1, 0), 0)),
            pl.BlockSpec((1, BLK, SWA_IN), lambda i, j: (i, 0, 0)),
        ],
        out_specs=pl.BlockSpec((1, TQ, GROUP_WIDTH), lambda i, j: (i, j, 0)),
        out_shape=jax.ShapeDtypeStruct((bsz, lp, GROUP_WIDTH), BF16),
        scratch_shapes=[pltpu.VMEM((N_HEADS, BLK, 2 * BLK), F32), pltpu.VMEM((3, N_HEADS, BLK, N_META), F32)],
        compiler_params=_params("arbitrary", "arbitrary"),
        name="swa_attn",
    )(rel_a, sinks, bkt_band, bkt_meta, ua, ua, ua)


def _diff_kernel(lam_ref, rel_ref, bkt_ref, g_ref, q_ref, k_ref, vt_ref, o_ref, wt_sc, b1_sc, addm_sc, m_sc, acc_sc):
    i = pl.program_id(1)
    nsub = TQ // BLK

    @pl.when(i == 0)
    def _():
        kr = lax.broadcasted_iota(jnp.int32, (BLK, 1), 0)
        qc = lax.broadcasted_iota(jnp.int32, (1, BLK), 1)
        for h in range(N_HEADS):
            far = rel_ref[REL_BUCKETS - 1, h]
            b0 = jnp.where(qc >= kr, _gather_bias(bkt_ref[0], rel_ref, h, far) * LOG2E, NEG)
            b1 = _gather_bias(bkt_ref[1], rel_ref, h, far) * LOG2E
            b1_sc[h] = b1
            blocks = {0: b0, 1: b1, 2: jnp.zeros((BLK, BLK), F32)}
            masked = jnp.full((BLK, BLK), NEG, F32)
            for u in range(nsub):
                row = [blocks[w - u] if w >= u else masked for w in range(nsub)]
                addm_sc[h, u * BLK:(u + 1) * BLK, :] = jnp.concatenate(row + row, axis=1)

    q = q_ref[0]
    lane_grp = lax.broadcasted_iota(jnp.int32, (1, GROUP_WIDTH), 1) // DIFF_QK
    zero = jnp.zeros_like(q)
    for h in range(N_HEADS):
        wt_sc[h, 0:TQ, :] = jnp.where(lane_grp == 2 * h, q, zero)
        wt_sc[h, TQ:2 * TQ, :] = jnp.where(lane_grp == 2 * h + 1, q, zero)
    m_sc[...] = jnp.full(m_sc.shape, NEG, F32)
    acc_sc[...] = jnp.zeros(acc_sc.shape, F32)

    ones_rows = jnp.ones((2 * 8, TQ), BF16)

    def scores(t, h):
        return _dot_nt(k_ref[0, pl.ds(pl.multiple_of(t * TQ, TQ), TQ), :], wt_sc[h])

    def consume(t, h, s, kind):
        if kind == "diag":
            s = s + addm_sc[h]
        elif kind == "near":
            top, bot = s[0:TQ - BLK], s[TQ - BLK:TQ]
            b1 = b1_sc[h]
            bot = jnp.concatenate([bot[:, 0:BLK] + b1, bot[:, BLK:TQ], bot[:, TQ:TQ + BLK] + b1,
                                   bot[:, TQ + BLK:2 * TQ]], axis=1)
            s = jnp.concatenate([top, bot], axis=0)
        m_old = m_sc[h]
        cm = jnp.maximum(jnp.maximum(s[0:BLK], s[BLK:2 * BLK]), s[2 * BLK:3 * BLK])
        n = BLK
        while n > 8:
            n //= 2
            cm = jnp.maximum(cm[0:n], cm[n:2 * n])
        m_new = jnp.maximum(m_old, jnp.max(cm, 0, keepdims=True))
        a = jnp.exp2(m_old - m_new)
        p = jnp.exp2(s - m_new).astype(BF16)
        vth = vt_ref[0, t, HEAD_DIM * h:HEAD_DIM * (h + 1), :]
        pv = _dot(jnp.concatenate([vth, ones_rows], axis=0), p)
        acc_sc[h] = a * acc_sc[h] + pv
        m_sc[h] = m_new

    def tile(t, kind):
        for h in range(N_HEADS):
            consume(t, h, scores(t, h), kind)

    def far(t, c):
        tile(t, None)
        return c

    lax.fori_loop(0, i - 1, far, 0)

    @pl.when(i >= 1)
    def _():
        tile(i - 1, "near")

    tile(i, "diag")

    lam = lam_ref[0]
    post = lam_ref[1]
    outs = []
    for h in range(N_HEADS):
        acc = acc_sc[h]
        o = acc[0:HEAD_DIM] / acc[HEAD_DIM:HEAD_DIM + 1]
        o = o[:, 0:TQ] - lam * o[:, TQ:2 * TQ]
        ms = jnp.mean(o * o, 0, keepdims=True)
        outs.append(o * lax.rsqrt(ms + 1e-5) * g_ref[...] * post)
    ot = jnp.concatenate(outs, axis=0).astype(BF16)
    eye = (lax.broadcasted_iota(jnp.int32, (TQ, TQ), 0) == lax.broadcasted_iota(jnp.int32, (TQ, TQ), 1))
    o_ref[0] = _dot_nt(eye.astype(BF16), ot).astype(BF16)


def _diff(qd, kd, vdt, lam2, rel_b, bkt_d, g_b):
    bsz, lp, _ = qd.shape
    nq = lp // TQ
    smem = pl.BlockSpec(memory_space=pltpu.SMEM)
    return pl.pallas_call(
        _diff_kernel,
        grid=(bsz, nq),
        in_specs=[
            smem, smem,
            pl.BlockSpec((2, BLK, BLK), lambda b, i: (0, 0, 0)),
            pl.BlockSpec((HEAD_DIM, 1), lambda b, i: (0, 0)),
            pl.BlockSpec((1, TQ, GROUP_WIDTH), lambda b, i: (b, i, 0)),
            pl.BlockSpec((1, lp, GROUP_WIDTH), lambda b, i: (b, 0, 0)),
            pl.BlockSpec((1, nq, GROUP_WIDTH, TQ), lambda b, i: (b, 0, 0, 0)),
        ],
        out_specs=pl.BlockSpec((1, TQ, GROUP_WIDTH), lambda b, i: (b, i, 0)),
        out_shape=jax.ShapeDtypeStruct((bsz, lp, GROUP_WIDTH), BF16),
        scratch_shapes=[
            pltpu.VMEM((N_HEADS, 2 * TQ, GROUP_WIDTH), BF16),
            pltpu.VMEM((N_HEADS, BLK, BLK), F32),
            pltpu.VMEM((N_HEADS, TQ, 2 * TQ), F32),
            pltpu.VMEM((N_HEADS, 1, 2 * TQ), F32),
            pltpu.VMEM((N_HEADS, HEAD_DIM + 2 * 8, 2 * TQ), F32),
        ],
        compiler_params=_params("arbitrary", "arbitrary"),
        name="diff_attn",
    )(lam2, rel_b, bkt_d, g_b, qd, kd, vdt)


def _rwkv_kernel(x_ref, xp_ref, mu_ref, w0_ref, w2_ref, a0_ref, a2_ref, g2_ref, kk_ref, ka_ref, rk_ref,
                 lg_ref, lb_ref, o_ref, zt_sc):
    c = pl.program_id(1)
    n = CHUNK
    ns = RSTEP // CHUNK
    gw = GROUP_WIDTH

    @pl.when(c == 0)
    def _():
        zt_sc[...] = jnp.zeros(zt_sc.shape, F32)

    x = x_ref[0]
    row = lax.broadcasted_iota(jnp.int32, (RSTEP, 1), 0)
    last_prev = jnp.where(c == 0, 0.0, xp_ref[0, 7:8, :])
    xs = jnp.where(row == 0, last_prev, pltpu.roll(x, 1, 0))
    xm = x + mu_ref[...] * (xs - x)
    r, k, v = xm[:, 0:gw], xm[:, gw:2 * gw], xm[:, 2 * gw:3 * gw]
    xw = xm[:, 768:768 + DECAY_LORA]
    xa = xm[:, 832:832 + AAA_LORA]
    xg = xm[:, 896:896 + GATE_LORA]
    logw = -math.exp(-0.5) * _sigmoid(w0_ref[...] + _dot3(jnp.tanh(xw), w2_ref[...]))
    a = _sigmoid(a0_ref[...] + _dot3(xa, a2_ref[...]))
    g = _dot3(_sigmoid(xg), g2_ref[...])

    lane_head = lax.broadcasted_iota(jnp.int32, (1, gw), 1) // HEAD_DIM
    sub_head = lax.broadcasted_iota(jnp.int32, (gw, 1), 0) // HEAD_DIM
    head_ones = (sub_head == lane_head).astype(BF16)

    kk = k * kk_ref[...]
    kk = kk / jnp.maximum(jnp.sqrt(_dot_exact_rhs(kk * kk, head_ones)), 1e-12)
    k2 = k * (1.0 + (a - 1.0) * ka_ref[...])
    bvec = kk * a

    ri = lax.broadcasted_iota(jnp.int32, (RSTEP, RSTEP), 0)
    ci = lax.broadcasted_iota(jnp.int32, (RSTEP, RSTEP), 1)
    tri = (((ri // n) == (ci // n)) & (ri >= ci)).astype(BF16)
    cum = _dot_exact_lhs(tri, logw)
    tot = jnp.concatenate([jnp.broadcast_to(cum[(s + 1) * n - 1:(s + 1) * n, :], (n, gw)) for s in range(ns)], axis=0)
    e_neg = jnp.exp(-cum)
    e_last = jnp.exp(tot - cum)
    a_t = (-kk * jnp.exp(cum - logw)).astype(BF16)
    r_t = (r * jnp.exp(cum)).astype(BF16)
    b_t, k_t = (bvec * e_neg).astype(BF16), (k2 * e_neg).astype(BF16)
    b_h, k_h = (bvec * e_last).astype(BF16), (k2 * e_last).astype(BF16)
    g_c = jnp.exp(tot)
    vb = v.astype(BF16)

    def stack(t, s):
        ts = t[s * n:(s + 1) * n]
        return jnp.concatenate([jnp.where(lane_head == h, ts, jnp.zeros_like(ts)) for h in range(N_HEADS)], axis=0)

    rr = lax.broadcasted_iota(jnp.int32, (gw, gw), 0)
    cc = lax.broadcasted_iota(jnp.int32, (gw, gw), 1)
    parts = []
    for s in range(ns):
        ast, rst, bst, kst = stack(a_t, s), stack(r_t, s), stack(b_t, s), stack(k_t, s)
        gmat = _dot_nt(jnp.concatenate([ast, rst], axis=0), jnp.concatenate([bst, kst], axis=0))
        low = jnp.where(rr > cc, gmat[0:gw, 0:gw], 0.0)
        aak = jnp.where(rr > cc, gmat[0:gw, gw:2 * gw], 0.0).astype(BF16)
        arb = jnp.where(rr >= cc, gmat[gw:2 * gw, 0:gw], 0.0).astype(BF16)
        ark = jnp.where(rr >= cc, gmat[gw:2 * gw, gw:2 * gw], 0.0).astype(BF16)
        pw = low
        tinv = jnp.where(rr == cc, 1.0, low)
        for _ in range(5):
            pwb = pw.astype(BF16)
            pw = _dot(pwb, pwb)
            tinv = tinv + _dot(tinv.astype(BF16), pw.astype(BF16))
        parts.append((ast, rst, aak, arb, ark, tinv.astype(BF16), stack(b_h, s), stack(k_h, s), stack(vb, s)))

    zt = zt_sc[...]
    ys = []
    for s in range(ns):
        ast, rst, aak, arb, ark, tinv, bhs, khs, vst = parts[s]
        ztb = zt.astype(BF16)
        u = _dot(tinv, (_dot_nt(ast, ztb) + _dot(aak, vst)).astype(BF16)).astype(BF16)
        ybd = _dot_nt(rst, ztb) + _dot(arb, u) + _dot(ark, vst)
        zt = zt * g_c[s * n:s * n + 1, :] + _dot_tn(u, bhs) + _dot_tn(vst, khs)
        ys.append(ybd[0:n] + ybd[n:2 * n] + ybd[2 * n:3 * n] + ybd[3 * n:4 * n])
    zt_sc[...] = zt
    y = jnp.concatenate(ys, axis=0)

    mean = _dot_exact_rhs(y, head_ones) * (1.0 / HEAD_DIM)
    yc = y - mean
    var = _dot_exact_rhs(yc * yc, head_ones) * (1.0 / HEAD_DIM)
    yn = yc * lax.rsqrt(var + 64e-5) * lg_ref[...] + lb_ref[...]
    bonus = _dot_exact_rhs(r * k2 * rk_ref[...], head_ones) * v
    o_ref[0] = ((yn + bonus) * g).astype(BF16)


def _rwkv(uc, mu, w0, w2, a0, a2, g2, k_k, k_a, r_k, lnx_g, lnx_b):
    bsz, lp, w = uc.shape
    nc = lp // RSTEP
    vec = lambda t: t.reshape(1, -1)
    full = lambda t: pl.BlockSpec(t.shape, lambda b, c: (0,) * t.ndim)
    args = [vec(mu), vec(w0), w2, vec(a0), a2, g2, vec(k_k), vec(k_a), vec(r_k), vec(lnx_g), vec(lnx_b)]
    return pl.pallas_call(
        _rwkv_kernel,
        grid=(bsz, nc),
        in_specs=[
            pl.BlockSpec((1, RSTEP, w), lambda b, c: (b, c, 0)),
            pl.BlockSpec((1, 8, w), lambda b, c: (b, jnp.maximum(c * (RSTEP // 8) - 1, 0), 0)),
        ] + [full(t) for t in args],
        out_specs=pl.BlockSpec((1, RSTEP, GROUP_WIDTH), lambda b, c: (b, c, 0)),
        out_shape=jax.ShapeDtypeStruct((bsz, lp, GROUP_WIDTH), BF16),
        scratch_shapes=[pltpu.VMEM((GROUP_WIDTH, GROUP_WIDTH), F32)],
        compiler_params=_params("arbitrary", "arbitrary"),
        name="rwkv7",
    )(uc, uc, *args)


def _conv_kernel(cur_ref, prev_ref, w_ref, b_ref, g_ref, gb_ref, o_ref, hcat_sc):
    j = pl.program_id(1)
    ch = GROUP_WIDTH

    def glu(t):
        return t[:, 0:ch] * _sigmoid(t[:, ch:2 * ch])

    hcat_sc[0:CONV_HIST, :] = jnp.where(j == 0, 0.0, glu(prev_ref[0]))
    hcat_sc[CONV_HIST:CONV_HIST + TQ, :] = glu(cur_ref[0])
    lead = CONV_HIST - (CONV_WIDTH - 1)
    acc = jnp.zeros((TQ, ch), F32) + b_ref[...]
    for ph in range(8):
        offs = [o for o in range(lead, CONV_HIST + 1) if o % 8 == ph]
        shifted = hcat_sc[ph:max(offs) + TQ, :]
        for o in offs:
            acc = acc + shifted[o - ph:o - ph + TQ] * w_ref[o - lead:o - lead + 1, :]
    lane_grp = lax.broadcasted_iota(jnp.int32, (1, ch), 1) // HEAD_DIM
    sub_grp = lax.broadcasted_iota(jnp.int32, (ch, 1), 0) // HEAD_DIM
    grp_ones = (sub_grp == lane_grp).astype(BF16)
    mean = _dot_exact_rhs(acc, grp_ones) * (1.0 / HEAD_DIM)
    xc = acc - mean
    var = _dot_exact_rhs(xc * xc, grp_ones) * (1.0 / HEAD_DIM)
    y = xc * lax.rsqrt(var + 1e-5) * g_ref[...] + gb_ref[...]
    o_ref[0] = (y * _sigmoid(y)).astype(BF16)


def _conv(ud, conv_w, conv_b, gn_g, gn_b):
    bsz, lp, w = ud.shape
    nb = lp // TQ
    vec = lambda t: t.reshape(1, -1)
    const = lambda shape: pl.BlockSpec(shape, lambda i, j: (0, 0))
    return pl.pallas_call(
        _conv_kernel,
        grid=(bsz, nb),
        in_specs=[
            pl.BlockSpec((1, TQ, w), lambda i, j: (i, j, 0)),
            pl.BlockSpec((1, CONV_HIST, w), lambda i, j: (i, jnp.maximum(j * (TQ // CONV_HIST) - 1, 0), 0)),
            const((CONV_WIDTH, GROUP_WIDTH)), const((1, GROUP_WIDTH)), const((1, GROUP_WIDTH)),
            const((1, GROUP_WIDTH)),
        ],
        out_specs=pl.BlockSpec((1, TQ, GROUP_WIDTH), lambda i, j: (i, j, 0)),
        out_shape=jax.ShapeDtypeStruct((bsz, lp, GROUP_WIDTH), BF16),
        scratch_shapes=[pltpu.VMEM((CONV_HIST + TQ, GROUP_WIDTH), F32)],
        compiler_params=_params("parallel", "arbitrary"),
        name="conv_module",
    )(ud, ud, conv_w, vec(conv_b), vec(gn_g), vec(gn_b))


def _router(logit_t, rb_ref):
    s = [_sigmoid(logit_t[e:e + 1, :]) for e in range(N_EXPERTS)]
    bz = [s[e] + rb_ref[e:e + 1, 0:1] for e in range(N_EXPERTS)]
    per = N_EXPERTS // N_GROUPS
    gsum = []
    for gi in range(N_GROUPS):
        a, b, c, d = bz[per * gi:per * gi + per]
        hi1, lo1, hi2, lo2 = jnp.maximum(a, b), jnp.minimum(a, b), jnp.maximum(c, d), jnp.minimum(c, d)
        gsum.append(jnp.maximum(hi1, hi2) + jnp.maximum(jnp.minimum(hi1, hi2), jnp.maximum(lo1, lo2)))
    best = jnp.zeros_like(gsum[0], dtype=jnp.int32)
    bval = gsum[0]
    for gi in range(1, N_GROUPS):
        take = gsum[gi] > bval
        best = jnp.where(take, gi, best)
        bval = jnp.where(take, gsum[gi], bval)

    def pick(vals, i):
        out = vals[i]
        for gi in range(1, N_GROUPS):
            out = jnp.where(best == gi, vals[per * gi + i], out)
        return out

    bv = [pick(bz, i) for i in range(per)]
    sv = [pick(s, i) for i in range(per)]
    i1 = jnp.zeros_like(best)
    v1 = bv[0]
    for i in range(1, per):
        take = bv[i] > v1
        i1 = jnp.where(take, i, i1)
        v1 = jnp.where(take, bv[i], v1)
    i2 = jnp.full_like(best, -1)
    v2 = jnp.full_like(v1, -jnp.inf)
    for i in range(per):
        take = (i1 != i) & (bv[i] > v2)
        i2 = jnp.where(take, i, i2)
        v2 = jnp.where(take, bv[i], v2)
    s1 = sv[0]
    s2 = sv[0]
    for i in range(1, per):
        s1 = jnp.where(i1 == i, sv[i], s1)
        s2 = jnp.where(i2 == i, sv[i], s2)
    tot = s1 + s2
    g1, g2 = s1 / tot, s2 / tot
    rows = []
    for e in range(N_EXPERTS):
        gi, i = divmod(e, per)
        hit = jnp.where(best == gi, jnp.where(i1 == i, g1, jnp.where(i2 == i, g2, 0.0)), 0.0)
        rows.append(hit)
    return jnp.concatenate(rows, axis=0)


def _outproj_kernel(ya_ref, yb_ref, yc_ref, yd_ref, w_ref, h_ref, g_ref, b_ref, rwt_ref, rb_ref, o_ref, comb_ref):
    gw = GROUP_WIDTH
    mix = (_dot(ya_ref[0], w_ref[0:gw, :]) + _dot(yb_ref[0], w_ref[gw:2 * gw, :])
           + _dot(yc_ref[0], w_ref[2 * gw:3 * gw, :]) + _dot(yd_ref[0], w_ref[3 * gw:4 * gw, :]))
    h1 = _ln(ALPHA * h_ref[...] + mix, g_ref[...], b_ref[...])
    o_ref[...] = h1
    wh, wl = _split2(rwt_ref[...])
    hh, hl = _split2(h1)
    logit_t = _dot_nt(wh, hh) + _dot_nt(wh, hl) + _dot_nt(wl, hh)
    comb_ref[...] = _router(logit_t, rb_ref)


def _outproj(ya, yb, yc, yd, w_bf, h, g, b, rwt, rb_b):
    bsz, lp, _ = ya.shape
    d = h.shape[1]
    tm = TQ
    nt = lp // tm
    spec = lambda w: pl.BlockSpec((1, tm, w), lambda i, j: (i, j, 0))
    flat = lambda w: pl.BlockSpec((tm, w), lambda i, j: (i * nt + j, 0))
    const = lambda shape: pl.BlockSpec(shape, lambda i, j: (0, 0))
    return pl.pallas_call(
        _outproj_kernel,
        grid=(bsz, nt),
        in_specs=[spec(GROUP_WIDTH)] * 4 + [const((d, d)), flat(d), const((1, d)), const((1, d)),
                                            const((N_EXPERTS, d)), const((N_EXPERTS, BLK))],
        out_specs=[flat(d), pl.BlockSpec((N_EXPERTS, tm), lambda i, j: (0, i * nt + j))],
        out_shape=[jax.ShapeDtypeStruct((bsz * lp, d), F32), jax.ShapeDtypeStruct((N_EXPERTS, bsz * lp), F32)],
        compiler_params=_params("parallel", "parallel"),
        name="out_proj_ln_router",
    )(ya, yb, yc, yd, w_bf, h, g.reshape(1, d), b.reshape(1, d), rwt, rb_b)


def _route_meta(comb, tm, lp, nreal):
    t = comb.shape[1]
    per = N_EXPERTS // N_GROUPS
    cg = comb.reshape(N_GROUPS, per, t)
    grp = jnp.argmax(jnp.sum(cg, 1), axis=0).astype(jnp.int32)
    comb4 = jnp.take_along_axis(cg, grp[None, None, :], axis=0)[0].T
    grp = jnp.where(jnp.arange(t) % lp < nreal, grp, N_GROUPS)
    order = jnp.argsort(grp, stable=True).astype(jnp.int32)
    counts = jnp.sum((grp[:, None] == jnp.arange(N_GROUPS)[None, :]).astype(jnp.int32), axis=0)
    start = jnp.cumsum(counts) - counts
    pcounts = (counts + tm - 1) // tm * tm
    pend = jnp.cumsum(pcounts)
    pstart = pend - pcounts
    nt = pl.cdiv(t // lp * nreal, tm) + N_GROUPS
    s = jnp.arange(nt * tm, dtype=jnp.int32)
    gs = jnp.minimum(jnp.searchsorted(pend, s, side="right"), N_GROUPS - 1).astype(jnp.int32)
    r = s - pstart[gs]
    valid = r < counts[gs]
    tok = order[jnp.clip(start[gs] + r, 0, t - 1)]
    src = jnp.where(valid, tok, -1)
    c4 = jnp.where(valid[:, None], comb4[tok], 0.0)
    return src, gs[::tm], c4, nt


def _moe_kernel(src_ref, tg_ref, h_hbm, c_ref, w1_ref, w3_ref, w2_ref, g_ref, b_ref, o_hbm, xbuf, obuf, gsem, ssem, *,
                tm, nt, bsz, lp, nreal):
    i = pl.program_id(0)
    slot = lax.rem(i, 2)
    npad = lp - nreal

    def pad_row(q):
        return (q // npad) * lp + nreal + q % npad

    def gather(tile, sl):
        base = tile * tm
        for r in range(tm):
            tok = jnp.maximum(src_ref[base + r], 0)
            pltpu.make_async_copy(h_hbm.at[pl.ds(tok, 1)], xbuf.at[sl, pl.ds(r, 1)], gsem.at[sl]).start()

    def scatter(sl):
        base = i * tm
        for r in range(tm):
            s = src_ref[base + r]
            dst = jnp.where(s >= 0, s, pad_row(sl * tm + r))
            pltpu.make_async_copy(obuf.at[sl, pl.ds(r, 1)], o_hbm.at[pl.ds(dst, 1)], ssem.at[sl]).start()

    def wait_gather(sl):
        pltpu.make_async_copy(h_hbm.at[pl.ds(0, tm)], xbuf.at[sl], gsem.at[sl]).wait()

    def wait_scatter(sl):
        pltpu.make_async_copy(obuf.at[sl], o_hbm.at[pl.ds(0, tm)], ssem.at[sl]).wait()

    @pl.when(i == 0)
    def _():
        gather(0, 0)
        obuf[0, 0:npad, :] = jnp.zeros((npad, obuf.shape[2]), F32)
        for bi in range(bsz):
            fill = pltpu.make_async_copy(obuf.at[0, pl.ds(0, npad)], o_hbm.at[pl.ds(bi * lp + nreal, npad)],
                                         ssem.at[0])
            fill.start()
            fill.wait()

    for sl in range(2):
        @pl.when((i + 1 < nt) & (slot == 1 - sl))
        def _(sl=sl):
            gather(i + 1, sl)

    wait_gather(slot)

    @pl.when(i >= 2)
    def _():
        wait_scatter(slot)

    x = xbuf[slot]
    xb = x.astype(BF16)
    c = c_ref[...]
    y = jnp.zeros(x.shape, F32)
    for e in range(N_EXPERTS // N_GROUPS):
        a1 = _dot(xb, w1_ref[0, e])
        act = (a1 * _sigmoid(a1)) * _dot(xb, w3_ref[0, e]) * c[:, e:e + 1]
        y = y + _dot(act.astype(BF16), w2_ref[0, e])
    obuf[slot] = _ln(ALPHA * x + y, g_ref[...], b_ref[...])

    for sl in range(2):
        @pl.when(slot == sl)
        def _(sl=sl):
            scatter(sl)

    @pl.when(i == nt - 1)
    def _():
        wait_scatter(slot)
        if nt >= 2:
            wait_scatter(1 - slot)


def _moe(h1, comb, w1g, w3g, w2g, g, b, bsz, lp, nreal):
    t, d = h1.shape
    tm = MOE_TM
    per = N_EXPERTS // N_GROUPS
    assert bsz * (lp - nreal) >= 2 * tm and lp - nreal <= tm
    src, tile_grp, c4, nt = _route_meta(comb, tm, lp, nreal)
    wspec = lambda a, b_: pl.BlockSpec((1, per, a, b_), lambda i, src, tg: (tg[i], 0, 0, 0))
    grid_spec = pltpu.PrefetchScalarGridSpec(
        num_scalar_prefetch=2,
        grid=(nt,),
        in_specs=[
            pl.BlockSpec(memory_space=pl.ANY),
            pl.BlockSpec((tm, per), lambda i, src, tg: (i, 0)),
            wspec(d, EXPERT_FF), wspec(d, EXPERT_FF), wspec(EXPERT_FF, d),
            pl.BlockSpec((1, d), lambda i, src, tg: (0, 0)),
            pl.BlockSpec((1, d), lambda i, src, tg: (0, 0)),
        ],
        out_specs=pl.BlockSpec(memory_space=pl.ANY),
        scratch_shapes=[
            pltpu.VMEM((2, tm, d), F32), pltpu.VMEM((2, tm, d), F32),
            pltpu.SemaphoreType.DMA((2,)), pltpu.SemaphoreType.DMA((2,)),
        ],
    )
    return pl.pallas_call(
        functools.partial(_moe_kernel, tm=tm, nt=nt, bsz=bsz, lp=lp, nreal=nreal),
        grid_spec=grid_spec,
        out_shape=jax.ShapeDtypeStruct((t, d), F32),
        compiler_params=_params("arbitrary"),
        name="moe_ln",
    )(src, tile_grp, h1, c4, w1g, w3g, w2g, g.reshape(1, d), b.reshape(1, d))


def _group_weights(w):
    return w.astype(BF16).reshape((N_GROUPS, N_EXPERTS // N_GROUPS) + w.shape[1:])


def kernel(x, meta, ln0_g, ln0_b, w_in, swa_sinks, rel_bias, diff_lq1, diff_lk1, diff_lq2, diff_lk2, diff_subln_g,
           rwkv_mu, rwkv_w0, rwkv_w2, rwkv_a0, rwkv_a2, rwkv_g2, rwkv_kk, rwkv_ka, rwkv_rk, rwkv_lnx_g,
           rwkv_lnx_b, conv_w, conv_b, conv_gn_g, conv_gn_b, w_out, ln1_g, ln1_b, router_w, router_b, exp_w1,
           exp_w3, exp_w2, ln2_g, ln2_b):
    bsz, seq, d = x.shape
    assert d == D_MODEL and seq % BLK == 0
    lp = pl.cdiv(seq + N_META, TQ) * TQ

    r = jnp.arange(BLK)
    bkt_band = _t5_bucket(r[:, None] - jnp.arange(2 * BLK)[None, :] + BLK)
    qpos3 = jnp.arange(3 * BLK).reshape(3, BLK)
    bkt_meta = _t5_bucket(qpos3[:, :, None] - jnp.arange(N_META)[None, None, :])
    dq0 = r[None, :] - r[:, None]
    bkt_d = jnp.stack([_t5_bucket(dq0), _t5_bucket(dq0 + BLK)])
    rel_a, rel_b = rel_bias[:, :N_HEADS], rel_bias[:, N_HEADS:]
    rwt = router_w.T
    rb_b = jnp.broadcast_to(router_b.astype(F32)[:, None], (N_EXPERTS, BLK))
    vcols = slice(SWA_IN + 512, SWA_IN + 768)

    h = _prep(x, meta, ln0_g, ln0_b, lp).reshape(bsz * lp, d)
    for l in range(DEPTH):
        w_l = w_in[l]
        ua, qd, kd, vdt, uc, ud = _inproj(h, w_l.astype(BF16), w_l[:, vcols].T.astype(BF16), bsz, lp)
        ya = _swa(ua, rel_a, swa_sinks[l], bkt_band, bkt_meta)
        lam_init = 0.8 - 0.6 * math.exp(-0.3 * l)
        lam = (jnp.exp(jnp.sum(diff_lq1[l] * diff_lk1[l])) - jnp.exp(jnp.sum(diff_lq2[l] * diff_lk2[l])) + lam_init)
        lam2 = jnp.stack([lam, jnp.asarray(1.0 - lam_init, F32)]).astype(F32)
        yb = _diff(qd, kd, vdt, lam2, rel_b, bkt_d, diff_subln_g[l].reshape(HEAD_DIM, 1))
        yc = _rwkv(uc, rwkv_mu[l], rwkv_w0[l], rwkv_w2[l], rwkv_a0[l], rwkv_a2[l], rwkv_g2[l], rwkv_kk[l],
                   rwkv_ka[l], rwkv_rk[l], rwkv_lnx_g[l], rwkv_lnx_b[l])
        yd = _conv(ud, conv_w[l], conv_b[l], conv_gn_g[l], conv_gn_b[l])
        h1, comb = _outproj(ya, yb, yc, yd, w_out[l].astype(BF16), h, ln1_g[l], ln1_b[l], rwt, rb_b)
        h = _moe(h1, comb, _group_weights(exp_w1[l]), _group_weights(exp_w3[l]), _group_weights(exp_w2[l]),
                 ln2_g[l], ln2_b[l], bsz, lp, N_META + seq)
    return h.reshape(bsz, lp, d)[:, N_META:N_META + seq]
```

```python
import functools
import math

import jax
import jax.numpy as jnp
import numpy as np
from jax import lax
from jax.experimental import pallas as pl
from jax.experimental.pallas import tpu as pltpu

F32 = jnp.float32
BF16 = jnp.bfloat16

D_MODEL = 1024
N_META = 16
BLK = 128
TQ = 3 * BLK
HEAD_DIM = 64
GROUP_WIDTH = 256
N_HEADS = 4
SWA_KV_HEADS = 2
DIFF_QK = 32
DECAY_LORA = 64
AAA_LORA = 64
GATE_LORA = 128
CONV_WIDTH = 31
CONV_HIST = 32
REL_BUCKETS = 32
REL_MAX_DIST = 128
N_EXPERTS = 16
N_GROUPS = 4
EXPERT_FF = 512
DEPTH = 2
ALPHA = (2 * DEPTH) ** 0.25
SWA_IN = 512
DIFF_IN = 768
RWKV_IN = 1024
CONV_IN = 512
IN_WIDTH = SWA_IN + DIFF_IN + RWKV_IN + CONV_IN
NEG = -1e30
LOG2E = math.log2(math.e)
CHUNK = 64
RSTEP = TQ
MOE_TM = 256
MOE_PAIRS = 6
VMEM_LIMIT = 56 * 1024 * 1024


def _dot(a, b, prec=None):
    return jnp.dot(a, b, preferred_element_type=F32, precision=prec)


def _dot_nt(a, b, prec=None):
    return lax.dot_general(a, b, (((1,), (1,)), ((), ())), preferred_element_type=F32, precision=prec)


def _dot_tn(a, b, prec=None):
    return lax.dot_general(a, b, (((0,), (0,)), ((), ())), preferred_element_type=F32, precision=prec)


def _split2(x):
    hi = x.astype(BF16)
    return hi, (x - hi.astype(F32)).astype(BF16)


def _dot_exact_rhs(x, m_bf):
    hi, lo = _split2(x)
    return _dot(hi, m_bf) + _dot(lo, m_bf)


def _dot_exact_lhs(m_bf, x):
    hi, lo = _split2(x)
    return _dot(m_bf, hi) + _dot(m_bf, lo)


def _dot3(a, b):
    ah, al = _split2(a)
    bh, bl = _split2(b)
    return _dot(ah, bh) + _dot(ah, bl) + _dot(al, bh)


def _ln(x, g, b, eps=1e-5):
    mu = jnp.mean(x, -1, keepdims=True)
    xc = x - mu
    var = jnp.mean(xc * xc, -1, keepdims=True)
    return xc * lax.rsqrt(var + eps) * g + b


def _sigmoid(x):
    return 1.0 / (1.0 + jnp.exp(-x))


def _pick_tile(n, candidates):
    for c in candidates:
        if n % c == 0:
            return c
    raise ValueError(f"no tile in {candidates} divides {n}")


def _params(*sem):
    return pltpu.CompilerParams(dimension_semantics=sem, vmem_limit_bytes=VMEM_LIMIT)


def _t5_bucket(dist):
    n = jnp.maximum(dist, 0)
    max_exact = REL_BUCKETS // 2
    log_ratio = jnp.log(jnp.maximum(n, 1).astype(F32) / max_exact) / math.log(REL_MAX_DIST / max_exact)
    large = jnp.minimum(max_exact + (log_ratio * (REL_BUCKETS - max_exact)).astype(jnp.int32), REL_BUCKETS - 1)
    return jnp.where(n < max_exact, n, large)


def _gather_bias(bkt, rel_ref, h, offset=0.0):
    acc = jnp.zeros(bkt.shape, F32)
    for b in range(REL_BUCKETS):
        acc = jnp.where(bkt == b, rel_ref[b, h] - offset, acc)
    return acc


def _prep_kernel(xa_ref, xb_ref, meta_ref, g_ref, b_ref, o_ref, *, nbx):
    j = pl.program_id(1)
    head = jnp.where(j == 0, meta_ref[...], xa_ref[0])
    body = _ln(xb_ref[0, 0:BLK - N_META, :], g_ref[...], b_ref[...])
    o_ref[0, 0:N_META, :] = jnp.where(j <= nbx, _ln(head, g_ref[...], b_ref[...]), 0.0)
    o_ref[0, N_META:BLK, :] = jnp.where(j < nbx, body, 0.0)


def _prep(x, meta, g, b, lp):
    bsz, seq, d = x.shape
    nbx = seq // BLK
    per = BLK // N_META
    return pl.pallas_call(
        functools.partial(_prep_kernel, nbx=nbx),
        grid=(bsz, lp // BLK),
        in_specs=[
            pl.BlockSpec((1, N_META, d), lambda i, j: (i, jnp.clip(per * j - 1, 0, per * nbx - 1), 0)),
            pl.BlockSpec((1, BLK, d), lambda i, j: (i, jnp.minimum(j, nbx - 1), 0)),
            pl.BlockSpec((N_META, d), lambda i, j: (0, 0)),
            pl.BlockSpec((1, d), lambda i, j: (0, 0)),
            pl.BlockSpec((1, d), lambda i, j: (0, 0)),
        ],
        out_specs=pl.BlockSpec((1, BLK, d), lambda i, j: (i, j, 0)),
        out_shape=jax.ShapeDtypeStruct((bsz, lp, d), F32),
        compiler_params=_params("parallel", "parallel"),
        name="prep_ln",
    )(x, x, meta, g.reshape(1, d), b.reshape(1, d))


def _inproj_kernel(h_ref, w_ref, wvt_ref, ua_ref, qd_ref, kd_ref, vdt_ref, uc_ref, ud_ref):
    hb = h_ref[...].astype(BF16)

    def proj(c0, c1):
        return _dot(hb, w_ref[:, c0:c1])

    ua_ref[0, :, 0:256] = (proj(0, 256) * (HEAD_DIM ** -0.5)).astype(BF16)
    ua_ref[0, :, 256:512] = proj(256, 512).astype(BF16)
    qd_ref[0] = (proj(512, 768) * (DIFF_QK ** -0.5 * LOG2E)).astype(BF16)
    kd_ref[0] = proj(768, 1024).astype(BF16)
    vdt_ref[0, 0] = _dot_nt(wvt_ref[...], hb).astype(BF16)
    uc_ref[0] = proj(1280, 2304)
    ud_ref[0] = proj(2304, 2816)


def _inproj(h, w_bf, wvt_bf, bsz, lp):
    d = h.shape[1]
    tm = TQ
    nt = lp // tm
    row = lambda w, dt: jax.ShapeDtypeStruct((bsz, lp, w), dt)
    spec = lambda w: pl.BlockSpec((1, tm, w), lambda i, j: (i, j, 0))
    return pl.pallas_call(
        _inproj_kernel,
        grid=(bsz, nt),
        in_specs=[
            pl.BlockSpec((tm, d), lambda i, j: (i * nt + j, 0)),
            pl.BlockSpec((d, IN_WIDTH), lambda i, j: (0, 0)),
            pl.BlockSpec((GROUP_WIDTH, d), lambda i, j: (0, 0)),
        ],
        out_specs=[
            spec(512), spec(256), spec(256),
            pl.BlockSpec((1, 1, GROUP_WIDTH, tm), lambda i, j: (i, j, 0, 0)),
            spec(1024), spec(512),
        ],
        out_shape=[
            row(512, BF16), row(256, BF16), row(256, BF16),
            jax.ShapeDtypeStruct((bsz, nt, GROUP_WIDTH, tm), BF16),
            row(1024, F32), row(512, F32),
        ],
        compiler_params=_params("parallel", "parallel"),
        name="in_proj",
    )(h, w_bf, wvt_bf)


def _swa_kernel(rel_ref, sink_ref, bband_ref, bmeta_ref, cur_ref, prev_ref, meta_ref, o_ref, band_sc, metab_sc):
    j = pl.program_id(1)

    @pl.when(j == 0)
    def _():
        for h in range(N_HEADS):
            band_sc[h] = _gather_bias(bband_ref[...], rel_ref, h)

        for n in range(3):
            for h in range(N_HEADS):
                metab_sc[n, h] = _gather_bias(bmeta_ref[n], rel_ref, h)

    nsub = TQ // BLK
    r = lax.broadcasted_iota(jnp.int32, (BLK, 1), 0)
    dq = r - lax.broadcasted_iota(jnp.int32, (1, 2 * BLK), 1) + BLK
    in_window = (dq >= 0) & (dq < BLK)
    meta = meta_ref[0, 0:N_META, :]
    for u in range(nsub):
        cur = cur_ref[0, u * BLK:(u + 1) * BLK, :]
        prev = prev_ref[0] if u == 0 else cur_ref[0, (u - 1) * BLK:u * BLK, :]
        qpos = (j * nsub + u) * BLK + r
        ok_meta = qpos >= lax.broadcasted_iota(jnp.int32, (1, N_META), 1)
        ok_band = in_window & (qpos - dq >= N_META)
        mtab = jnp.where(j == 0, u, 2)
        outs = []
        for h in range(N_HEADS):
            g = h // (N_HEADS // SWA_KV_HEADS)
            kc, vc = 256 + HEAD_DIM * g, 384 + HEAD_DIM * g
            q = cur[:, HEAD_DIM * h:HEAD_DIM * (h + 1)]
            kcat = jnp.concatenate([prev[:, kc:kc + HEAD_DIM], cur[:, kc:kc + HEAD_DIM]], axis=0)
            vcat = jnp.concatenate([prev[:, vc:vc + HEAD_DIM], cur[:, vc:vc + HEAD_DIM]], axis=0)
            s_m = jnp.where(ok_meta, _dot_nt(q, meta[:, kc:kc + HEAD_DIM]) + metab_sc[mtab, h], NEG)
            s_b = jnp.where(ok_band, _dot_nt(q, kcat) + band_sc[h], NEG)
            sink = sink_ref[h]
            m = jnp.maximum(jnp.maximum(jnp.max(s_m, -1, keepdims=True), jnp.max(s_b, -1, keepdims=True)), sink)
            p_m, p_b = jnp.exp(s_m - m), jnp.exp(s_b - m)
            den = jnp.sum(p_m, -1, keepdims=True) + jnp.sum(p_b, -1, keepdims=True) + jnp.exp(sink - m)
            o = _dot(p_m.astype(BF16), meta[:, vc:vc + HEAD_DIM]) + _dot(p_b.astype(BF16), vcat)
            outs.append(o / den)
        o_ref[0, u * BLK:(u + 1) * BLK, :] = jnp.concatenate(outs, axis=1).astype(BF16)


def _swa(ua, rel_a, sinks, bkt_band, bkt_meta):
    bsz, lp, _ = ua.shape
    nsub = TQ // BLK
    smem = pl.BlockSpec(memory_space=pltpu.SMEM)
    return pl.pallas_call(
        _swa_kernel,
        grid=(bsz, lp // TQ),
        in_specs=[
            smem, smem,
            pl.BlockSpec((BLK, 2 * BLK), lambda i, j: (0, 0)),
            pl.BlockSpec((3, BLK, N_META), lambda i, j: (0, 0, 0)),
            pl.BlockSpec((1, TQ, SWA_IN), lambda i, j: (i, j, 0)),
            pl.BlockSpec((1, BLK, SWA_IN), lambda i, j: (i, jnp.maximum(j * nsub - 1, 0), 0)),
            pl.BlockSpec((1, BLK, SWA_IN), lambda i, j: (i, 0, 0)),
        ],
        out_specs=pl.BlockSpec((1, TQ, GROUP_WIDTH), lambda i, j: (i, j, 0)),
        out_shape=jax.ShapeDtypeStruct((bsz, lp, GROUP_WIDTH), BF16),
        scratch_shapes=[pltpu.VMEM((N_HEADS, BLK, 2 * BLK), F32), pltpu.VMEM((3, N_HEADS, BLK, N_META), F32)],
        compiler_params=_params("arbitrary", "arbitrary"),
        name="swa_attn",
    )(rel_a, sinks, bkt_band, bkt_meta, ua, ua, ua)


def _diff_kernel(lam_ref, rel_ref, bkt_ref, g_ref, q_ref, k_ref, vt_ref, o_ref, wt_sc, b1_sc, addm_sc, m_sc, acc_sc):
    i = pl.program_id(1)
    nsub = TQ // BLK

    @pl.when(i == 0)
    def _():
        kr = lax.broadcasted_iota(jnp.int32, (BLK, 1), 0)
        qc = lax.broadcasted_iota(jnp.int32, (1, BLK), 1)
        for h in range(N_HEADS):
            far = rel_ref[REL_BUCKETS - 1, h]
            b0 = jnp.where(qc >= kr, _gather_bias(bkt_ref[0], rel_ref, h, far) * LOG2E, NEG)
            b1 = _gather_bias(bkt_ref[1], rel_ref, h, far) * LOG2E
            b1_sc[h] = b1
            blocks = {0: b0, 1: b1, 2: jnp.zeros((BLK, BLK), F32)}
            masked = jnp.full((BLK, BLK), NEG, F32)
            for u in range(nsub):
                row = [blocks[w - u] if w >= u else masked for w in range(nsub)]
                addm_sc[h, u * BLK:(u + 1) * BLK, :] = jnp.concatenate(row + row, axis=1)

    q = q_ref[0]
    lane_grp = lax.broadcasted_iota(jnp.int32, (1, GROUP_WIDTH), 1) // DIFF_QK
    zero = jnp.zeros_like(q)
    for h in range(N_HEADS):
        wt_sc[h, 0:TQ, :] = jnp.where(lane_grp == 2 * h, q, zero)
        wt_sc[h, TQ:2 * TQ, :] = jnp.where(lane_grp == 2 * h + 1, q, zero)
    m_sc[...] = jnp.full(m_sc.shape, NEG, F32)
    acc_sc[...] = jnp.zeros(acc_sc.shape, F32)

    ones_rows = jnp.ones((2 * 8, TQ), BF16)

    def scores(t, h):
        return _dot_nt(k_ref[0, pl.ds(pl.multiple_of(t * TQ, TQ), TQ), :], wt_sc[h])

    def consume(t, h, s, kind):
        if kind == "diag":
            s = s + addm_sc[h]
        elif kind == "near":
            top, bot = s[0:TQ - BLK], s[TQ - BLK:TQ]
            b1 = b1_sc[h]
            bot = jnp.concatenate([bot[:, 0:BLK] + b1, bot[:, BLK:TQ], bot[:, TQ:TQ + BLK] + b1,
                                   bot[:, TQ + BLK:2 * TQ]], axis=1)
            s = jnp.concatenate([top, bot], axis=0)
        m_old = m_sc[h]
        cm = jnp.maximum(jnp.maximum(s[0:BLK], s[BLK:2 * BLK]), s[2 * BLK:3 * BLK])
        n = BLK
        while n > 8:
            n //= 2
            cm = jnp.maximum(cm[0:n], cm[n:2 * n])
        m_new = jnp.maximum(m_old, jnp.max(cm, 0, keepdims=True))
        a = jnp.exp2(m_old - m_new)
        p = jnp.exp2(s - m_new).astype(BF16)
        vth = vt_ref[0, t, HEAD_DIM * h:HEAD_DIM * (h + 1), :]
        pv = _dot(jnp.concatenate([vth, ones_rows], axis=0), p)
        acc_sc[h] = a * acc_sc[h] + pv
        m_sc[h] = m_new

    def tile(t, kind):
        for h in range(N_HEADS):
            consume(t, h, scores(t, h), kind)

    def far(t, c):
        tile(t, None)
        return c

    lax.fori_loop(0, i - 1, far, 0)

    @pl.when(i >= 1)
    def _():
        tile(i - 1, "near")

    tile(i, "diag")

    lam = lam_ref[0]
    post = lam_ref[1]
    outs = []
    for h in range(N_HEADS):
        acc = acc_sc[h]
        o = acc[0:HEAD_DIM] / acc[HEAD_DIM:HEAD_DIM + 1]
        o = o[:, 0:TQ] - lam * o[:, TQ:2 * TQ]
        ms = jnp.mean(o * o, 0, keepdims=True)
        outs.append(o * lax.rsqrt(ms + 1e-5) * g_ref[...] * post)
    ot = jnp.concatenate(outs, axis=0).astype(BF16)
    eye = (lax.broadcasted_iota(jnp.int32, (TQ, TQ), 0) == lax.broadcasted_iota(jnp.int32, (TQ, TQ), 1))
    o_ref[0] = _dot_nt(eye.astype(BF16), ot).astype(BF16)


def _diff(qd, kd, vdt, lam2, rel_b, bkt_d, g_b):
    bsz, lp, _ = qd.shape
    nq = lp // TQ
    smem = pl.BlockSpec(memory_space=pltpu.SMEM)
    return pl.pallas_call(
        _diff_kernel,
        grid=(bsz, nq),
        in_specs=[
            smem, smem,
            pl.BlockSpec((2, BLK, BLK), lambda b, i: (0, 0, 0)),
            pl.BlockSpec((HEAD_DIM, 1), lambda b, i: (0, 0)),
            pl.BlockSpec((1, TQ, GROUP_WIDTH), lambda b, i: (b, i, 0)),
            pl.BlockSpec((1, lp, GROUP_WIDTH), lambda b, i: (b, 0, 0)),
            pl.BlockSpec((1, nq, GROUP_WIDTH, TQ), lambda b, i: (b, 0, 0, 0)),
        ],
        out_specs=pl.BlockSpec((1, TQ, GROUP_WIDTH), lambda b, i: (b, i, 0)),
        out_shape=jax.ShapeDtypeStruct((bsz, lp, GROUP_WIDTH), BF16),
        scratch_shapes=[
            pltpu.VMEM((N_HEADS, 2 * TQ, GROUP_WIDTH), BF16),
            pltpu.VMEM((N_HEADS, BLK, BLK), F32),
            pltpu.VMEM((N_HEADS, TQ, 2 * TQ), F32),
            pltpu.VMEM((N_HEADS, 1, 2 * TQ), F32),
            pltpu.VMEM((N_HEADS, HEAD_DIM + 2 * 8, 2 * TQ), F32),
        ],
        compiler_params=_params("arbitrary", "arbitrary"),
        name="diff_attn",
    )(lam2, rel_b, bkt_d, g_b, qd, kd, vdt)


def _rwkv_kernel(x_ref, xp_ref, mu_ref, w0_ref, w2_ref, a0_ref, a2_ref, g2_ref, kk_ref, ka_ref, rk_ref,
                 lg_ref, lb_ref, o_ref, zt_sc):
    c = pl.program_id(1)
    n = CHUNK
    ns = RSTEP // CHUNK
    gw = GROUP_WIDTH

    @pl.when(c == 0)
    def _():
        zt_sc[...] = jnp.zeros(zt_sc.shape, F32)

    x = x_ref[0]
    row = lax.broadcasted_iota(jnp.int32, (RSTEP, 1), 0)
    last_prev = jnp.where(c == 0, 0.0, xp_ref[0, 7:8, :])
    xs = jnp.where(row == 0, last_prev, pltpu.roll(x, 1, 0))
    xm = x + mu_ref[...] * (xs - x)
    r, k, v = xm[:, 0:gw], xm[:, gw:2 * gw], xm[:, 2 * gw:3 * gw]
    xw = xm[:, 768:768 + DECAY_LORA]
    xa = xm[:, 832:832 + AAA_LORA]
    xg = xm[:, 896:896 + GATE_LORA]
    logw = -math.exp(-0.5) * _sigmoid(w0_ref[...] + _dot3(jnp.tanh(xw), w2_ref[...]))
    a = _sigmoid(a0_ref[...] + _dot3(xa, a2_ref[...]))
    g = _dot3(_sigmoid(xg), g2_ref[...])

    lane_head = lax.broadcasted_iota(jnp.int32, (1, gw), 1) // HEAD_DIM
    sub_head = lax.broadcasted_iota(jnp.int32, (gw, 1), 0) // HEAD_DIM
    head_ones = (sub_head == lane_head).astype(BF16)

    kk = k * kk_ref[...]
    kk = kk / jnp.maximum(jnp.sqrt(_dot_exact_rhs(kk * kk, head_ones)), 1e-12)
    k2 = k * (1.0 + (a - 1.0) * ka_ref[...])
    bvec = kk * a

    ri = lax.broadcasted_iota(jnp.int32, (RSTEP, RSTEP), 0)
    ci = lax.broadcasted_iota(jnp.int32, (RSTEP, RSTEP), 1)
    tri = (((ri // n) == (ci // n)) & (ri >= ci)).astype(BF16)
    cum = _dot_exact_lhs(tri, logw)
    tot = jnp.concatenate([jnp.broadcast_to(cum[(s + 1) * n - 1:(s + 1) * n, :], (n, gw)) for s in range(ns)], axis=0)
    e_neg = jnp.exp(-cum)
    e_last = jnp.exp(tot - cum)
    a_t = (-kk * jnp.exp(cum - logw)).astype(BF16)
    r_t = (r * jnp.exp(cum)).astype(BF16)
    b_t, k_t = (bvec * e_neg).astype(BF16), (k2 * e_neg).astype(BF16)
    b_h, k_h = (bvec * e_last).astype(BF16), (k2 * e_last).astype(BF16)
    g_c = jnp.exp(tot)
    vb = v.astype(BF16)

    def stack(t, s):
        ts = t[s * n:(s + 1) * n]
        return jnp.concatenate([jnp.where(lane_head == h, ts, jnp.zeros_like(ts)) for h in range(N_HEADS)], axis=0)

    rr = lax.broadcasted_iota(jnp.int32, (gw, gw), 0)
    cc = lax.broadcasted_iota(jnp.int32, (gw, gw), 1)
    parts = []
    for s in range(ns):
        ast, rst, bst, kst = stack(a_t, s), stack(r_t, s), stack(b_t, s), stack(k_t, s)
        gmat = _dot_nt(jnp.concatenate([ast, rst], axis=0), jnp.concatenate([bst, kst], axis=0))
        low = jnp.where(rr > cc, gmat[0:gw, 0:gw], 0.0)
        aak = jnp.where(rr > cc, gmat[0:gw, gw:2 * gw], 0.0).astype(BF16)
        arb = jnp.where(rr >= cc, gmat[gw:2 * gw, 0:gw], 0.0).astype(BF16)
        ark = jnp.where(rr >= cc, gmat[gw:2 * gw, gw:2 * gw], 0.0).astype(BF16)
        pw = low
        tinv = jnp.where(rr == cc, 1.0, low)
        for _ in range(5):
            pwb = pw.astype(BF16)
            pw = _dot(pwb, pwb)
            tinv = tinv + _dot(tinv.astype(BF16), pw.astype(BF16))
        parts.append((ast, rst, aak, arb, ark, tinv.astype(BF16), stack(b_h, s), stack(k_h, s), stack(vb, s)))

    zt = zt_sc[...]
    ys = []
    for s in range(ns):
        ast, rst, aak, arb, ark, tinv, bhs, khs, vst = parts[s]
        ztb = zt.astype(BF16)
        u = _dot(tinv, (_dot_nt(ast, ztb) + _dot(aak, vst)).astype(BF16)).astype(BF16)
        ybd = _dot_nt(rst, ztb) + _dot(arb, u) + _dot(ark, vst)
        zt = zt * g_c[s * n:s * n + 1, :] + _dot_tn(u, bhs) + _dot_tn(vst, khs)
        ys.append(ybd[0:n] + ybd[n:2 * n] + ybd[2 * n:3 * n] + ybd[3 * n:4 * n])
    zt_sc[...] = zt
    y = jnp.concatenate(ys, axis=0)

    mean = _dot_exact_rhs(y, head_ones) * (1.0 / HEAD_DIM)
    yc = y - mean
    var = _dot_exact_rhs(yc * yc, head_ones) * (1.0 / HEAD_DIM)
    yn = yc * lax.rsqrt(var + 64e-5) * lg_ref[...] + lb_ref[...]
    bonus = _dot_exact_rhs(r * k2 * rk_ref[...], head_ones) * v
    o_ref[0] = ((yn + bonus) * g).astype(BF16)


def _rwkv(uc, mu, w0, w2, a0, a2, g2, k_k, k_a, r_k, lnx_g, lnx_b):
    bsz, lp, w = uc.shape
    nc = lp // RSTEP
    vec = lambda t: t.reshape(1, -1)
    full = lambda t: pl.BlockSpec(t.shape, lambda b, c: (0,) * t.ndim)
    args = [vec(mu), vec(w0), w2, vec(a0), a2, g2, vec(k_k), vec(k_a), vec(r_k), vec(lnx_g), vec(lnx_b)]
    return pl.pallas_call(
        _rwkv_kernel,
        grid=(bsz, nc),
        in_specs=[
            pl.BlockSpec((1, RSTEP, w), lambda b, c: (b, c, 0)),
            pl.BlockSpec((1, 8, w), lambda b, c: (b, jnp.maximum(c * (RSTEP // 8) - 1, 0), 0)),
        ] + [full(t) for t in args],
        out_specs=pl.BlockSpec((1, RSTEP, GROUP_WIDTH), lambda b, c: (b, c, 0)),
        out_shape=jax.ShapeDtypeStruct((bsz, lp, GROUP_WIDTH), BF16),
        scratch_shapes=[pltpu.VMEM((GROUP_WIDTH, GROUP_WIDTH), F32)],
        compiler_params=_params("arbitrary", "arbitrary"),
        name="rwkv7",
    )(uc, uc, *args)


def _conv_kernel(cur_ref, prev_ref, w_ref, b_ref, g_ref, gb_ref, o_ref, hcat_sc):
    j = pl.program_id(1)
    ch = GROUP_WIDTH

    def glu(t):
        return t[:, 0:ch] * _sigmoid(t[:, ch:2 * ch])

    hcat_sc[0:CONV_HIST, :] = jnp.where(j == 0, 0.0, glu(prev_ref[0]))
    hcat_sc[CONV_HIST:CONV_HIST + TQ, :] = glu(cur_ref[0])
    lead = CONV_HIST - (CONV_WIDTH - 1)
    acc = jnp.zeros((TQ, ch), F32) + b_ref[...]
    for ph in range(8):
        offs = [o for o in range(lead, CONV_HIST + 1) if o % 8 == ph]
        shifted = hcat_sc[ph:max(offs) + TQ, :]
        for o in offs:
            acc = acc + shifted[o - ph:o - ph + TQ] * w_ref[o - lead:o - lead + 1, :]
    lane_grp = lax.broadcasted_iota(jnp.int32, (1, ch), 1) // HEAD_DIM
    sub_grp = lax.broadcasted_iota(jnp.int32, (ch, 1), 0) // HEAD_DIM
    grp_ones = (sub_grp == lane_grp).astype(BF16)
    mean = _dot_exact_rhs(acc, grp_ones) * (1.0 / HEAD_DIM)
    xc = acc - mean
    var = _dot_exact_rhs(xc * xc, grp_ones) * (1.0 / HEAD_DIM)
    y = xc * lax.rsqrt(var + 1e-5) * g_ref[...] + gb_ref[...]
    o_ref[0] = (y * _sigmoid(y)).astype(BF16)


def _conv(ud, conv_w, conv_b, gn_g, gn_b):
    bsz, lp, w = ud.shape
    nb = lp // TQ
    vec = lambda t: t.reshape(1, -1)
    const = lambda shape: pl.BlockSpec(shape, lambda i, j: (0, 0))
    return pl.pallas_call(
        _conv_kernel,
        grid=(bsz, nb),
        in_specs=[
            pl.BlockSpec((1, TQ, w), lambda i, j: (i, j, 0)),
            pl.BlockSpec((1, CONV_HIST, w), lambda i, j: (i, jnp.maximum(j * (TQ // CONV_HIST) - 1, 0), 0)),
            const((CONV_WIDTH, GROUP_WIDTH)), const((1, GROUP_WIDTH)), const((1, GROUP_WIDTH)),
            const((1, GROUP_WIDTH)),
        ],
        out_specs=pl.BlockSpec((1, TQ, GROUP_WIDTH), lambda i, j: (i, j, 0)),
        out_shape=jax.ShapeDtypeStruct((bsz, lp, GROUP_WIDTH), BF16),
        scratch_shapes=[pltpu.VMEM((CONV_HIST + TQ, GROUP_WIDTH), F32)],
        compiler_params=_params("parallel", "arbitrary"),
        name="conv_module",
    )(ud, ud, conv_w, vec(conv_b), vec(gn_g), vec(gn_b))


def _router(logit_t, rb_ref):
    s = [_sigmoid(logit_t[e:e + 1, :]) for e in range(N_EXPERTS)]
    bz = [s[e] + rb_ref[e:e + 1, 0:1] for e in range(N_EXPERTS)]
    per = N_EXPERTS // N_GROUPS
    gsum = []
    for gi in range(N_GROUPS):
        a, b, c, d = bz[per * gi:per * gi + per]
        hi1, lo1, hi2, lo2 = jnp.maximum(a, b), jnp.minimum(a, b), jnp.maximum(c, d), jnp.minimum(c, d)
        gsum.append(jnp.maximum(hi1, hi2) + jnp.maximum(jnp.minimum(hi1, hi2), jnp.maximum(lo1, lo2)))
    best = jnp.zeros_like(gsum[0], dtype=jnp.int32)
    bval = gsum[0]
    for gi in range(1, N_GROUPS):
        take = gsum[gi] > bval
        best = jnp.where(take, gi, best)
        bval = jnp.where(take, gsum[gi], bval)

    def pick(vals, i):
        out = vals[i]
        for gi in range(1, N_GROUPS):
            out = jnp.where(best == gi, vals[per * gi + i], out)
        return out

    bv = [pick(bz, i) for i in range(per)]
    sv = [pick(s, i) for i in range(per)]
    i1 = jnp.zeros_like(best)
    v1 = bv[0]
    for i in range(1, per):
        take = bv[i] > v1
        i1 = jnp.where(take, i, i1)
        v1 = jnp.where(take, bv[i], v1)
    i2 = jnp.full_like(best, -1)
    v2 = jnp.full_like(v1, -jnp.inf)
    for i in range(per):
        take = (i1 != i) & (bv[i] > v2)
        i2 = jnp.where(take, i, i2)
        v2 = jnp.where(take, bv[i], v2)
    s1 = sv[0]
    s2 = sv[0]
    for i in range(1, per):
        s1 = jnp.where(i1 == i, sv[i], s1)
        s2 = jnp.where(i2 == i, sv[i], s2)
    tot = s1 + s2
    g1, g2 = s1 / tot, s2 / tot
    rows = []
    for e in range(N_EXPERTS):
        gi, i = divmod(e, per)
        hit = jnp.where(best == gi, jnp.where(i1 == i, g1, jnp.where(i2 == i, g2, 0.0)), 0.0)
        rows.append(hit)
    return jnp.concatenate(rows, axis=0)


def _outproj_kernel(ya_ref, yb_ref, yc_ref, yd_ref, w_ref, h_ref, g_ref, b_ref, rwt_ref, rb_ref, o_ref, comb_ref):
    gw = GROUP_WIDTH
    mix = (_dot(ya_ref[0], w_ref[0:gw, :]) + _dot(yb_ref[0], w_ref[gw:2 * gw, :])
           + _dot(yc_ref[0], w_ref[2 * gw:3 * gw, :]) + _dot(yd_ref[0], w_ref[3 * gw:4 * gw, :]))
    h1 = _ln(ALPHA * h_ref[...] + mix, g_ref[...], b_ref[...])
    o_ref[...] = h1
    wh, wl = _split2(rwt_ref[...])
    hh, hl = _split2(h1)
    logit_t = _dot_nt(wh, hh) + _dot_nt(wh, hl) + _dot_nt(wl, hh)
    comb_ref[...] = _router(logit_t, rb_ref)


def _outproj(ya, yb, yc, yd, w_bf, h, g, b, rwt, rb_b):
    bsz, lp, _ = ya.shape
    d = h.shape[1]
    tm = TQ
    nt = lp // tm
    spec = lambda w: pl.BlockSpec((1, tm, w), lambda i, j: (i, j, 0))
    flat = lambda w: pl.BlockSpec((tm, w), lambda i, j: (i * nt + j, 0))
    const = lambda shape: pl.BlockSpec(shape, lambda i, j: (0, 0))
    return pl.pallas_call(
        _outproj_kernel,
        grid=(bsz, nt),
        in_specs=[spec(GROUP_WIDTH)] * 4 + [const((d, d)), flat(d), const((1, d)), const((1, d)),
                                            const((N_EXPERTS, d)), const((N_EXPERTS, BLK))],
        out_specs=[flat(d), pl.BlockSpec((N_EXPERTS, tm), lambda i, j: (0, i * nt + j))],
        out_shape=[jax.ShapeDtypeStruct((bsz * lp, d), F32), jax.ShapeDtypeStruct((N_EXPERTS, bsz * lp), F32)],
        compiler_params=_params("parallel", "parallel"),
        name="out_proj_ln_router",
    )(ya, yb, yc, yd, w_bf, h, g.reshape(1, d), b.reshape(1, d), rwt, rb_b)


def _route_meta(comb, tm, lp, nreal):
    t = comb.shape[1]
    per = N_EXPERTS // N_GROUPS
    cg = comb.reshape(N_GROUPS, per, t)
    grp = jnp.argmax(jnp.sum(cg, 1), axis=0).astype(jnp.int32)
    c4 = jnp.take_along_axis(cg, grp[None, None, :], axis=0)[0]
    nz = c4 > 0
    lo = jnp.argmax(nz, axis=0).astype(jnp.int32)
    hi = (per - 1 - jnp.argmax(nz[::-1], axis=0)).astype(jnp.int32)
    one = lo == hi
    lo, hi = jnp.where(one & (lo == per - 1), per - 2, lo), jnp.where(one, jnp.minimum(lo + 1, per - 1), hi)
    w_lo = jnp.take_along_axis(c4, lo[None, :], axis=0)[0]
    w_hi = jnp.take_along_axis(c4, hi[None, :], axis=0)[0]
    pair_id = jnp.asarray(np.array([[0, 0, 1, 2], [0, 0, 3, 4], [0, 0, 0, 5], [0, 0, 0, 0]], np.int32))
    ncls = N_GROUPS * MOE_PAIRS
    cls = jnp.where(jnp.arange(t) % lp < nreal, grp * MOE_PAIRS + pair_id[lo, hi], ncls)
    order = jnp.argsort(cls, stable=True).astype(jnp.int32)
    counts = jnp.sum((cls[:, None] == jnp.arange(ncls)[None, :]).astype(jnp.int32), axis=0)
    start = jnp.cumsum(counts) - counts
    pcounts = (counts + tm - 1) // tm * tm
    pend = jnp.cumsum(pcounts)
    pstart = pend - pcounts
    nt = pl.cdiv(t // lp * nreal, tm) + ncls
    s = jnp.arange(nt * tm, dtype=jnp.int32)
    gs = jnp.minimum(jnp.searchsorted(pend, s, side="right"), ncls - 1).astype(jnp.int32)
    r = s - pstart[gs]
    valid = r < counts[gs]
    tok = order[jnp.clip(start[gs] + r, 0, t - 1)]
    src = jnp.where(valid, tok, -1)
    c2 = jnp.where(valid[:, None], jnp.stack([w_lo[tok], w_hi[tok]], axis=-1), 0.0)
    n_tiles = (pend[-1] // tm).astype(jnp.int32)
    tcls = gs[::tm][jnp.minimum(jnp.arange(nt), n_tiles - 1)]
    pair_lo = jnp.asarray(np.array([0, 0, 0, 1, 1, 2], np.int32))
    pair_hi = jnp.asarray(np.array([1, 2, 3, 2, 3, 3], np.int32))
    e_lo = (tcls // MOE_PAIRS * per + pair_lo[tcls % MOE_PAIRS]).astype(jnp.int32)
    e_hi = (tcls // MOE_PAIRS * per + pair_hi[tcls % MOE_PAIRS]).astype(jnp.int32)
    return src, e_lo, e_hi, n_tiles.reshape(1), c2, nt


def _moe_kernel(src_ref, elo_ref, ehi_ref, nt_ref, h_hbm, c_ref, w1a_ref, w3a_ref, w2a_ref, w1b_ref, w3b_ref, w2b_ref,
                g_ref, b_ref, o_hbm, xbuf, obuf, gsem, ssem, *, tm, bsz, lp, nreal):
    i = pl.program_id(0)
    nt = nt_ref[0]
    slot = lax.rem(i, 2)
    npad = lp - nreal

    def pad_row(q):
        return (q // npad) * lp + nreal + q % npad

    def gather(tile, sl):
        base = tile * tm
        for r in range(tm):
            tok = jnp.maximum(src_ref[base + r], 0)
            pltpu.make_async_copy(h_hbm.at[pl.ds(tok, 1)], xbuf.at[sl, pl.ds(r, 1)], gsem.at[sl]).start()

    def scatter(sl):
        base = i * tm
        for r in range(tm):
            s = src_ref[base + r]
            dst = jnp.where(s >= 0, s, pad_row(sl * tm + r))
            pltpu.make_async_copy(obuf.at[sl, pl.ds(r, 1)], o_hbm.at[pl.ds(dst, 1)], ssem.at[sl]).start()

    def wait_gather(sl):
        pltpu.make_async_copy(h_hbm.at[pl.ds(0, tm)], xbuf.at[sl], gsem.at[sl]).wait()

    def wait_scatter(sl):
        pltpu.make_async_copy(obuf.at[sl], o_hbm.at[pl.ds(0, tm)], ssem.at[sl]).wait()

    @pl.when(i == 0)
    def _():
        gather(0, 0)
        obuf[0, 0:npad, :] = jnp.zeros((npad, obuf.shape[2]), F32)
        for bi in range(bsz):
            fill = pltpu.make_async_copy(obuf.at[0, pl.ds(0, npad)], o_hbm.at[pl.ds(bi * lp + nreal, npad)],
                                         ssem.at[0])
            fill.start()
            fill.wait()

    for sl in range(2):
        @pl.when((i + 1 < nt) & (slot == 1 - sl))
        def _(sl=sl):
            gather(i + 1, sl)

    @pl.when(i < nt)
    def _():
        wait_gather(slot)

        @pl.when(i >= 2)
        def _():
            wait_scatter(slot)

        x = xbuf[slot]
        xb = x.astype(BF16)
        c = c_ref[...]
        y = jnp.zeros(x.shape, F32)
        for e, (w1_ref, w3_ref, w2_ref) in enumerate(((w1a_ref, w3a_ref, w2a_ref), (w1b_ref, w3b_ref, w2b_ref))):
            a1 = _dot(xb, w1_ref[0])
            act = (a1 * _sigmoid(a1)) * _dot(xb, w3_ref[0]) * c[:, e:e + 1]
            y = y + _dot(act.astype(BF16), w2_ref[0])
        obuf[slot] = _ln(ALPHA * x + y, g_ref[...], b_ref[...])

        for sl in range(2):
            @pl.when(slot == sl)
            def _(sl=sl):
                scatter(sl)

    @pl.when(i == nt - 1)
    def _():
        wait_scatter(slot)

        @pl.when(nt >= 2)
        def _():
            wait_scatter(1 - slot)


def _moe(h1, comb, w1, w3, w2, g, b, bsz, lp, nreal):
    t, d = h1.shape
    tm = MOE_TM
    assert bsz * (lp - nreal) >= 2 * tm and lp - nreal <= tm
    src, e_lo, e_hi, n_tiles, c2, nt = _route_meta(comb, tm, lp, nreal)
    const = lambda i, src, elo, ehi, ntl: (0, 0)
    wa = lambda a, b_: pl.BlockSpec((1, a, b_), lambda i, src, elo, ehi, ntl: (elo[i], 0, 0))
    wb = lambda a, b_: pl.BlockSpec((1, a, b_), lambda i, src, elo, ehi, ntl: (ehi[i], 0, 0))
    grid_spec = pltpu.PrefetchScalarGridSpec(
        num_scalar_prefetch=4,
        grid=(nt,),
        in_specs=[
            pl.BlockSpec(memory_space=pl.ANY),
            pl.BlockSpec((tm, 2), lambda i, src, elo, ehi, ntl: (i, 0)),
            wa(d, EXPERT_FF), wa(d, EXPERT_FF), wa(EXPERT_FF, d),
            wb(d, EXPERT_FF), wb(d, EXPERT_FF), wb(EXPERT_FF, d),
            pl.BlockSpec((1, d), const),
            pl.BlockSpec((1, d), const),
        ],
        out_specs=pl.BlockSpec(memory_space=pl.ANY),
        scratch_shapes=[
            pltpu.VMEM((2, tm, d), F32), pltpu.VMEM((2, tm, d), F32),
            pltpu.SemaphoreType.DMA((2,)), pltpu.SemaphoreType.DMA((2,)),
        ],
    )
    return pl.pallas_call(
        functools.partial(_moe_kernel, tm=tm, bsz=bsz, lp=lp, nreal=nreal),
        grid_spec=grid_spec,
        out_shape=jax.ShapeDtypeStruct((t, d), F32),
        compiler_params=_params("arbitrary"),
        name="moe_ln",
    )(src, e_lo, e_hi, n_tiles, h1, c2, w1, w3, w2, w1, w3, w2, g.reshape(1, d), b.reshape(1, d))


def kernel(x, meta, ln0_g, ln0_b, w_in, swa_sinks, rel_bias, diff_lq1, diff_lk1, diff_lq2, diff_lk2, diff_subln_g,
           rwkv_mu, rwkv_w0, rwkv_w2, rwkv_a0, rwkv_a2, rwkv_g2, rwkv_kk, rwkv_ka, rwkv_rk, rwkv_lnx_g,
           rwkv_lnx_b, conv_w, conv_b, conv_gn_g, conv_gn_b, w_out, ln1_g, ln1_b, router_w, router_b, exp_w1,
           exp_w3, exp_w2, ln2_g, ln2_b):
    bsz, seq, d = x.shape
    assert d == D_MODEL and seq % BLK == 0
    lp = pl.cdiv(seq + N_META, TQ) * TQ

    r = jnp.arange(BLK)
    bkt_band = _t5_bucket(r[:, None] - jnp.arange(2 * BLK)[None, :] + BLK)
    qpos3 = jnp.arange(3 * BLK).reshape(3, BLK)
    bkt_meta = _t5_bucket(qpos3[:, :, None] - jnp.arange(N_META)[None, None, :])
    dq0 = r[None, :] - r[:, None]
    bkt_d = jnp.stack([_t5_bucket(dq0), _t5_bucket(dq0 + BLK)])
    rel_a, rel_b = rel_bias[:, :N_HEADS], rel_bias[:, N_HEADS:]
    rwt = router_w.T
    rb_b = jnp.broadcast_to(router_b.astype(F32)[:, None], (N_EXPERTS, BLK))
    vcols = slice(SWA_IN + 512, SWA_IN + 768)

    h = _prep(x, meta, ln0_g, ln0_b, lp).reshape(bsz * lp, d)
    for l in range(DEPTH):
        w_l = w_in[l]
        ua, qd, kd, vdt, uc, ud = _inproj(h, w_l.astype(BF16), w_l[:, vcols].T.astype(BF16), bsz, lp)
        ya = _swa(ua, rel_a, swa_sinks[l], bkt_band, bkt_meta)
        lam_init = 0.8 - 0.6 * math.exp(-0.3 * l)
        lam = (jnp.exp(jnp.sum(diff_lq1[l] * diff_lk1[l])) - jnp.exp(jnp.sum(diff_lq2[l] * diff_lk2[l])) + lam_init)
        lam2 = jnp.stack([lam, jnp.asarray(1.0 - lam_init, F32)]).astype(F32)
        yb = _diff(qd, kd, vdt, lam2, rel_b, bkt_d, diff_subln_g[l].reshape(HEAD_DIM, 1))
        yc = _rwkv(uc, rwkv_mu[l], rwkv_w0[l], rwkv_w2[l], rwkv_a0[l], rwkv_a2[l], rwkv_g2[l], rwkv_kk[l],
                   rwkv_ka[l], rwkv_rk[l], rwkv_lnx_g[l], rwkv_lnx_b[l])
        yd = _conv(ud, conv_w[l], conv_b[l], conv_gn_g[l], conv_gn_b[l])
        h1, comb = _outproj(ya, yb, yc, yd, w_out[l].astype(BF16), h, ln1_g[l], ln1_b[l], rwt, rb_b)
        h = _moe(h1, comb, exp_w1[l].astype(BF16), exp_w3[l].astype(BF16), exp_w2[l].astype(BF16),
                 ln2_g[l], ln2_b[l], bsz, lp, N_META + seq)
    return h.reshape(bsz, lp, d)[:, N_META:N_META + seq]
```

```python
import functools
import math

import jax
import jax.numpy as jnp
import numpy as np
from jax import lax
from jax.experimental import pallas as pl
from jax.experimental.pallas import tpu as pltpu

F32 = jnp.float32
BF16 = jnp.bfloat16

D_MODEL = 1024
N_META = 16
BLK = 128
TQ = 3 * BLK
HEAD_DIM = 64
GROUP_WIDTH = 256
N_HEADS = 4
SWA_KV_HEADS = 2
DIFF_QK = 32
DECAY_LORA = 64
AAA_LORA = 64
GATE_LORA = 128
CONV_WIDTH = 31
CONV_HIST = 32
REL_BUCKETS = 32
REL_MAX_DIST = 128
N_EXPERTS = 16
N_GROUPS = 4
EXPERT_FF = 512
DEPTH = 2
ALPHA = (2 * DEPTH) ** 0.25
SWA_IN = 512
DIFF_IN = 768
RWKV_IN = 1024
CONV_IN = 512
IN_WIDTH = SWA_IN + DIFF_IN + RWKV_IN + CONV_IN
NEG = -1e30
LOG2E = math.log2(math.e)
CHUNK = 64
RSTEP = TQ
MOE_TM = 256
MOE_PAIRS = 6
VMEM_LIMIT = 56 * 1024 * 1024


def _dot(a, b, prec=None):
    return jnp.dot(a, b, preferred_element_type=F32, precision=prec)


def _dot_nt(a, b, prec=None):
    return lax.dot_general(a, b, (((1,), (1,)), ((), ())), preferred_element_type=F32, precision=prec)


def _dot_tn(a, b, prec=None):
    return lax.dot_general(a, b, (((0,), (0,)), ((), ())), preferred_element_type=F32, precision=prec)


def _split2(x):
    hi = x.astype(BF16)
    return hi, (x - hi.astype(F32)).astype(BF16)


def _dot_exact_rhs(x, m_bf):
    hi, lo = _split2(x)
    return _dot(hi, m_bf) + _dot(lo, m_bf)


def _dot_exact_lhs(m_bf, x):
    hi, lo = _split2(x)
    return _dot(m_bf, hi) + _dot(m_bf, lo)


def _dot3(a, b):
    ah, al = _split2(a)
    bh, bl = _split2(b)
    return _dot(ah, bh) + _dot(ah, bl) + _dot(al, bh)


def _ln(x, g, b, eps=1e-5):
    mu = jnp.mean(x, -1, keepdims=True)
    xc = x - mu
    var = jnp.mean(xc * xc, -1, keepdims=True)
    return xc * lax.rsqrt(var + eps) * g + b


def _sigmoid(x):
    return 1.0 / (1.0 + jnp.exp(-x))


def _pick_tile(n, candidates):
    for c in candidates:
        if n % c == 0:
            return c
    raise ValueError(f"no tile in {candidates} divides {n}")


def _params(*sem):
    return pltpu.CompilerParams(dimension_semantics=sem, vmem_limit_bytes=VMEM_LIMIT)


def _t5_bucket(dist):
    n = jnp.maximum(dist, 0)
    max_exact = REL_BUCKETS // 2
    log_ratio = jnp.log(jnp.maximum(n, 1).astype(F32) / max_exact) / math.log(REL_MAX_DIST / max_exact)
    large = jnp.minimum(max_exact + (log_ratio * (REL_BUCKETS - max_exact)).astype(jnp.int32), REL_BUCKETS - 1)
    return jnp.where(n < max_exact, n, large)


def _gather_bias(bkt, rel_ref, h, offset=0.0):
    acc = jnp.zeros(bkt.shape, F32)
    for b in range(REL_BUCKETS):
        acc = jnp.where(bkt == b, rel_ref[b, h] - offset, acc)
    return acc


def _prep_kernel(xa_ref, xb_ref, meta_ref, g_ref, b_ref, o_ref, *, nbx):
    j = pl.program_id(1)
    head = jnp.where(j == 0, meta_ref[...], xa_ref[0])
    body = _ln(xb_ref[0, 0:BLK - N_META, :], g_ref[...], b_ref[...])
    o_ref[0, 0:N_META, :] = jnp.where(j <= nbx, _ln(head, g_ref[...], b_ref[...]), 0.0)
    o_ref[0, N_META:BLK, :] = jnp.where(j < nbx, body, 0.0)


def _prep(x, meta, g, b, lp):
    bsz, seq, d = x.shape
    nbx = seq // BLK
    per = BLK // N_META
    return pl.pallas_call(
        functools.partial(_prep_kernel, nbx=nbx),
        grid=(bsz, lp // BLK),
        in_specs=[
            pl.BlockSpec((1, N_META, d), lambda i, j: (i, jnp.clip(per * j - 1, 0, per * nbx - 1), 0)),
            pl.BlockSpec((1, BLK, d), lambda i, j: (i, jnp.minimum(j, nbx - 1), 0)),
            pl.BlockSpec((N_META, d), lambda i, j: (0, 0)),
            pl.BlockSpec((1, d), lambda i, j: (0, 0)),
            pl.BlockSpec((1, d), lambda i, j: (0, 0)),
        ],
        out_specs=pl.BlockSpec((1, BLK, d), lambda i, j: (i, j, 0)),
        out_shape=jax.ShapeDtypeStruct((bsz, lp, d), F32),
        compiler_params=_params("parallel", "parallel"),
        name="prep_ln",
    )(x, x, meta, g.reshape(1, d), b.reshape(1, d))


def _inproj_kernel(h_ref, w_ref, wvt_ref, ua_ref, qd_ref, kd_ref, vdt_ref, uc_ref, ud_ref):
    hb = h_ref[...].astype(BF16)

    def proj(c0, c1):
        return _dot(hb, w_ref[:, c0:c1])

    ua_ref[0, :, 0:256] = (proj(0, 256) * (HEAD_DIM ** -0.5)).astype(BF16)
    ua_ref[0, :, 256:512] = proj(256, 512).astype(BF16)
    qd_ref[0] = (proj(512, 768) * (DIFF_QK ** -0.5 * LOG2E)).astype(BF16)
    kd_ref[0] = proj(768, 1024).astype(BF16)
    vdt_ref[0, 0] = _dot_nt(wvt_ref[...], hb).astype(BF16)
    uc_ref[0] = proj(1280, 2304)
    ud_ref[0] = proj(2304, 2816)


def _inproj(h, w_bf, wvt_bf, bsz, lp):
    d = h.shape[1]
    tm = TQ
    nt = lp // tm
    row = lambda w, dt: jax.ShapeDtypeStruct((bsz, lp, w), dt)
    spec = lambda w: pl.BlockSpec((1, tm, w), lambda i, j: (i, j, 0))
    return pl.pallas_call(
        _inproj_kernel,
        grid=(bsz, nt),
        in_specs=[
            pl.BlockSpec((tm, d), lambda i, j: (i * nt + j, 0)),
            pl.BlockSpec((d, IN_WIDTH), lambda i, j: (0, 0)),
            pl.BlockSpec((GROUP_WIDTH, d), lambda i, j: (0, 0)),
        ],
        out_specs=[
            spec(512), spec(256), spec(256),
            pl.BlockSpec((1, 1, GROUP_WIDTH, tm), lambda i, j: (i, j, 0, 0)),
            spec(1024), spec(512),
        ],
        out_shape=[
            row(512, BF16), row(256, BF16), row(256, BF16),
            jax.ShapeDtypeStruct((bsz, nt, GROUP_WIDTH, tm), BF16),
            row(1024, F32), row(512, F32),
        ],
        compiler_params=_params("parallel", "parallel"),
        name="in_proj",
    )(h, w_bf, wvt_bf)


def _swa_kernel(rel_ref, sink_ref, bband_ref, bmeta_ref, cur_ref, prev_ref, meta_ref, o_ref, band_sc, metab_sc):
    j = pl.program_id(1)

    @pl.when(j == 0)
    def _():
        for h in range(N_HEADS):
            band_sc[h] = _gather_bias(bband_ref[...], rel_ref, h)

        for n in range(3):
            for h in range(N_HEADS):
                metab_sc[n, h] = _gather_bias(bmeta_ref[n], rel_ref, h)

    nsub = TQ // BLK
    r = lax.broadcasted_iota(jnp.int32, (BLK, 1), 0)
    dq = r - lax.broadcasted_iota(jnp.int32, (1, 2 * BLK), 1) + BLK
    in_window = (dq >= 0) & (dq < BLK)
    meta = meta_ref[0, 0:N_META, :]
    for u in range(nsub):
        cur = cur_ref[0, u * BLK:(u + 1) * BLK, :]
        prev = prev_ref[0] if u == 0 else cur_ref[0, (u - 1) * BLK:u * BLK, :]
        qpos = (j * nsub + u) * BLK + r
        ok_meta = qpos >= lax.broadcasted_iota(jnp.int32, (1, N_META), 1)
        ok_band = in_window & (qpos - dq >= N_META)
        mtab = jnp.where(j == 0, u, 2)
        outs = []
        for h in range(N_HEADS):
            g = h // (N_HEADS // SWA_KV_HEADS)
            kc, vc = 256 + HEAD_DIM * g, 384 + HEAD_DIM * g
            q = cur[:, HEAD_DIM * h:HEAD_DIM * (h + 1)]
            kcat = jnp.concatenate([prev[:, kc:kc + HEAD_DIM], cur[:, kc:kc + HEAD_DIM]], axis=0)
            vcat = jnp.concatenate([prev[:, vc:vc + HEAD_DIM], cur[:, vc:vc + HEAD_DIM]], axis=0)
            s_m = jnp.where(ok_meta, _dot_nt(q, meta[:, kc:kc + HEAD_DIM]) + metab_sc[mtab, h], NEG)
            s_b = jnp.where(ok_band, _dot_nt(q, kcat) + band_sc[h], NEG)
            sink = sink_ref[h]
            m = jnp.maximum(jnp.maximum(jnp.max(s_m, -1, keepdims=True), jnp.max(s_b, -1, keepdims=True)), sink)
            p_m, p_b = jnp.exp(s_m - m), jnp.exp(s_b - m)
            den = jnp.sum(p_m, -1, keepdims=True) + jnp.sum(p_b, -1, keepdims=True) + jnp.exp(sink - m)
            o = _dot(p_m.astype(BF16), meta[:, vc:vc + HEAD_DIM]) + _dot(p_b.astype(BF16), vcat)
            outs.append(o / den)
        o_ref[0, u * BLK:(u + 1) * BLK, :] = jnp.concatenate(outs, axis=1).astype(BF16)


def _swa(ua, rel_a, sinks, bkt_band, bkt_meta):
    bsz, lp, _ = ua.shape
    nsub = TQ // BLK
    smem = pl.BlockSpec(memory_space=pltpu.SMEM)
    return pl.pallas_call(
        _swa_kernel,
        grid=(bsz, lp // TQ),
        in_specs=[
            smem, smem,
            pl.BlockSpec((BLK, 2 * BLK), lambda i, j: (0, 0)),
            pl.BlockSpec((3, BLK, N_META), lambda i, j: (0, 0, 0)),
            pl.BlockSpec((1, TQ, SWA_IN), lambda i, j: (i, j, 0)),
            pl.BlockSpec((1, BLK, SWA_IN), lambda i, j: (i, jnp.maximum(j * nsub - 1, 0), 0)),
            pl.BlockSpec((1, BLK, SWA_IN), lambda i, j: (i, 0, 0)),
        ],
        out_specs=pl.BlockSpec((1, TQ, GROUP_WIDTH), lambda i, j: (i, j, 0)),
        out_shape=jax.ShapeDtypeStruct((bsz, lp, GROUP_WIDTH), BF16),
        scratch_shapes=[pltpu.VMEM((N_HEADS, BLK, 2 * BLK), F32), pltpu.VMEM((3, N_HEADS, BLK, N_META), F32)],
        compiler_params=_params("arbitrary", "arbitrary"),
        name="swa_attn",
    )(rel_a, sinks, bkt_band, bkt_meta, ua, ua, ua)


def _diff_kernel(lam_ref, rel_ref, bkt_ref, g_ref, q_ref, k_ref, vt_ref, o_ref, wt_sc, b1_sc, addm_sc, m_sc, acc_sc):
    i = pl.program_id(1)
    nsub = TQ // BLK

    @pl.when(i == 0)
    def _():
        kr = lax.broadcasted_iota(jnp.int32, (BLK, 1), 0)
        qc = lax.broadcasted_iota(jnp.int32, (1, BLK), 1)
        for h in range(N_HEADS):
            far = rel_ref[REL_BUCKETS - 1, h]
            b0 = jnp.where(qc >= kr, _gather_bias(bkt_ref[0], rel_ref, h, far) * LOG2E, NEG)
            b1 = _gather_bias(bkt_ref[1], rel_ref, h, far) * LOG2E
            b1_sc[h] = b1
            blocks = {0: b0, 1: b1, 2: jnp.zeros((BLK, BLK), F32)}
            masked = jnp.full((BLK, BLK), NEG, F32)
            for u in range(nsub):
                row = [blocks[w - u] if w >= u else masked for w in range(nsub)]
                addm_sc[h, u * BLK:(u + 1) * BLK, :] = jnp.concatenate(row + row, axis=1)

    q = q_ref[0]
    lane_grp = lax.broadcasted_iota(jnp.int32, (1, GROUP_WIDTH), 1) // DIFF_QK
    zero = jnp.zeros_like(q)
    for h in range(N_HEADS):
        wt_sc[h, 0:TQ, :] = jnp.where(lane_grp == 2 * h, q, zero)
        wt_sc[h, TQ:2 * TQ, :] = jnp.where(lane_grp == 2 * h + 1, q, zero)
    m_sc[...] = jnp.full(m_sc.shape, NEG, F32)
    acc_sc[...] = jnp.zeros(acc_sc.shape, F32)

    ones_rows = jnp.ones((2 * 8, TQ), BF16)

    def scores(t, h):
        return _dot_nt(k_ref[0, pl.ds(pl.multiple_of(t * TQ, TQ), TQ), :], wt_sc[h])

    def consume(t, h, s, kind):
        if kind == "diag":
            s = s + addm_sc[h]
        elif kind == "near":
            top, bot = s[0:TQ - BLK], s[TQ - BLK:TQ]
            b1 = b1_sc[h]
            bot = jnp.concatenate([bot[:, 0:BLK] + b1, bot[:, BLK:TQ], bot[:, TQ:TQ + BLK] + b1,
                                   bot[:, TQ + BLK:2 * TQ]], axis=1)
            s = jnp.concatenate([top, bot], axis=0)
        m_old = m_sc[h]
        cm = jnp.maximum(jnp.maximum(s[0:BLK], s[BLK:2 * BLK]), s[2 * BLK:3 * BLK])
        n = BLK
        while n > 8:
            n //= 2
            cm = jnp.maximum(cm[0:n], cm[n:2 * n])
        m_new = jnp.maximum(m_old, jnp.max(cm, 0, keepdims=True))
        a = jnp.exp2(m_old - m_new)
        p = jnp.exp2(s - m_new).astype(BF16)
        vth = vt_ref[0, t, HEAD_DIM * h:HEAD_DIM * (h + 1), :]
        pv = _dot(jnp.concatenate([vth, ones_rows], axis=0), p)
        acc_sc[h] = a * acc_sc[h] + pv
        m_sc[h] = m_new

    def tile(t, kind):
        for h in range(N_HEADS):
            consume(t, h, scores(t, h), kind)

    def far(t, c):
        tile(t, None)
        return c

    lax.fori_loop(0, i - 1, far, 0)

    @pl.when(i >= 1)
    def _():
        tile(i - 1, "near")

    tile(i, "diag")

    lam = lam_ref[0]
    post = lam_ref[1]
    outs = []
    for h in range(N_HEADS):
        acc = acc_sc[h]
        o = acc[0:HEAD_DIM] / acc[HEAD_DIM:HEAD_DIM + 1]
        o = o[:, 0:TQ] - lam * o[:, TQ:2 * TQ]
        ms = jnp.mean(o * o, 0, keepdims=True)
        outs.append(o * lax.rsqrt(ms + 1e-5) * g_ref[...] * post)
    ot = jnp.concatenate(outs, axis=0).astype(BF16)
    eye = (lax.broadcasted_iota(jnp.int32, (TQ, TQ), 0) == lax.broadcasted_iota(jnp.int32, (TQ, TQ), 1))
    o_ref[0] = _dot_nt(eye.astype(BF16), ot).astype(BF16)


def _diff(qd, kd, vdt, lam2, rel_b, bkt_d, g_b):
    bsz, lp, _ = qd.shape
    nq = lp // TQ
    smem = pl.BlockSpec(memory_space=pltpu.SMEM)
    return pl.pallas_call(
        _diff_kernel,
        grid=(bsz, nq),
        in_specs=[
            smem, smem,
            pl.BlockSpec((2, BLK, BLK), lambda b, i: (0, 0, 0)),
            pl.BlockSpec((HEAD_DIM, 1), lambda b, i: (0, 0)),
            pl.BlockSpec((1, TQ, GROUP_WIDTH), lambda b, i: (b, i, 0)),
            pl.BlockSpec((1, lp, GROUP_WIDTH), lambda b, i: (b, 0, 0)),
            pl.BlockSpec((1, nq, GROUP_WIDTH, TQ), lambda b, i: (b, 0, 0, 0)),
        ],
        out_specs=pl.BlockSpec((1, TQ, GROUP_WIDTH), lambda b, i: (b, i, 0)),
        out_shape=jax.ShapeDtypeStruct((bsz, lp, GROUP_WIDTH), BF16),
        scratch_shapes=[
            pltpu.VMEM((N_HEADS, 2 * TQ, GROUP_WIDTH), BF16),
            pltpu.VMEM((N_HEADS, BLK, BLK), F32),
            pltpu.VMEM((N_HEADS, TQ, 2 * TQ), F32),
            pltpu.VMEM((N_HEADS, 1, 2 * TQ), F32),
            pltpu.VMEM((N_HEADS, HEAD_DIM + 2 * 8, 2 * TQ), F32),
        ],
        compiler_params=_params("arbitrary", "arbitrary"),
        name="diff_attn",
    )(lam2, rel_b, bkt_d, g_b, qd, kd, vdt)


def _rwkv_kernel(x_ref, xp_ref, mu_ref, w0_ref, w2_ref, a0_ref, a2_ref, g2_ref, kk_ref, ka_ref, rk_ref,
                 lg_ref, lb_ref, o_ref, zt_sc):
    c = pl.program_id(1)
    n = CHUNK
    ns = RSTEP // CHUNK
    gw = GROUP_WIDTH

    @pl.when(c == 0)
    def _():
        zt_sc[...] = jnp.zeros(zt_sc.shape, F32)

    x = x_ref[0]
    row = lax.broadcasted_iota(jnp.int32, (RSTEP, 1), 0)
    last_prev = jnp.where(c == 0, 0.0, xp_ref[0, 7:8, :])
    xs = jnp.where(row == 0, last_prev, pltpu.roll(x, 1, 0))
    xm = x + mu_ref[...] * (xs - x)
    r, k, v = xm[:, 0:gw], xm[:, gw:2 * gw], xm[:, 2 * gw:3 * gw]
    xw = xm[:, 768:768 + DECAY_LORA]
    xa = xm[:, 832:832 + AAA_LORA]
    xg = xm[:, 896:896 + GATE_LORA]
    logw = -math.exp(-0.5) * _sigmoid(w0_ref[...] + _dot3(jnp.tanh(xw), w2_ref[...]))
    a = _sigmoid(a0_ref[...] + _dot3(xa, a2_ref[...]))
    g = _dot3(_sigmoid(xg), g2_ref[...])

    lane_head = lax.broadcasted_iota(jnp.int32, (1, gw), 1) // HEAD_DIM
    sub_head = lax.broadcasted_iota(jnp.int32, (gw, 1), 0) // HEAD_DIM
    head_ones = (sub_head == lane_head).astype(BF16)

    kk = k * kk_ref[...]
    kk = kk / jnp.maximum(jnp.sqrt(_dot_exact_rhs(kk * kk, head_ones)), 1e-12)
    k2 = k * (1.0 + (a - 1.0) * ka_ref[...])
    bvec = kk * a

    ri = lax.broadcasted_iota(jnp.int32, (RSTEP, RSTEP), 0)
    ci = lax.broadcasted_iota(jnp.int32, (RSTEP, RSTEP), 1)
    tri = (((ri // n) == (ci // n)) & (ri >= ci)).astype(BF16)
    cum = _dot_exact_lhs(tri, logw)
    tot = jnp.concatenate([jnp.broadcast_to(cum[(s + 1) * n - 1:(s + 1) * n, :], (n, gw)) for s in range(ns)], axis=0)
    e_neg = jnp.exp(-cum)
    e_last = jnp.exp(tot - cum)
    a_t = (-kk * jnp.exp(cum - logw)).astype(BF16)
    r_t = (r * jnp.exp(cum)).astype(BF16)
    b_t, k_t = (bvec * e_neg).astype(BF16), (k2 * e_neg).astype(BF16)
    b_h, k_h = (bvec * e_last).astype(BF16), (k2 * e_last).astype(BF16)
    g_c = jnp.exp(tot)
    vb = v.astype(BF16)

    def stack(t, s):
        ts = t[s * n:(s + 1) * n]
        return jnp.concatenate([jnp.where(lane_head == h, ts, jnp.zeros_like(ts)) for h in range(N_HEADS)], axis=0)

    rr = lax.broadcasted_iota(jnp.int32, (gw, gw), 0)
    cc = lax.broadcasted_iota(jnp.int32, (gw, gw), 1)
    parts = []
    for s in range(ns):
        ast, rst, bst, kst = stack(a_t, s), stack(r_t, s), stack(b_t, s), stack(k_t, s)
        gmat = _dot_nt(jnp.concatenate([ast, rst], axis=0), jnp.concatenate([bst, kst], axis=0))
        low = jnp.where(rr > cc, gmat[0:gw, 0:gw], 0.0)
        aak = jnp.where(rr > cc, gmat[0:gw, gw:2 * gw], 0.0).astype(BF16)
        arb = jnp.where(rr >= cc, gmat[gw:2 * gw, 0:gw], 0.0).astype(BF16)
        ark = jnp.where(rr >= cc, gmat[gw:2 * gw, gw:2 * gw], 0.0).astype(BF16)
        pw = low
        tinv = jnp.where(rr == cc, 1.0, low)
        for _ in range(5):
            pwb = pw.astype(BF16)
            pw = _dot(pwb, pwb)
            tinv = tinv + _dot(tinv.astype(BF16), pw.astype(BF16))
        parts.append((ast, rst, aak, arb, ark, tinv.astype(BF16), stack(b_h, s), stack(k_h, s), stack(vb, s)))

    zt = zt_sc[...]
    ys = []
    for s in range(ns):
        ast, rst, aak, arb, ark, tinv, bhs, khs, vst = parts[s]
        ztb = zt.astype(BF16)
        u = _dot(tinv, (_dot_nt(ast, ztb) + _dot(aak, vst)).astype(BF16)).astype(BF16)
        ybd = _dot_nt(rst, ztb) + _dot(arb, u) + _dot(ark, vst)
        zt = zt * g_c[s * n:s * n + 1, :] + _dot_tn(u, bhs) + _dot_tn(vst, khs)
        ys.append(ybd[0:n] + ybd[n:2 * n] + ybd[2 * n:3 * n] + ybd[3 * n:4 * n])
    zt_sc[...] = zt
    y = jnp.concatenate(ys, axis=0)

    mean = _dot_exact_rhs(y, head_ones) * (1.0 / HEAD_DIM)
    yc = y - mean
    var = _dot_exact_rhs(yc * yc, head_ones) * (1.0 / HEAD_DIM)
    yn = yc * lax.rsqrt(var + 64e-5) * lg_ref[...] + lb_ref[...]
    bonus = _dot_exact_rhs(r * k2 * rk_ref[...], head_ones) * v
    o_ref[0] = ((yn + bonus) * g).astype(BF16)


def _rwkv(uc, mu, w0, w2, a0, a2, g2, k_k, k_a, r_k, lnx_g, lnx_b):
    bsz, lp, w = uc.shape
    nc = lp // RSTEP
    vec = lambda t: t.reshape(1, -1)
    full = lambda t: pl.BlockSpec(t.shape, lambda b, c: (0,) * t.ndim)
    args = [vec(mu), vec(w0), w2, vec(a0), a2, g2, vec(k_k), vec(k_a), vec(r_k), vec(lnx_g), vec(lnx_b)]
    return pl.pallas_call(
        _rwkv_kernel,
        grid=(bsz, nc),
        in_specs=[
            pl.BlockSpec((1, RSTEP, w), lambda b, c: (b, c, 0)),
            pl.BlockSpec((1, 8, w), lambda b, c: (b, jnp.maximum(c * (RSTEP // 8) - 1, 0), 0)),
        ] + [full(t) for t in args],
        out_specs=pl.BlockSpec((1, RSTEP, GROUP_WIDTH), lambda b, c: (b, c, 0)),
        out_shape=jax.ShapeDtypeStruct((bsz, lp, GROUP_WIDTH), BF16),
        scratch_shapes=[pltpu.VMEM((GROUP_WIDTH, GROUP_WIDTH), F32)],
        compiler_params=_params("arbitrary", "arbitrary"),
        name="rwkv7",
    )(uc, uc, *args)


def _conv_kernel(cur_ref, prev_ref, w_ref, b_ref, g_ref, gb_ref, o_ref, hcat_sc):
    j = pl.program_id(1)
    ch = GROUP_WIDTH

    def glu(t):
        return t[:, 0:ch] * _sigmoid(t[:, ch:2 * ch])

    hcat_sc[0:CONV_HIST, :] = jnp.where(j == 0, 0.0, glu(prev_ref[0]))
    hcat_sc[CONV_HIST:CONV_HIST + TQ, :] = glu(cur_ref[0])
    lead = CONV_HIST - (CONV_WIDTH - 1)
    acc = jnp.zeros((TQ, ch), F32) + b_ref[...]
    for ph in range(8):
        offs = [o for o in range(lead, CONV_HIST + 1) if o % 8 == ph]
        shifted = hcat_sc[ph:max(offs) + TQ, :]
        for o in offs:
            acc = acc + shifted[o - ph:o - ph + TQ] * w_ref[o - lead:o - lead + 1, :]
    lane_grp = lax.broadcasted_iota(jnp.int32, (1, ch), 1) // HEAD_DIM
    sub_grp = lax.broadcasted_iota(jnp.int32, (ch, 1), 0) // HEAD_DIM
    grp_ones = (sub_grp == lane_grp).astype(BF16)
    mean = _dot_exact_rhs(acc, grp_ones) * (1.0 / HEAD_DIM)
    xc = acc - mean
    var = _dot_exact_rhs(xc * xc, grp_ones) * (1.0 / HEAD_DIM)
    y = xc * lax.rsqrt(var + 1e-5) * g_ref[...] + gb_ref[...]
    o_ref[0] = (y * _sigmoid(y)).astype(BF16)


def _conv(ud, conv_w, conv_b, gn_g, gn_b):
    bsz, lp, w = ud.shape
    nb = lp // TQ
    vec = lambda t: t.reshape(1, -1)
    const = lambda shape: pl.BlockSpec(shape, lambda i, j: (0, 0))
    return pl.pallas_call(
        _conv_kernel,
        grid=(bsz, nb),
        in_specs=[
            pl.BlockSpec((1, TQ, w), lambda i, j: (i, j, 0)),
            pl.BlockSpec((1, CONV_HIST, w), lambda i, j: (i, jnp.maximum(j * (TQ // CONV_HIST) - 1, 0), 0)),
            const((CONV_WIDTH, GROUP_WIDTH)), const((1, GROUP_WIDTH)), const((1, GROUP_WIDTH)),
            const((1, GROUP_WIDTH)),
        ],
        out_specs=pl.BlockSpec((1, TQ, GROUP_WIDTH), lambda i, j: (i, j, 0)),
        out_shape=jax.ShapeDtypeStruct((bsz, lp, GROUP_WIDTH), BF16),
        scratch_shapes=[pltpu.VMEM((CONV_HIST + TQ, GROUP_WIDTH), F32)],
        compiler_params=_params("parallel", "arbitrary"),
        name="conv_module",
    )(ud, ud, conv_w, vec(conv_b), vec(gn_g), vec(gn_b))


def _router(logit_t, rb_ref):
    s = [_sigmoid(logit_t[e:e + 1, :]) for e in range(N_EXPERTS)]
    bz = [s[e] + rb_ref[e:e + 1, 0:1] for e in range(N_EXPERTS)]
    per = N_EXPERTS // N_GROUPS
    gsum = []
    for gi in range(N_GROUPS):
        a, b, c, d = bz[per * gi:per * gi + per]
        hi1, lo1, hi2, lo2 = jnp.maximum(a, b), jnp.minimum(a, b), jnp.maximum(c, d), jnp.minimum(c, d)
        gsum.append(jnp.maximum(hi1, hi2) + jnp.maximum(jnp.minimum(hi1, hi2), jnp.maximum(lo1, lo2)))
    best = jnp.zeros_like(gsum[0], dtype=jnp.int32)
    bval = gsum[0]
    for gi in range(1, N_GROUPS):
        take = gsum[gi] > bval
        best = jnp.where(take, gi, best)
        bval = jnp.where(take, gsum[gi], bval)

    def pick(vals, i):
        out = vals[i]
        for gi in range(1, N_GROUPS):
            out = jnp.where(best == gi, vals[per * gi + i], out)
        return out

    bv = [pick(bz, i) for i in range(per)]
    sv = [pick(s, i) for i in range(per)]
    i1 = jnp.zeros_like(best)
    v1 = bv[0]
    for i in range(1, per):
        take = bv[i] > v1
        i1 = jnp.where(take, i, i1)
        v1 = jnp.where(take, bv[i], v1)
    i2 = jnp.full_like(best, -1)
    v2 = jnp.full_like(v1, -jnp.inf)
    for i in range(per):
        take = (i1 != i) & (bv[i] > v2)
        i2 = jnp.where(take, i, i2)
        v2 = jnp.where(take, bv[i], v2)
    s1 = sv[0]
    s2 = sv[0]
    for i in range(1, per):
        s1 = jnp.where(i1 == i, sv[i], s1)
        s2 = jnp.where(i2 == i, sv[i], s2)
    tot = s1 + s2
    g1, g2 = s1 / tot, s2 / tot
    rows = []
    for e in range(N_EXPERTS):
        gi, i = divmod(e, per)
        hit = jnp.where(best == gi, jnp.where(i1 == i, g1, jnp.where(i2 == i, g2, 0.0)), 0.0)
        rows.append(hit)
    return jnp.concatenate(rows, axis=0)


def _outproj_kernel(ya_ref, yb_ref, yc_ref, yd_ref, w_ref, h_ref, g_ref, b_ref, rwt_ref, rb_ref, o_ref, comb_ref):
    gw = GROUP_WIDTH
    mix = (_dot(ya_ref[0], w_ref[0:gw, :]) + _dot(yb_ref[0], w_ref[gw:2 * gw, :])
           + _dot(yc_ref[0], w_ref[2 * gw:3 * gw, :]) + _dot(yd_ref[0], w_ref[3 * gw:4 * gw, :]))
    h1 = _ln(ALPHA * h_ref[...] + mix, g_ref[...], b_ref[...])
    o_ref[...] = h1
    wh, wl = _split2(rwt_ref[...])
    hh, hl = _split2(h1)
    logit_t = _dot_nt(wh, hh) + _dot_nt(wh, hl) + _dot_nt(wl, hh)
    comb_ref[...] = _router(logit_t, rb_ref)


def _outproj(ya, yb, yc, yd, w_bf, h, g, b, rwt, rb_b):
    bsz, lp, _ = ya.shape
    d = h.shape[1]
    tm = TQ
    nt = lp // tm
    spec = lambda w: pl.BlockSpec((1, tm, w), lambda i, j: (i, j, 0))
    flat = lambda w: pl.BlockSpec((tm, w), lambda i, j: (i * nt + j, 0))
    const = lambda shape: pl.BlockSpec(shape, lambda i, j: (0, 0))
    return pl.pallas_call(
        _outproj_kernel,
        grid=(bsz, nt),
        in_specs=[spec(GROUP_WIDTH)] * 4 + [const((d, d)), flat(d), const((1, d)), const((1, d)),
                                            const((N_EXPERTS, d)), const((N_EXPERTS, BLK))],
        out_specs=[flat(d), pl.BlockSpec((N_EXPERTS, tm), lambda i, j: (0, i * nt + j))],
        out_shape=[jax.ShapeDtypeStruct((bsz * lp, d), F32), jax.ShapeDtypeStruct((N_EXPERTS, bsz * lp), F32)],
        compiler_params=_params("parallel", "parallel"),
        name="out_proj_ln_router",
    )(ya, yb, yc, yd, w_bf, h, g.reshape(1, d), b.reshape(1, d), rwt, rb_b)


def _route_meta(comb, tm, lp, nreal):
    t = comb.shape[1]
    per = N_EXPERTS // N_GROUPS
    gsum = comb.reshape(N_GROUPS, per, t).sum(1)
    grp = jnp.argmax(gsum, axis=0).astype(jnp.int32)
    c4 = jnp.sum(jnp.where(jnp.arange(N_GROUPS)[:, None, None] == grp[None, None, :],
                           comb.reshape(N_GROUPS, per, t), 0.0), axis=0)
    mem = jnp.arange(per, dtype=jnp.int32)[:, None]
    lo = jnp.min(jnp.where(c4 > 0, mem, per), axis=0)
    hi = jnp.max(jnp.where(c4 > 0, mem, -1), axis=0)
    lo, hi = jnp.where(hi < 0, 0, lo), jnp.where(hi < 0, per - 1, hi)
    one = lo == hi
    lo, hi = jnp.where(one & (lo == per - 1), per - 2, lo), jnp.where(one, jnp.minimum(lo + 1, per - 1), hi)
    w_lo = jnp.sum(jnp.where(mem == lo[None, :], c4, 0.0), axis=0)
    w_hi = jnp.sum(jnp.where(mem == hi[None, :], c4, 0.0), axis=0)
    pair_base = lambda a: a * (2 * per - 1 - a) // 2
    ncls = N_GROUPS * MOE_PAIRS
    cls = jnp.where(jnp.arange(t) % lp < nreal, grp * MOE_PAIRS + pair_base(lo) + hi - lo - 1, ncls)
    order = jnp.argsort(cls, stable=True).astype(jnp.int32)
    counts = jnp.sum((cls[:, None] == jnp.arange(ncls)[None, :]).astype(jnp.int32), axis=0)
    start = jnp.cumsum(counts) - counts
    pcounts = (counts + tm - 1) // tm * tm
    pend = jnp.cumsum(pcounts)
    nt = pl.cdiv(t // lp * nreal, tm) + ncls
    s = jnp.arange(nt * tm, dtype=jnp.int32)
    past = s[:, None] >= pend[None, :]
    gs = jnp.minimum(jnp.sum(past.astype(jnp.int32), axis=1), ncls - 1)
    skipped = jnp.sum(jnp.where(past, (pcounts - counts)[None, :], 0), axis=1)
    in_cls = s - jnp.sum(jnp.where(past, pcounts[None, :], 0), axis=1)
    valid = in_cls < jnp.sum(jnp.where(gs[:, None] == jnp.arange(ncls)[None, :], counts[None, :], 0), axis=1)
    valid = valid & (s < pend[ncls - 1])
    tok = order[jnp.clip(s - skipped, 0, t - 1)]
    src = jnp.where(valid, tok, -1)
    c2 = jnp.where(valid[:, None], jnp.stack([w_lo[tok], w_hi[tok]], axis=-1), 0.0)
    n_tiles = (pend[-1] // tm).astype(jnp.int32)
    tile_first = jnp.minimum(jnp.arange(nt, dtype=jnp.int32), n_tiles - 1) * tm
    tcls = jnp.minimum(jnp.sum((tile_first[:, None] >= pend[None, :]).astype(jnp.int32), axis=1), ncls - 1)
    pid = tcls % MOE_PAIRS
    p_lo = (pid >= pair_base(1)).astype(jnp.int32) + (pid >= pair_base(2)).astype(jnp.int32)
    p_hi = pid - pair_base(p_lo) + p_lo + 1
    e_lo = (tcls // MOE_PAIRS * per + p_lo).astype(jnp.int32)
    e_hi = (tcls // MOE_PAIRS * per + p_hi).astype(jnp.int32)
    return src, e_lo, e_hi, n_tiles.reshape(1), c2, nt


def _moe_kernel(src_ref, elo_ref, ehi_ref, nt_ref, h_hbm, c_ref, w1a_ref, w3a_ref, w2a_ref, w1b_ref, w3b_ref, w2b_ref,
                g_ref, b_ref, o_hbm, xbuf, obuf, gsem, ssem, *, tm, bsz, lp, nreal):
    i = pl.program_id(0)
    nt = nt_ref[0]
    slot = lax.rem(i, 2)
    npad = lp - nreal

    def pad_row(q):
        return (q // npad) * lp + nreal + q % npad

    def gather(tile, sl):
        base = tile * tm
        for r in range(tm):
            tok = jnp.maximum(src_ref[base + r], 0)
            pltpu.make_async_copy(h_hbm.at[pl.ds(tok, 1)], xbuf.at[sl, pl.ds(r, 1)], gsem.at[sl]).start()

    def scatter(sl):
        base = i * tm
        for r in range(tm):
            s = src_ref[base + r]
            dst = jnp.where(s >= 0, s, pad_row(sl * tm + r))
            pltpu.make_async_copy(obuf.at[sl, pl.ds(r, 1)], o_hbm.at[pl.ds(dst, 1)], ssem.at[sl]).start()

    def wait_gather(sl):
        pltpu.make_async_copy(h_hbm.at[pl.ds(0, tm)], xbuf.at[sl], gsem.at[sl]).wait()

    def wait_scatter(sl):
        pltpu.make_async_copy(obuf.at[sl], o_hbm.at[pl.ds(0, tm)], ssem.at[sl]).wait()

    @pl.when(i == 0)
    def _():
        gather(0, 0)
        obuf[0, 0:npad, :] = jnp.zeros((npad, obuf.shape[2]), F32)
        for bi in range(bsz):
            fill = pltpu.make_async_copy(obuf.at[0, pl.ds(0, npad)], o_hbm.at[pl.ds(bi * lp + nreal, npad)],
                                         ssem.at[0])
            fill.start()
            fill.wait()

    for sl in range(2):
        @pl.when((i + 1 < nt) & (slot == 1 - sl))
        def _(sl=sl):
            gather(i + 1, sl)

    @pl.when(i < nt)
    def _():
        wait_gather(slot)

        @pl.when(i >= 2)
        def _():
            wait_scatter(slot)

        x = xbuf[slot]
        xb = x.astype(BF16)
        c = c_ref[...]
        y = jnp.zeros(x.shape, F32)
        for e, (w1_ref, w3_ref, w2_ref) in enumerate(((w1a_ref, w3a_ref, w2a_ref), (w1b_ref, w3b_ref, w2b_ref))):
            a1 = _dot(xb, w1_ref[0])
            act = (a1 * _sigmoid(a1)) * _dot(xb, w3_ref[0]) * c[:, e:e + 1]
            y = y + _dot(act.astype(BF16), w2_ref[0])
        obuf[slot] = _ln(ALPHA * x + y, g_ref[...], b_ref[...])

        for sl in range(2):
            @pl.when(slot == sl)
            def _(sl=sl):
                scatter(sl)

    @pl.when(i == nt - 1)
    def _():
        wait_scatter(slot)

        @pl.when(nt >= 2)
        def _():
            wait_scatter(1 - slot)


def _moe(h1, comb, w1, w3, w2, g, b, bsz, lp, nreal):
    t, d = h1.shape
    tm = MOE_TM
    assert bsz * (lp - nreal) >= 2 * tm and lp - nreal <= tm
    src, e_lo, e_hi, n_tiles, c2, nt = _route_meta(comb, tm, lp, nreal)
    const = lambda i, src, elo, ehi, ntl: (0, 0)
    wa = lambda a, b_: pl.BlockSpec((1, a, b_), lambda i, src, elo, ehi, ntl: (elo[i], 0, 0))
    wb = lambda a, b_: pl.BlockSpec((1, a, b_), lambda i, src, elo, ehi, ntl: (ehi[i], 0, 0))
    grid_spec = pltpu.PrefetchScalarGridSpec(
        num_scalar_prefetch=4,
        grid=(nt,),
        in_specs=[
            pl.BlockSpec(memory_space=pl.ANY),
            pl.BlockSpec((tm, 2), lambda i, src, elo, ehi, ntl: (i, 0)),
            wa(d, EXPERT_FF), wa(d, EXPERT_FF), wa(EXPERT_FF, d),
            wb(d, EXPERT_FF), wb(d, EXPERT_FF), wb(EXPERT_FF, d),
            pl.BlockSpec((1, d), const),
            pl.BlockSpec((1, d), const),
        ],
        out_specs=pl.BlockSpec(memory_space=pl.ANY),
        scratch_shapes=[
            pltpu.VMEM((2, tm, d), F32), pltpu.VMEM((2, tm, d), F32),
            pltpu.SemaphoreType.DMA((2,)), pltpu.SemaphoreType.DMA((2,)),
        ],
    )
    return pl.pallas_call(
        functools.partial(_moe_kernel, tm=tm, bsz=bsz, lp=lp, nreal=nreal),
        grid_spec=grid_spec,
        out_shape=jax.ShapeDtypeStruct((t, d), F32),
        compiler_params=_params("arbitrary"),
        name="moe_ln",
    )(src, e_lo, e_hi, n_tiles, h1, c2, w1, w3, w2, w1, w3, w2, g.reshape(1, d), b.reshape(1, d))


def kernel(x, meta, ln0_g, ln0_b, w_in, swa_sinks, rel_bias, diff_lq1, diff_lk1, diff_lq2, diff_lk2, diff_subln_g,
           rwkv_mu, rwkv_w0, rwkv_w2, rwkv_a0, rwkv_a2, rwkv_g2, rwkv_kk, rwkv_ka, rwkv_rk, rwkv_lnx_g,
           rwkv_lnx_b, conv_w, conv_b, conv_gn_g, conv_gn_b, w_out, ln1_g, ln1_b, router_w, router_b, exp_w1,
           exp_w3, exp_w2, ln2_g, ln2_b):
    bsz, seq, d = x.shape
    assert d == D_MODEL and seq % BLK == 0
    lp = pl.cdiv(seq + N_META, TQ) * TQ

    r = jnp.arange(BLK)
    bkt_band = _t5_bucket(r[:, None] - jnp.arange(2 * BLK)[None, :] + BLK)
    qpos3 = jnp.arange(3 * BLK).reshape(3, BLK)
    bkt_meta = _t5_bucket(qpos3[:, :, None] - jnp.arange(N_META)[None, None, :])
    dq0 = r[None, :] - r[:, None]
    bkt_d = jnp.stack([_t5_bucket(dq0), _t5_bucket(dq0 + BLK)])
    rel_a, rel_b = rel_bias[:, :N_HEADS], rel_bias[:, N_HEADS:]
    rwt = router_w.T
    rb_b = jnp.broadcast_to(router_b.astype(F32)[:, None], (N_EXPERTS, BLK))
    vcols = slice(SWA_IN + 512, SWA_IN + 768)

    h = _prep(x, meta, ln0_g, ln0_b, lp).reshape(bsz * lp, d)
    for l in range(DEPTH):
        w_l = w_in[l]
        ua, qd, kd, vdt, uc, ud = _inproj(h, w_l.astype(BF16), w_l[:, vcols].T.astype(BF16), bsz, lp)
        ya = _swa(ua, rel_a, swa_sinks[l], bkt_band, bkt_meta)
        lam_init = 0.8 - 0.6 * math.exp(-0.3 * l)
        lam = (jnp.exp(jnp.sum(diff_lq1[l] * diff_lk1[l])) - jnp.exp(jnp.sum(diff_lq2[l] * diff_lk2[l])) + lam_init)
        lam2 = jnp.stack([lam, jnp.asarray(1.0 - lam_init, F32)]).astype(F32)
        yb = _diff(qd, kd, vdt, lam2, rel_b, bkt_d, diff_subln_g[l].reshape(HEAD_DIM, 1))
        yc = _rwkv(uc, rwkv_mu[l], rwkv_w0[l], rwkv_w2[l], rwkv_a0[l], rwkv_a2[l], rwkv_g2[l], rwkv_kk[l],
                   rwkv_ka[l], rwkv_rk[l], rwkv_lnx_g[l], rwkv_lnx_b[l])
        yd = _conv(ud, conv_w[l], conv_b[l], conv_gn_g[l], conv_gn_b[l])
        h1, comb = _outproj(ya, yb, yc, yd, w_out[l].astype(BF16), h, ln1_g[l], ln1_b[l], rwt, rb_b)
        h = _moe(h1, comb, exp_w1[l].astype(BF16), exp_w3[l].astype(BF16), exp_w2[l].astype(BF16),
                 ln2_g[l], ln2_b[l], bsz, lp, N_META + seq)
    return h.reshape(bsz, lp, d)[:, N_META:N_META + seq]
```

```python
import functools
import math

import jax
import jax.numpy as jnp
import numpy as np
from jax import lax
from jax.experimental import pallas as pl
from jax.experimental.pallas import tpu as pltpu

F32 = jnp.float32
BF16 = jnp.bfloat16

D_MODEL = 1024
N_META = 16
BLK = 128
TQ = 3 * BLK
HEAD_DIM = 64
GROUP_WIDTH = 256
N_HEADS = 4
SWA_KV_HEADS = 2
DIFF_QK = 32
DECAY_LORA = 64
AAA_LORA = 64
GATE_LORA = 128
CONV_WIDTH = 31
CONV_HIST = 32
REL_BUCKETS = 32
REL_MAX_DIST = 128
N_EXPERTS = 16
N_GROUPS = 4
EXPERT_FF = 512
DEPTH = 2
ALPHA = (2 * DEPTH) ** 0.25
SWA_IN = 512
DIFF_IN = 768
RWKV_IN = 1024
CONV_IN = 512
IN_WIDTH = SWA_IN + DIFF_IN + RWKV_IN + CONV_IN
NEG = -1e30
LOG2E = math.log2(math.e)
CHUNK = 64
RWKV_STEPS = (11 * CHUNK, 6 * CHUNK)
MOE_TM = 256
MOE_PAIRS = 6
VMEM_LIMIT = 56 * 1024 * 1024


def _dot(a, b, prec=None):
    return jnp.dot(a, b, preferred_element_type=F32, precision=prec)


def _dot_nt(a, b, prec=None):
    return lax.dot_general(a, b, (((1,), (1,)), ((), ())), preferred_element_type=F32, precision=prec)


def _dot_tn(a, b, prec=None):
    return lax.dot_general(a, b, (((0,), (0,)), ((), ())), preferred_element_type=F32, precision=prec)


def _split2(x):
    hi = x.astype(BF16)
    return hi, (x - hi.astype(F32)).astype(BF16)


def _dot_exact_rhs(x, m_bf):
    hi, lo = _split2(x)
    return _dot(hi, m_bf) + _dot(lo, m_bf)


def _dot_exact_lhs(m_bf, x):
    hi, lo = _split2(x)
    return _dot(m_bf, hi) + _dot(m_bf, lo)


def _dot3(a, b):
    ah, al = _split2(a)
    bh, bl = _split2(b)
    return _dot(ah, bh) + _dot(ah, bl) + _dot(al, bh)


def _ln(x, g, b, eps=1e-5):
    mu = jnp.mean(x, -1, keepdims=True)
    xc = x - mu
    var = jnp.mean(xc * xc, -1, keepdims=True)
    return xc * lax.rsqrt(var + eps) * g + b


def _sigmoid(x):
    return 1.0 / (1.0 + jnp.exp(-x))


def _pick_tile(n, candidates):
    for c in candidates:
        if n % c == 0:
            return c
    raise ValueError(f"no tile in {candidates} divides {n}")


def _params(*sem):
    return pltpu.CompilerParams(dimension_semantics=sem, vmem_limit_bytes=VMEM_LIMIT)


def _t5_bucket(dist):
    n = jnp.maximum(dist, 0)
    max_exact = REL_BUCKETS // 2
    log_ratio = jnp.log(jnp.maximum(n, 1).astype(F32) / max_exact) / math.log(REL_MAX_DIST / max_exact)
    large = jnp.minimum(max_exact + (log_ratio * (REL_BUCKETS - max_exact)).astype(jnp.int32), REL_BUCKETS - 1)
    return jnp.where(n < max_exact, n, large)


def _gather_bias(bkt, rel_ref, h, offset=0.0):
    acc = jnp.zeros(bkt.shape, F32)
    for b in range(REL_BUCKETS):
        acc = jnp.where(bkt == b, rel_ref[b, h] - offset, acc)
    return acc


def _prep_kernel(xa_ref, xb_ref, meta_ref, g_ref, b_ref, o_ref, *, nbx):
    j = pl.program_id(1)
    head = jnp.where(j == 0, meta_ref[...], xa_ref[0])
    body = _ln(xb_ref[0, 0:BLK - N_META, :], g_ref[...], b_ref[...])
    o_ref[0, 0:N_META, :] = jnp.where(j <= nbx, _ln(head, g_ref[...], b_ref[...]), 0.0)
    o_ref[0, N_META:BLK, :] = jnp.where(j < nbx, body, 0.0)


def _prep(x, meta, g, b, lp):
    bsz, seq, d = x.shape
    nbx = seq // BLK
    per = BLK // N_META
    return pl.pallas_call(
        functools.partial(_prep_kernel, nbx=nbx),
        grid=(bsz, lp // BLK),
        in_specs=[
            pl.BlockSpec((1, N_META, d), lambda i, j: (i, jnp.clip(per * j - 1, 0, per * nbx - 1), 0)),
            pl.BlockSpec((1, BLK, d), lambda i, j: (i, jnp.minimum(j, nbx - 1), 0)),
            pl.BlockSpec((N_META, d), lambda i, j: (0, 0)),
            pl.BlockSpec((1, d), lambda i, j: (0, 0)),
            pl.BlockSpec((1, d), lambda i, j: (0, 0)),
        ],
        out_specs=pl.BlockSpec((1, BLK, d), lambda i, j: (i, j, 0)),
        out_shape=jax.ShapeDtypeStruct((bsz, lp, d), F32),
        compiler_params=_params("parallel", "parallel"),
        name="prep_ln",
    )(x, x, meta, g.reshape(1, d), b.reshape(1, d))


def _inproj_kernel(h_ref, w_ref, wvt_ref, ua_ref, qd_ref, kd_ref, vdt_ref, uc_ref, ud_ref):
    hb = h_ref[...].astype(BF16)

    def proj(c0, c1):
        return _dot(hb, w_ref[:, c0:c1])

    ua_ref[0, :, 0:256] = (proj(0, 256) * (HEAD_DIM ** -0.5)).astype(BF16)
    ua_ref[0, :, 256:512] = proj(256, 512).astype(BF16)
    qd_ref[0] = (proj(512, 768) * (DIFF_QK ** -0.5 * LOG2E)).astype(BF16)
    kd_ref[0] = proj(768, 1024).astype(BF16)
    vdt_ref[0, 0] = _dot_nt(wvt_ref[...], hb).astype(BF16)
    uc_ref[0] = proj(1280, 2304)
    ud_ref[0] = proj(2304, 2816)


def _inproj(h, w_bf, wvt_bf, bsz, lp):
    d = h.shape[1]
    tm = TQ
    nt = lp // tm
    row = lambda w, dt: jax.ShapeDtypeStruct((bsz, lp, w), dt)
    spec = lambda w: pl.BlockSpec((1, tm, w), lambda i, j: (i, j, 0))
    return pl.pallas_call(
        _inproj_kernel,
        grid=(bsz, nt),
        in_specs=[
            pl.BlockSpec((tm, d), lambda i, j: (i * nt + j, 0)),
            pl.BlockSpec((d, IN_WIDTH), lambda i, j: (0, 0)),
            pl.BlockSpec((GROUP_WIDTH, d), lambda i, j: (0, 0)),
        ],
        out_specs=[
            spec(512), spec(256), spec(256),
            pl.BlockSpec((1, 1, GROUP_WIDTH, tm), lambda i, j: (i, j, 0, 0)),
            spec(1024), spec(512),
        ],
        out_shape=[
            row(512, BF16), row(256, BF16), row(256, BF16),
            jax.ShapeDtypeStruct((bsz, nt, GROUP_WIDTH, tm), BF16),
            row(1024, F32), row(512, F32),
        ],
        compiler_params=_params("parallel", "parallel"),
        name="in_proj",
    )(h, w_bf, wvt_bf)


def _swa_kernel(rel_ref, sink_ref, bband_ref, bmeta_ref, cur_ref, prev_ref, meta_ref, o_ref, band_sc, metab_sc):
    j = pl.program_id(1)

    @pl.when(j == 0)
    def _():
        for h in range(N_HEADS):
            band_sc[h] = _gather_bias(bband_ref[...], rel_ref, h)

        for n in range(3):
            for h in range(N_HEADS):
                metab_sc[n, h] = _gather_bias(bmeta_ref[n], rel_ref, h)

    nsub = TQ // BLK
    r = lax.broadcasted_iota(jnp.int32, (BLK, 1), 0)
    dq = r - lax.broadcasted_iota(jnp.int32, (1, 2 * BLK), 1) + BLK
    in_window = (dq >= 0) & (dq < BLK)
    meta = meta_ref[0, 0:N_META, :]
    for u in range(nsub):
        cur = cur_ref[0, u * BLK:(u + 1) * BLK, :]
        prev = prev_ref[0] if u == 0 else cur_ref[0, (u - 1) * BLK:u * BLK, :]
        qpos = (j * nsub + u) * BLK + r
        ok_meta = qpos >= lax.broadcasted_iota(jnp.int32, (1, N_META), 1)
        ok_band = in_window & (qpos - dq >= N_META)
        mtab = jnp.where(j == 0, u, 2)
        outs = []
        for h in range(N_HEADS):
            g = h // (N_HEADS // SWA_KV_HEADS)
            kc, vc = 256 + HEAD_DIM * g, 384 + HEAD_DIM * g
            q = cur[:, HEAD_DIM * h:HEAD_DIM * (h + 1)]
            kcat = jnp.concatenate([prev[:, kc:kc + HEAD_DIM], cur[:, kc:kc + HEAD_DIM]], axis=0)
            vcat = jnp.concatenate([prev[:, vc:vc + HEAD_DIM], cur[:, vc:vc + HEAD_DIM]], axis=0)
            s_m = jnp.where(ok_meta, _dot_nt(q, meta[:, kc:kc + HEAD_DIM]) + metab_sc[mtab, h], NEG)
            s_b = jnp.where(ok_band, _dot_nt(q, kcat) + band_sc[h], NEG)
            sink = sink_ref[h]
            m = jnp.maximum(jnp.maximum(jnp.max(s_m, -1, keepdims=True), jnp.max(s_b, -1, keepdims=True)), sink)
            p_m, p_b = jnp.exp(s_m - m), jnp.exp(s_b - m)
            den = jnp.sum(p_m, -1, keepdims=True) + jnp.sum(p_b, -1, keepdims=True) + jnp.exp(sink - m)
            o = _dot(p_m.astype(BF16), meta[:, vc:vc + HEAD_DIM]) + _dot(p_b.astype(BF16), vcat)
            outs.append(o / den)
        o_ref[0, u * BLK:(u + 1) * BLK, :] = jnp.concatenate(outs, axis=1).astype(BF16)


def _swa(ua, rel_a, sinks, bkt_band, bkt_meta):
    bsz, lp, _ = ua.shape
    nsub = TQ // BLK
    smem = pl.BlockSpec(memory_space=pltpu.SMEM)
    return pl.pallas_call(
        _swa_kernel,
        grid=(bsz, lp // TQ),
        in_specs=[
            smem, smem,
            pl.BlockSpec((BLK, 2 * BLK), lambda i, j: (0, 0)),
            pl.BlockSpec((3, BLK, N_META), lambda i, j: (0, 0, 0)),
            pl.BlockSpec((1, TQ, SWA_IN), lambda i, j: (i, j, 0)),
            pl.BlockSpec((1, BLK, SWA_IN), lambda i, j: (i, jnp.maximum(j * nsub - 1, 0), 0)),
            pl.BlockSpec((1, BLK, SWA_IN), lambda i, j: (i, 0, 0)),
        ],
        out_specs=pl.BlockSpec((1, TQ, GROUP_WIDTH), lambda i, j: (i, j, 0)),
        out_shape=jax.ShapeDtypeStruct((bsz, lp, GROUP_WIDTH), BF16),
        scratch_shapes=[pltpu.VMEM((N_HEADS, BLK, 2 * BLK), F32), pltpu.VMEM((3, N_HEADS, BLK, N_META), F32)],
        compiler_params=_params("arbitrary", "arbitrary"),
        name="swa_attn",
    )(rel_a, sinks, bkt_band, bkt_meta, ua, ua, ua)


def _diff_kernel(lam_ref, rel_ref, bkt_ref, g_ref, q_ref, k_ref, vt_ref, o_ref, wt_sc, b1_sc, addm_sc, m_sc, acc_sc):
    i = pl.program_id(1)
    nsub = TQ // BLK

    @pl.when(i == 0)
    def _():
        kr = lax.broadcasted_iota(jnp.int32, (BLK, 1), 0)
        qc = lax.broadcasted_iota(jnp.int32, (1, BLK), 1)
        for h in range(N_HEADS):
            far = rel_ref[REL_BUCKETS - 1, h]
            b0 = jnp.where(qc >= kr, _gather_bias(bkt_ref[0], rel_ref, h, far) * LOG2E, NEG)
            b1 = _gather_bias(bkt_ref[1], rel_ref, h, far) * LOG2E
            b1_sc[h] = b1
            blocks = {0: b0, 1: b1, 2: jnp.zeros((BLK, BLK), F32)}
            masked = jnp.full((BLK, BLK), NEG, F32)
            for u in range(nsub):
                row = [blocks[w - u] if w >= u else masked for w in range(nsub)]
                addm_sc[h, u * BLK:(u + 1) * BLK, :] = jnp.concatenate(row + row, axis=1)

    q = q_ref[0]
    lane_grp = lax.broadcasted_iota(jnp.int32, (1, GROUP_WIDTH), 1) // DIFF_QK
    zero = jnp.zeros_like(q)
    for h in range(N_HEADS):
        wt_sc[h, 0:TQ, :] = jnp.where(lane_grp == 2 * h, q, zero)
        wt_sc[h, TQ:2 * TQ, :] = jnp.where(lane_grp == 2 * h + 1, q, zero)
    m_sc[...] = jnp.full(m_sc.shape, NEG, F32)
    acc_sc[...] = jnp.zeros(acc_sc.shape, F32)

    ones_rows = jnp.ones((2 * 8, TQ), BF16)

    def scores(t, h):
        return _dot_nt(k_ref[0, pl.ds(pl.multiple_of(t * TQ, TQ), TQ), :], wt_sc[h])

    def consume(t, h, s, kind):
        if kind == "diag":
            s = s + addm_sc[h]
        elif kind == "near":
            top, bot = s[0:TQ - BLK], s[TQ - BLK:TQ]
            b1 = b1_sc[h]
            bot = jnp.concatenate([bot[:, 0:BLK] + b1, bot[:, BLK:TQ], bot[:, TQ:TQ + BLK] + b1,
                                   bot[:, TQ + BLK:2 * TQ]], axis=1)
            s = jnp.concatenate([top, bot], axis=0)
        m_old = m_sc[h]
        cm = jnp.maximum(jnp.maximum(s[0:BLK], s[BLK:2 * BLK]), s[2 * BLK:3 * BLK])
        n = BLK
        while n > 8:
            n //= 2
            cm = jnp.maximum(cm[0:n], cm[n:2 * n])
        m_new = jnp.maximum(m_old, jnp.max(cm, 0, keepdims=True))
        a = jnp.exp2(m_old - m_new)
        p = jnp.exp2(s - m_new).astype(BF16)
        vth = vt_ref[0, t, HEAD_DIM * h:HEAD_DIM * (h + 1), :]
        pv = _dot(jnp.concatenate([vth, ones_rows], axis=0), p)
        acc_sc[h] = a * acc_sc[h] + pv
        m_sc[h] = m_new

    def tile(t, kind):
        for h in range(N_HEADS):
            consume(t, h, scores(t, h), kind)

    def far(t, c):
        tile(t, None)
        return c

    lax.fori_loop(0, i - 1, far, 0)

    @pl.when(i >= 1)
    def _():
        tile(i - 1, "near")

    tile(i, "diag")

    lam = lam_ref[0]
    post = lam_ref[1]
    outs = []
    for h in range(N_HEADS):
        acc = acc_sc[h]
        o = acc[0:HEAD_DIM] / acc[HEAD_DIM:HEAD_DIM + 1]
        o = o[:, 0:TQ] - lam * o[:, TQ:2 * TQ]
        ms = jnp.mean(o * o, 0, keepdims=True)
        outs.append(o * lax.rsqrt(ms + 1e-5) * g_ref[...] * post)
    ot = jnp.concatenate(outs, axis=0).astype(BF16)
    eye = (lax.broadcasted_iota(jnp.int32, (TQ, TQ), 0) == lax.broadcasted_iota(jnp.int32, (TQ, TQ), 1))
    o_ref[0] = _dot_nt(eye.astype(BF16), ot).astype(BF16)


def _diff(qd, kd, vdt, lam2, rel_b, bkt_d, g_b):
    bsz, lp, _ = qd.shape
    nq = lp // TQ
    smem = pl.BlockSpec(memory_space=pltpu.SMEM)
    return pl.pallas_call(
        _diff_kernel,
        grid=(bsz, nq),
        in_specs=[
            smem, smem,
            pl.BlockSpec((2, BLK, BLK), lambda b, i: (0, 0, 0)),
            pl.BlockSpec((HEAD_DIM, 1), lambda b, i: (0, 0)),
            pl.BlockSpec((1, TQ, GROUP_WIDTH), lambda b, i: (b, i, 0)),
            pl.BlockSpec((1, lp, GROUP_WIDTH), lambda b, i: (b, 0, 0)),
            pl.BlockSpec((1, nq, GROUP_WIDTH, TQ), lambda b, i: (b, 0, 0, 0)),
        ],
        out_specs=pl.BlockSpec((1, TQ, GROUP_WIDTH), lambda b, i: (b, i, 0)),
        out_shape=jax.ShapeDtypeStruct((bsz, lp, GROUP_WIDTH), BF16),
        scratch_shapes=[
            pltpu.VMEM((N_HEADS, 2 * TQ, GROUP_WIDTH), BF16),
            pltpu.VMEM((N_HEADS, BLK, BLK), F32),
            pltpu.VMEM((N_HEADS, TQ, 2 * TQ), F32),
            pltpu.VMEM((N_HEADS, 1, 2 * TQ), F32),
            pltpu.VMEM((N_HEADS, HEAD_DIM + 2 * 8, 2 * TQ), F32),
        ],
        compiler_params=_params("arbitrary", "arbitrary"),
        name="diff_attn",
    )(lam2, rel_b, bkt_d, g_b, qd, kd, vdt)


def _rwkv_kernel(x_ref, xp_ref, mu_ref, w0_ref, w2_ref, a0_ref, a2_ref, g2_ref, kk_ref, ka_ref, rk_ref,
                 lg_ref, lb_ref, o_ref, zt_sc):
    c = pl.program_id(1)
    n = CHUNK
    rstep = x_ref.shape[1]
    ns = rstep // CHUNK
    gw = GROUP_WIDTH

    @pl.when(c == 0)
    def _():
        zt_sc[...] = jnp.zeros(zt_sc.shape, F32)

    x = x_ref[0]
    row = lax.broadcasted_iota(jnp.int32, (rstep, 1), 0)
    last_prev = jnp.where(c == 0, 0.0, xp_ref[0, 7:8, :])
    xs = jnp.where(row == 0, last_prev, pltpu.roll(x, 1, 0))
    xm = x + mu_ref[...] * (xs - x)
    r, k, v = xm[:, 0:gw], xm[:, gw:2 * gw], xm[:, 2 * gw:3 * gw]
    xw = xm[:, 768:768 + DECAY_LORA]
    xa = xm[:, 832:832 + AAA_LORA]
    xg = xm[:, 896:896 + GATE_LORA]
    logw = -math.exp(-0.5) * _sigmoid(w0_ref[...] + _dot3(jnp.tanh(xw), w2_ref[...]))
    a = _sigmoid(a0_ref[...] + _dot3(xa, a2_ref[...]))
    g = _dot3(_sigmoid(xg), g2_ref[...])

    lane_head = lax.broadcasted_iota(jnp.int32, (1, gw), 1) // HEAD_DIM
    sub_head = lax.broadcasted_iota(jnp.int32, (gw, 1), 0) // HEAD_DIM
    head_ones = (sub_head == lane_head).astype(BF16)

    kk = k * kk_ref[...]
    kk = kk / jnp.maximum(jnp.sqrt(_dot_exact_rhs(kk * kk, head_ones)), 1e-12)
    k2 = k * (1.0 + (a - 1.0) * ka_ref[...])
    bvec = kk * a

    ri = lax.broadcasted_iota(jnp.int32, (rstep, rstep), 0)
    ci = lax.broadcasted_iota(jnp.int32, (rstep, rstep), 1)
    tri = (((ri // n) == (ci // n)) & (ri >= ci)).astype(BF16)
    cum = _dot_exact_lhs(tri, logw)
    tot = jnp.concatenate([jnp.broadcast_to(cum[(s + 1) * n - 1:(s + 1) * n, :], (n, gw)) for s in range(ns)], axis=0)
    e_neg = jnp.exp(-cum)
    e_last = jnp.exp(tot - cum)
    a_t = (-kk * jnp.exp(cum - logw)).astype(BF16)
    r_t = (r * jnp.exp(cum)).astype(BF16)
    b_t, k_t = (bvec * e_neg).astype(BF16), (k2 * e_neg).astype(BF16)
    b_h, k_h = (bvec * e_last).astype(BF16), (k2 * e_last).astype(BF16)
    g_c = jnp.exp(tot)
    vb = v.astype(BF16)

    def stack(t, s):
        ts = t[s * n:(s + 1) * n]
        return jnp.concatenate([jnp.where(lane_head == h, ts, jnp.zeros_like(ts)) for h in range(N_HEADS)], axis=0)

    rr = lax.broadcasted_iota(jnp.int32, (gw, gw), 0)
    cc = lax.broadcasted_iota(jnp.int32, (gw, gw), 1)
    parts = []
    for s in range(ns):
        ast, rst, bst, kst = stack(a_t, s), stack(r_t, s), stack(b_t, s), stack(k_t, s)
        gmat = _dot_nt(jnp.concatenate([ast, rst], axis=0), jnp.concatenate([bst, kst], axis=0))
        low = jnp.where(rr > cc, gmat[0:gw, 0:gw], 0.0)
        aak = jnp.where(rr > cc, gmat[0:gw, gw:2 * gw], 0.0).astype(BF16)
        arb = jnp.where(rr >= cc, gmat[gw:2 * gw, 0:gw], 0.0).astype(BF16)
        ark = jnp.where(rr >= cc, gmat[gw:2 * gw, gw:2 * gw], 0.0).astype(BF16)
        pw = low
        tinv = jnp.where(rr == cc, 1.0, low)
        for _ in range(5):
            pwb = pw.astype(BF16)
            pw = _dot(pwb, pwb)
            tinv = tinv + _dot(tinv.astype(BF16), pw.astype(BF16))
        parts.append((ast, rst, aak, arb, ark, tinv.astype(BF16), stack(b_h, s), stack(k_h, s), stack(vb, s)))

    zt = zt_sc[...]
    ys = []
    for s in range(ns):
        ast, rst, aak, arb, ark, tinv, bhs, khs, vst = parts[s]
        ztb = zt.astype(BF16)
        u = _dot(tinv, (_dot_nt(ast, ztb) + _dot(aak, vst)).astype(BF16)).astype(BF16)
        ybd = _dot_nt(rst, ztb) + _dot(arb, u) + _dot(ark, vst)
        zt = zt * g_c[s * n:s * n + 1, :] + _dot_tn(u, bhs) + _dot_tn(vst, khs)
        ys.append(ybd[0:n] + ybd[n:2 * n] + ybd[2 * n:3 * n] + ybd[3 * n:4 * n])
    zt_sc[...] = zt
    y = jnp.concatenate(ys, axis=0)

    mean = _dot_exact_rhs(y, head_ones) * (1.0 / HEAD_DIM)
    yc = y - mean
    var = _dot_exact_rhs(yc * yc, head_ones) * (1.0 / HEAD_DIM)
    yn = yc * lax.rsqrt(var + 64e-5) * lg_ref[...] + lb_ref[...]
    bonus = _dot_exact_rhs(r * k2 * rk_ref[...], head_ones) * v
    o_ref[0] = ((yn + bonus) * g).astype(BF16)


def _rwkv(uc, mu, w0, w2, a0, a2, g2, k_k, k_a, r_k, lnx_g, lnx_b):
    bsz, lp, w = uc.shape
    rstep = _pick_tile(lp, RWKV_STEPS)
    nc = lp // rstep
    vec = lambda t: t.reshape(1, -1)
    full = lambda t: pl.BlockSpec(t.shape, lambda b, c: (0,) * t.ndim)
    args = [vec(mu), vec(w0), w2, vec(a0), a2, g2, vec(k_k), vec(k_a), vec(r_k), vec(lnx_g), vec(lnx_b)]
    return pl.pallas_call(
        _rwkv_kernel,
        grid=(bsz, nc),
        in_specs=[
            pl.BlockSpec((1, rstep, w), lambda b, c: (b, c, 0)),
            pl.BlockSpec((1, 8, w), lambda b, c: (b, jnp.maximum(c * (rstep // 8) - 1, 0), 0)),
        ] + [full(t) for t in args],
        out_specs=pl.BlockSpec((1, rstep, GROUP_WIDTH), lambda b, c: (b, c, 0)),
        out_shape=jax.ShapeDtypeStruct((bsz, lp, GROUP_WIDTH), BF16),
        scratch_shapes=[pltpu.VMEM((GROUP_WIDTH, GROUP_WIDTH), F32)],
        compiler_params=_params("arbitrary", "arbitrary"),
        name="rwkv7",
    )(uc, uc, *args)


def _conv_kernel(cur_ref, prev_ref, w_ref, b_ref, g_ref, gb_ref, o_ref, hcat_sc):
    j = pl.program_id(1)
    ch = GROUP_WIDTH

    def glu(t):
        return t[:, 0:ch] * _sigmoid(t[:, ch:2 * ch])

    hcat_sc[0:CONV_HIST, :] = jnp.where(j == 0, 0.0, glu(prev_ref[0]))
    hcat_sc[CONV_HIST:CONV_HIST + TQ, :] = glu(cur_ref[0])
    lead = CONV_HIST - (CONV_WIDTH - 1)
    acc = jnp.zeros((TQ, ch), F32) + b_ref[...]
    for ph in range(8):
        offs = [o for o in range(lead, CONV_HIST + 1) if o % 8 == ph]
        shifted = hcat_sc[ph:max(offs) + TQ, :]
        for o in offs:
            acc = acc + shifted[o - ph:o - ph + TQ] * w_ref[o - lead:o - lead + 1, :]
    lane_grp = lax.broadcasted_iota(jnp.int32, (1, ch), 1) // HEAD_DIM
    sub_grp = lax.broadcasted_iota(jnp.int32, (ch, 1), 0) // HEAD_DIM
    grp_ones = (sub_grp == lane_grp).astype(BF16)
    mean = _dot_exact_rhs(acc, grp_ones) * (1.0 / HEAD_DIM)
    xc = acc - mean
    var = _dot_exact_rhs(xc * xc, grp_ones) * (1.0 / HEAD_DIM)
    y = xc * lax.rsqrt(var + 1e-5) * g_ref[...] + gb_ref[...]
    o_ref[0] = (y * _sigmoid(y)).astype(BF16)


def _conv(ud, conv_w, conv_b, gn_g, gn_b):
    bsz, lp, w = ud.shape
    nb = lp // TQ
    vec = lambda t: t.reshape(1, -1)
    const = lambda shape: pl.BlockSpec(shape, lambda i, j: (0, 0))
    return pl.pallas_call(
        _conv_kernel,
        grid=(bsz, nb),
        in_specs=[
            pl.BlockSpec((1, TQ, w), lambda i, j: (i, j, 0)),
            pl.BlockSpec((1, CONV_HIST, w), lambda i, j: (i, jnp.maximum(j * (TQ // CONV_HIST) - 1, 0), 0)),
            const((CONV_WIDTH, GROUP_WIDTH)), const((1, GROUP_WIDTH)), const((1, GROUP_WIDTH)),
            const((1, GROUP_WIDTH)),
        ],
        out_specs=pl.BlockSpec((1, TQ, GROUP_WIDTH), lambda i, j: (i, j, 0)),
        out_shape=jax.ShapeDtypeStruct((bsz, lp, GROUP_WIDTH), BF16),
        scratch_shapes=[pltpu.VMEM((CONV_HIST + TQ, GROUP_WIDTH), F32)],
        compiler_params=_params("parallel", "arbitrary"),
        name="conv_module",
    )(ud, ud, conv_w, vec(conv_b), vec(gn_g), vec(gn_b))


def _router(logit_t, rb_ref):
    s = [_sigmoid(logit_t[e:e + 1, :]) for e in range(N_EXPERTS)]
    bz = [s[e] + rb_ref[e:e + 1, 0:1] for e in range(N_EXPERTS)]
    per = N_EXPERTS // N_GROUPS
    gsum = []
    for gi in range(N_GROUPS):
        a, b, c, d = bz[per * gi:per * gi + per]
        hi1, lo1, hi2, lo2 = jnp.maximum(a, b), jnp.minimum(a, b), jnp.maximum(c, d), jnp.minimum(c, d)
        gsum.append(jnp.maximum(hi1, hi2) + jnp.maximum(jnp.minimum(hi1, hi2), jnp.maximum(lo1, lo2)))
    best = jnp.zeros_like(gsum[0], dtype=jnp.int32)
    bval = gsum[0]
    for gi in range(1, N_GROUPS):
        take = gsum[gi] > bval
        best = jnp.where(take, gi, best)
        bval = jnp.where(take, gsum[gi], bval)

    def pick(vals, i):
        out = vals[i]
        for gi in range(1, N_GROUPS):
            out = jnp.where(best == gi, vals[per * gi + i], out)
        return out

    bv = [pick(bz, i) for i in range(per)]
    sv = [pick(s, i) for i in range(per)]
    i1 = jnp.zeros_like(best)
    v1 = bv[0]
    for i in range(1, per):
        take = bv[i] > v1
        i1 = jnp.where(take, i, i1)
        v1 = jnp.where(take, bv[i], v1)
    i2 = jnp.full_like(best, -1)
    v2 = jnp.full_like(v1, -jnp.inf)
    for i in range(per):
        take = (i1 != i) & (bv[i] > v2)
        i2 = jnp.where(take, i, i2)
        v2 = jnp.where(take, bv[i], v2)
    s1 = sv[0]
    s2 = sv[0]
    for i in range(1, per):
        s1 = jnp.where(i1 == i, sv[i], s1)
        s2 = jnp.where(i2 == i, sv[i], s2)
    tot = s1 + s2
    g1, g2 = s1 / tot, s2 / tot
    rows = []
    for e in range(N_EXPERTS):
        gi, i = divmod(e, per)
        hit = jnp.where(best == gi, jnp.where(i1 == i, g1, jnp.where(i2 == i, g2, 0.0)), 0.0)
        rows.append(hit)
    return jnp.concatenate(rows, axis=0)


def _outproj_kernel(ya_ref, yb_ref, yc_ref, yd_ref, w_ref, h_ref, g_ref, b_ref, rwt_ref, rb_ref, o_ref, comb_ref):
    gw = GROUP_WIDTH
    mix = (_dot(ya_ref[0], w_ref[0:gw, :]) + _dot(yb_ref[0], w_ref[gw:2 * gw, :])
           + _dot(yc_ref[0], w_ref[2 * gw:3 * gw, :]) + _dot(yd_ref[0], w_ref[3 * gw:4 * gw, :]))
    h1 = _ln(ALPHA * h_ref[...] + mix, g_ref[...], b_ref[...])
    o_ref[...] = h1
    wh, wl = _split2(rwt_ref[...])
    hh, hl = _split2(h1)
    logit_t = _dot_nt(wh, hh) + _dot_nt(wh, hl) + _dot_nt(wl, hh)
    comb_ref[...] = _router(logit_t, rb_ref)


def _outproj(ya, yb, yc, yd, w_bf, h, g, b, rwt, rb_b):
    bsz, lp, _ = ya.shape
    d = h.shape[1]
    tm = TQ
    nt = lp // tm
    spec = lambda w: pl.BlockSpec((1, tm, w), lambda i, j: (i, j, 0))
    flat = lambda w: pl.BlockSpec((tm, w), lambda i, j: (i * nt + j, 0))
    const = lambda shape: pl.BlockSpec(shape, lambda i, j: (0, 0))
    return pl.pallas_call(
        _outproj_kernel,
        grid=(bsz, nt),
        in_specs=[spec(GROUP_WIDTH)] * 4 + [const((d, d)), flat(d), const((1, d)), const((1, d)),
                                            const((N_EXPERTS, d)), const((N_EXPERTS, BLK))],
        out_specs=[flat(d), pl.BlockSpec((N_EXPERTS, tm), lambda i, j: (0, i * nt + j))],
        out_shape=[jax.ShapeDtypeStruct((bsz * lp, d), F32), jax.ShapeDtypeStruct((N_EXPERTS, bsz * lp), F32)],
        compiler_params=_params("parallel", "parallel"),
        name="out_proj_ln_router",
    )(ya, yb, yc, yd, w_bf, h, g.reshape(1, d), b.reshape(1, d), rwt, rb_b)


def _route_meta(comb, tm, lp, nreal):
    t = comb.shape[1]
    per = N_EXPERTS // N_GROUPS
    gsum = comb.reshape(N_GROUPS, per, t).sum(1)
    grp = jnp.argmax(gsum, axis=0).astype(jnp.int32)
    c4 = jnp.sum(jnp.where(jnp.arange(N_GROUPS)[:, None, None] == grp[None, None, :],
                           comb.reshape(N_GROUPS, per, t), 0.0), axis=0)
    mem = jnp.arange(per, dtype=jnp.int32)[:, None]
    lo = jnp.min(jnp.where(c4 > 0, mem, per), axis=0)
    hi = jnp.max(jnp.where(c4 > 0, mem, -1), axis=0)
    lo, hi = jnp.where(hi < 0, 0, lo), jnp.where(hi < 0, per - 1, hi)
    one = lo == hi
    lo, hi = jnp.where(one & (lo == per - 1), per - 2, lo), jnp.where(one, jnp.minimum(lo + 1, per - 1), hi)
    w_lo = jnp.sum(jnp.where(mem == lo[None, :], c4, 0.0), axis=0)
    w_hi = jnp.sum(jnp.where(mem == hi[None, :], c4, 0.0), axis=0)
    pair_base = lambda a: a * (2 * per - 1 - a) // 2
    ncls = N_GROUPS * MOE_PAIRS
    cls = jnp.where(jnp.arange(t) % lp < nreal, grp * MOE_PAIRS + pair_base(lo) + hi - lo - 1, ncls)
    order = jnp.argsort(cls, stable=True).astype(jnp.int32)
    counts = jnp.sum((cls[:, None] == jnp.arange(ncls)[None, :]).astype(jnp.int32), axis=0)
    start = jnp.cumsum(counts) - counts
    pcounts = (counts + tm - 1) // tm * tm
    pend = jnp.cumsum(pcounts)
    nt = pl.cdiv(t // lp * nreal, tm) + ncls
    s = jnp.arange(nt * tm, dtype=jnp.int32)
    past = s[:, None] >= pend[None, :]
    gs = jnp.minimum(jnp.sum(past.astype(jnp.int32), axis=1), ncls - 1)
    skipped = jnp.sum(jnp.where(past, (pcounts - counts)[None, :], 0), axis=1)
    in_cls = s - jnp.sum(jnp.where(past, pcounts[None, :], 0), axis=1)
    valid = in_cls < jnp.sum(jnp.where(gs[:, None] == jnp.arange(ncls)[None, :], counts[None, :], 0), axis=1)
    valid = valid & (s < pend[ncls - 1])
    tok = order[jnp.clip(s - skipped, 0, t - 1)]
    src = jnp.where(valid, tok, -1)
    c2 = jnp.where(valid[:, None], jnp.stack([w_lo[tok], w_hi[tok]], axis=-1), 0.0)
    n_tiles = (pend[-1] // tm).astype(jnp.int32)
    tile_first = jnp.minimum(jnp.arange(nt, dtype=jnp.int32), n_tiles - 1) * tm
    tcls = jnp.minimum(jnp.sum((tile_first[:, None] >= pend[None, :]).astype(jnp.int32), axis=1), ncls - 1)
    pid = tcls % MOE_PAIRS
    p_lo = (pid >= pair_base(1)).astype(jnp.int32) + (pid >= pair_base(2)).astype(jnp.int32)
    p_hi = pid - pair_base(p_lo) + p_lo + 1
    e_lo = (tcls // MOE_PAIRS * per + p_lo).astype(jnp.int32)
    e_hi = (tcls // MOE_PAIRS * per + p_hi).astype(jnp.int32)
    return src, e_lo, e_hi, n_tiles.reshape(1), c2, nt


def _moe_kernel(src_ref, elo_ref, ehi_ref, nt_ref, h_hbm, c_ref, w1a_ref, w3a_ref, w2a_ref, w1b_ref, w3b_ref, w2b_ref,
                g_ref, b_ref, o_hbm, xbuf, obuf, gsem, ssem, *, tm, bsz, lp, nreal):
    i = pl.program_id(0)
    nt = nt_ref[0]
    slot = lax.rem(i, 2)
    npad = lp - nreal

    def pad_row(q):
        return (q // npad) * lp + nreal + q % npad

    def gather(tile, sl):
        base = tile * tm
        for r in range(tm):
            tok = jnp.maximum(src_ref[base + r], 0)
            pltpu.make_async_copy(h_hbm.at[pl.ds(tok, 1)], xbuf.at[sl, pl.ds(r, 1)], gsem.at[sl]).start()

    def scatter(sl):
        base = i * tm
        for r in range(tm):
            s = src_ref[base + r]
            dst = jnp.where(s >= 0, s, pad_row(sl * tm + r))
            pltpu.make_async_copy(obuf.at[sl, pl.ds(r, 1)], o_hbm.at[pl.ds(dst, 1)], ssem.at[sl]).start()

    def wait_gather(sl):
        pltpu.make_async_copy(h_hbm.at[pl.ds(0, tm)], xbuf.at[sl], gsem.at[sl]).wait()

    def wait_scatter(sl):
        pltpu.make_async_copy(obuf.at[sl], o_hbm.at[pl.ds(0, tm)], ssem.at[sl]).wait()

    @pl.when(i == 0)
    def _():
        gather(0, 0)
        obuf[0, 0:npad, :] = jnp.zeros((npad, obuf.shape[2]), F32)
        for bi in range(bsz):
            fill = pltpu.make_async_copy(obuf.at[0, pl.ds(0, npad)], o_hbm.at[pl.ds(bi * lp + nreal, npad)],
                                         ssem.at[0])
            fill.start()
            fill.wait()

    for sl in range(2):
        @pl.when((i + 1 < nt) & (slot == 1 - sl))
        def _(sl=sl):
            gather(i + 1, sl)

    @pl.when(i < nt)
    def _():
        wait_gather(slot)

        @pl.when(i >= 2)
        def _():
            wait_scatter(slot)

        x = xbuf[slot]
        xb = x.astype(BF16)
        c = c_ref[...]
        y = jnp.zeros(x.shape, F32)
        for e, (w1_ref, w3_ref, w2_ref) in enumerate(((w1a_ref, w3a_ref, w2a_ref), (w1b_ref, w3b_ref, w2b_ref))):
            a1 = _dot(xb, w1_ref[0])
            act = (a1 * _sigmoid(a1)) * _dot(xb, w3_ref[0]) * c[:, e:e + 1]
            y = y + _dot(act.astype(BF16), w2_ref[0])
        obuf[slot] = _ln(ALPHA * x + y, g_ref[...], b_ref[...])

        for sl in range(2):
            @pl.when(slot == sl)
            def _(sl=sl):
                scatter(sl)

    @pl.when(i == nt - 1)
    def _():
        wait_scatter(slot)

        @pl.when(nt >= 2)
        def _():
            wait_scatter(1 - slot)


def _moe(h1, comb, w1, w3, w2, g, b, bsz, lp, nreal):
    t, d = h1.shape
    tm = MOE_TM
    assert bsz * (lp - nreal) >= 2 * tm and lp - nreal <= tm
    src, e_lo, e_hi, n_tiles, c2, nt = _route_meta(comb, tm, lp, nreal)
    const = lambda i, src, elo, ehi, ntl: (0, 0)
    wa = lambda a, b_: pl.BlockSpec((1, a, b_), lambda i, src, elo, ehi, ntl: (elo[i], 0, 0))
    wb = lambda a, b_: pl.BlockSpec((1, a, b_), lambda i, src, elo, ehi, ntl: (ehi[i], 0, 0))
    grid_spec = pltpu.PrefetchScalarGridSpec(
        num_scalar_prefetch=4,
        grid=(nt,),
        in_specs=[
            pl.BlockSpec(memory_space=pl.ANY),
            pl.BlockSpec((tm, 2), lambda i, src, elo, ehi, ntl: (i, 0)),
            wa(d, EXPERT_FF), wa(d, EXPERT_FF), wa(EXPERT_FF, d),
            wb(d, EXPERT_FF), wb(d, EXPERT_FF), wb(EXPERT_FF, d),
            pl.BlockSpec((1, d), const),
            pl.BlockSpec((1, d), const),
        ],
        out_specs=pl.BlockSpec(memory_space=pl.ANY),
        scratch_shapes=[
            pltpu.VMEM((2, tm, d), F32), pltpu.VMEM((2, tm, d), F32),
            pltpu.SemaphoreType.DMA((2,)), pltpu.SemaphoreType.DMA((2,)),
        ],
    )
    return pl.pallas_call(
        functools.partial(_moe_kernel, tm=tm, bsz=bsz, lp=lp, nreal=nreal),
        grid_spec=grid_spec,
        out_shape=jax.ShapeDtypeStruct((t, d), F32),
        compiler_params=_params("arbitrary"),
        name="moe_ln",
    )(src, e_lo, e_hi, n_tiles, h1, c2, w1, w3, w2, w1, w3, w2, g.reshape(1, d), b.reshape(1, d))


def kernel(x, meta, ln0_g, ln0_b, w_in, swa_sinks, rel_bias, diff_lq1, diff_lk1, diff_lq2, diff_lk2, diff_subln_g,
           rwkv_mu, rwkv_w0, rwkv_w2, rwkv_a0, rwkv_a2, rwkv_g2, rwkv_kk, rwkv_ka, rwkv_rk, rwkv_lnx_g,
           rwkv_lnx_b, conv_w, conv_b, conv_gn_g, conv_gn_b, w_out, ln1_g, ln1_b, router_w, router_b, exp_w1,
           exp_w3, exp_w2, ln2_g, ln2_b):
    bsz, seq, d = x.shape
    assert d == D_MODEL and seq % BLK == 0
    lp = pl.cdiv(seq + N_META, TQ) * TQ

    r = jnp.arange(BLK)
    bkt_band = _t5_bucket(r[:, None] - jnp.arange(2 * BLK)[None, :] + BLK)
    qpos3 = jnp.arange(3 * BLK).reshape(3, BLK)
    bkt_meta = _t5_bucket(qpos3[:, :, None] - jnp.arange(N_META)[None, None, :])
    dq0 = r[None, :] - r[:, None]
    bkt_d = jnp.stack([_t5_bucket(dq0), _t5_bucket(dq0 + BLK)])
    rel_a, rel_b = rel_bias[:, :N_HEADS], rel_bias[:, N_HEADS:]
    rwt = router_w.T
    rb_b = jnp.broadcast_to(router_b.astype(F32)[:, None], (N_EXPERTS, BLK))
    vcols = slice(SWA_IN + 512, SWA_IN + 768)

    h = _prep(x, meta, ln0_g, ln0_b, lp).reshape(bsz * lp, d)
    for l in range(DEPTH):
        w_l = w_in[l]
        ua, qd, kd, vdt, uc, ud = _inproj(h, w_l.astype(BF16), w_l[:, vcols].T.astype(BF16), bsz, lp)
        ya = _swa(ua, rel_a, swa_sinks[l], bkt_band, bkt_meta)
        lam_init = 0.8 - 0.6 * math.exp(-0.3 * l)
        lam = (jnp.exp(jnp.sum(diff_lq1[l] * diff_lk1[l])) - jnp.exp(jnp.sum(diff_lq2[l] * diff_lk2[l])) + lam_init)
        lam2 = jnp.stack([lam, jnp.asarray(1.0 - lam_init, F32)]).astype(F32)
        yb = _diff(qd, kd, vdt, lam2, rel_b, bkt_d, diff_subln_g[l].reshape(HEAD_DIM, 1))
        yc = _rwkv(uc, rwkv_mu[l], rwkv_w0[l], rwkv_w2[l], rwkv_a0[l], rwkv_a2[l], rwkv_g2[l], rwkv_kk[l],
                   rwkv_ka[l], rwkv_rk[l], rwkv_lnx_g[l], rwkv_lnx_b[l])
        yd = _conv(ud, conv_w[l], conv_b[l], conv_gn_g[l], conv_gn_b[l])
        h1, comb = _outproj(ya, yb, yc, yd, w_out[l].astype(BF16), h, ln1_g[l], ln1_b[l], rwt, rb_b)
        h = _moe(h1, comb, exp_w1[l].astype(BF16), exp_w3[l].astype(BF16), exp_w2[l].astype(BF16),
                 ln2_g[l], ln2_b[l], bsz, lp, N_META + seq)
    return h.reshape(bsz, lp, d)[:, N_META:N_META + seq]
```

```python
import functools
import math

import jax
import jax.numpy as jnp
import numpy as np
from jax import lax
from jax.experimental import pallas as pl
from jax.experimental.pallas import tpu as pltpu

F32 = jnp.float32
BF16 = jnp.bfloat16

D_MODEL = 1024
N_META = 16
BLK = 128
TQ = 3 * BLK
HEAD_DIM = 64
GROUP_WIDTH = 256
N_HEADS = 4
SWA_KV_HEADS = 2
DIFF_QK = 32
DECAY_LORA = 64
AAA_LORA = 64
GATE_LORA = 128
CONV_WIDTH = 31
CONV_HIST = 32
REL_BUCKETS = 32
REL_MAX_DIST = 128
N_EXPERTS = 16
N_GROUPS = 4
EXPERT_FF = 512
DEPTH = 2
ALPHA = (2 * DEPTH) ** 0.25
SWA_IN = 512
DIFF_IN = 768
RWKV_IN = 1024
CONV_IN = 512
IN_WIDTH = SWA_IN + DIFF_IN + RWKV_IN + CONV_IN
NEG = -1e30
LOG2E = math.log2(math.e)
CHUNK = 64
RWKV_STEPS = (11 * CHUNK, 6 * CHUNK)
MOE_TM = 256
MOE_PAIRS = 6
VMEM_LIMIT = 56 * 1024 * 1024


def _dot(a, b, prec=None):
    return jnp.dot(a, b, preferred_element_type=F32, precision=prec)


def _dot_nt(a, b, prec=None):
    return lax.dot_general(a, b, (((1,), (1,)), ((), ())), preferred_element_type=F32, precision=prec)


def _dot_tn(a, b, prec=None):
    return lax.dot_general(a, b, (((0,), (0,)), ((), ())), preferred_element_type=F32, precision=prec)


def _split2(x):
    hi = x.astype(BF16)
    return hi, (x - hi.astype(F32)).astype(BF16)


def _dot_exact_rhs(x, m_bf):
    hi, lo = _split2(x)
    return _dot(hi, m_bf) + _dot(lo, m_bf)


def _dot_exact_lhs(m_bf, x):
    hi, lo = _split2(x)
    return _dot(m_bf, hi) + _dot(m_bf, lo)


def _dot3(a, b):
    ah, al = _split2(a)
    bh, bl = _split2(b)
    return _dot(ah, bh) + _dot(ah, bl) + _dot(al, bh)


def _ln(x, g, b, eps=1e-5):
    mu = jnp.mean(x, -1, keepdims=True)
    xc = x - mu
    var = jnp.mean(xc * xc, -1, keepdims=True)
    return xc * lax.rsqrt(var + eps) * g + b


def _sigmoid(x):
    return 1.0 / (1.0 + jnp.exp(-x))


def _pick_tile(n, candidates):
    for c in candidates:
        if n % c == 0:
            return c
    raise ValueError(f"no tile in {candidates} divides {n}")


def _params(*sem):
    return pltpu.CompilerParams(dimension_semantics=sem, vmem_limit_bytes=VMEM_LIMIT)


def _t5_bucket(dist):
    n = jnp.maximum(dist, 0)
    max_exact = REL_BUCKETS // 2
    log_ratio = jnp.log(jnp.maximum(n, 1).astype(F32) / max_exact) / math.log(REL_MAX_DIST / max_exact)
    large = jnp.minimum(max_exact + (log_ratio * (REL_BUCKETS - max_exact)).astype(jnp.int32), REL_BUCKETS - 1)
    return jnp.where(n < max_exact, n, large)


def _gather_bias(bkt, rel_ref, h, offset=0.0):
    acc = jnp.zeros(bkt.shape, F32)
    for b in range(REL_BUCKETS):
        acc = jnp.where(bkt == b, rel_ref[b, h] - offset, acc)
    return acc


def _prep_kernel(xa_ref, xb_ref, meta_ref, g_ref, b_ref, o_ref, *, nbx):
    j = pl.program_id(1)
    head = jnp.where(j == 0, meta_ref[...], xa_ref[0])
    body = _ln(xb_ref[0, 0:BLK - N_META, :], g_ref[...], b_ref[...])
    o_ref[0, 0:N_META, :] = jnp.where(j <= nbx, _ln(head, g_ref[...], b_ref[...]), 0.0)
    o_ref[0, N_META:BLK, :] = jnp.where(j < nbx, body, 0.0)


def _prep(x, meta, g, b, lp):
    bsz, seq, d = x.shape
    nbx = seq // BLK
    per = BLK // N_META
    return pl.pallas_call(
        functools.partial(_prep_kernel, nbx=nbx),
        grid=(bsz, lp // BLK),
        in_specs=[
            pl.BlockSpec((1, N_META, d), lambda i, j: (i, jnp.clip(per * j - 1, 0, per * nbx - 1), 0)),
            pl.BlockSpec((1, BLK, d), lambda i, j: (i, jnp.minimum(j, nbx - 1), 0)),
            pl.BlockSpec((N_META, d), lambda i, j: (0, 0)),
            pl.BlockSpec((1, d), lambda i, j: (0, 0)),
            pl.BlockSpec((1, d), lambda i, j: (0, 0)),
        ],
        out_specs=pl.BlockSpec((1, BLK, d), lambda i, j: (i, j, 0)),
        out_shape=jax.ShapeDtypeStruct((bsz, lp, d), F32),
        compiler_params=_params("parallel", "parallel"),
        name="prep_ln",
    )(x, x, meta, g.reshape(1, d), b.reshape(1, d))


def _inproj_kernel(h_ref, w_ref, wvt_ref, ua_ref, qd_ref, kd_ref, vdt_ref, uc_ref, ud_ref):
    hb = h_ref[...].astype(BF16)

    def proj(c0, c1):
        return _dot(hb, w_ref[:, c0:c1])

    ua_ref[0, :, 0:256] = (proj(0, 256) * (HEAD_DIM ** -0.5)).astype(BF16)
    ua_ref[0, :, 256:512] = proj(256, 512).astype(BF16)
    qd_ref[0] = (proj(512, 768) * (DIFF_QK ** -0.5 * LOG2E)).astype(BF16)
    kd_ref[0] = proj(768, 1024).astype(BF16)
    vdt_ref[0, 0] = _dot_nt(wvt_ref[...], hb).astype(BF16)
    uc_ref[0] = proj(1280, 2304)
    ud_ref[0] = proj(2304, 2816)


def _inproj(h, w_bf, wvt_bf, bsz, lp):
    d = h.shape[1]
    tm = TQ
    nt = lp // tm
    row = lambda w, dt: jax.ShapeDtypeStruct((bsz, lp, w), dt)
    spec = lambda w: pl.BlockSpec((1, tm, w), lambda i, j: (i, j, 0))
    return pl.pallas_call(
        _inproj_kernel,
        grid=(bsz, nt),
        in_specs=[
            pl.BlockSpec((tm, d), lambda i, j: (i * nt + j, 0)),
            pl.BlockSpec((d, IN_WIDTH), lambda i, j: (0, 0)),
            pl.BlockSpec((GROUP_WIDTH, d), lambda i, j: (0, 0)),
        ],
        out_specs=[
            spec(512), spec(256), spec(256),
            pl.BlockSpec((1, 1, GROUP_WIDTH, tm), lambda i, j: (i, j, 0, 0)),
            spec(1024), spec(512),
        ],
        out_shape=[
            row(512, BF16), row(256, BF16), row(256, BF16),
            jax.ShapeDtypeStruct((bsz, nt, GROUP_WIDTH, tm), BF16),
            row(1024, F32), row(512, F32),
        ],
        compiler_params=_params("parallel", "parallel"),
        name="in_proj",
    )(h, w_bf, wvt_bf)


def _swa_kernel(rel_ref, sink_ref, bband_ref, bmeta_ref, cur_ref, prev_ref, meta_ref, o_ref, band_sc, metab_sc):
    j = pl.program_id(1)

    @pl.when(j == 0)
    def _():
        for h in range(N_HEADS):
            band_sc[h] = _gather_bias(bband_ref[...], rel_ref, h)

        for n in range(3):
            for h in range(N_HEADS):
                metab_sc[n, h] = _gather_bias(bmeta_ref[n], rel_ref, h)

    nsub = TQ // BLK
    kr = lax.broadcasted_iota(jnp.int32, (2 * BLK, 1), 0)
    dq = lax.broadcasted_iota(jnp.int32, (1, BLK), 1) - kr + BLK
    in_window = (dq >= 0) & (dq < BLK)
    mrow = lax.broadcasted_iota(jnp.int32, (N_META, 1), 0)
    eye = (lax.broadcasted_iota(jnp.int32, (BLK, BLK), 0) == lax.broadcasted_iota(jnp.int32, (BLK, BLK), 1))
    meta = meta_ref[0, 0:N_META, :]
    for u in range(nsub):
        cur = cur_ref[0, u * BLK:(u + 1) * BLK, :]
        prev = prev_ref[0] if u == 0 else cur_ref[0, (u - 1) * BLK:u * BLK, :]
        base = (j * nsub + u) * BLK
        ok_meta = base + lax.broadcasted_iota(jnp.int32, (1, BLK), 1) >= mrow
        ok_band = in_window & (base - BLK + kr >= N_META)
        mtab = jnp.where(j == 0, u, 2)
        outs = []
        for h in range(N_HEADS):
            g = h // (N_HEADS // SWA_KV_HEADS)
            kc, vc = 256 + HEAD_DIM * g, 384 + HEAD_DIM * g
            q = cur[:, HEAD_DIM * h:HEAD_DIM * (h + 1)]
            kcat = jnp.concatenate([prev[:, kc:kc + HEAD_DIM], cur[:, kc:kc + HEAD_DIM]], axis=0)
            vcat = jnp.concatenate([prev[:, vc:vc + HEAD_DIM], cur[:, vc:vc + HEAD_DIM]], axis=0)
            s_m = jnp.where(ok_meta, _dot_nt(meta[:, kc:kc + HEAD_DIM], q) + metab_sc[mtab, h], NEG)
            s_b = jnp.where(ok_band, _dot_nt(kcat, q) + band_sc[h], NEG)
            sink = sink_ref[h]
            m = jnp.maximum(jnp.maximum(jnp.max(s_m, 0, keepdims=True), jnp.max(s_b, 0, keepdims=True)), sink)
            p_m, p_b = jnp.exp(s_m - m), jnp.exp(s_b - m)
            den = jnp.sum(p_m, 0, keepdims=True) + jnp.sum(p_b, 0, keepdims=True) + jnp.exp(sink - m)
            o = _dot_tn(meta[:, vc:vc + HEAD_DIM], p_m.astype(BF16)) + _dot_tn(vcat, p_b.astype(BF16))
            outs.append(o / den)
        ot = jnp.concatenate(outs, axis=0).astype(BF16)
        o_ref[0, u * BLK:(u + 1) * BLK, :] = _dot_nt(eye.astype(BF16), ot).astype(BF16)


def _swa(ua, rel_a, sinks, bkt_band, bkt_meta):
    bsz, lp, _ = ua.shape
    nsub = TQ // BLK
    smem = pl.BlockSpec(memory_space=pltpu.SMEM)
    return pl.pallas_call(
        _swa_kernel,
        grid=(bsz, lp // TQ),
        in_specs=[
            smem, smem,
            pl.BlockSpec((2 * BLK, BLK), lambda i, j: (0, 0)),
            pl.BlockSpec((3, N_META, BLK), lambda i, j: (0, 0, 0)),
            pl.BlockSpec((1, TQ, SWA_IN), lambda i, j: (i, j, 0)),
            pl.BlockSpec((1, BLK, SWA_IN), lambda i, j: (i, jnp.maximum(j * nsub - 1, 0), 0)),
            pl.BlockSpec((1, BLK, SWA_IN), lambda i, j: (i, 0, 0)),
        ],
        out_specs=pl.BlockSpec((1, TQ, GROUP_WIDTH), lambda i, j: (i, j, 0)),
        out_shape=jax.ShapeDtypeStruct((bsz, lp, GROUP_WIDTH), BF16),
        scratch_shapes=[pltpu.VMEM((N_HEADS, 2 * BLK, BLK), F32), pltpu.VMEM((3, N_HEADS, N_META, BLK), F32)],
        compiler_params=_params("arbitrary", "arbitrary"),
        name="swa_attn",
    )(rel_a, sinks, bkt_band, bkt_meta, ua, ua, ua)


def _diff_kernel(lam_ref, rel_ref, bkt_ref, g_ref, q_ref, k_ref, vt_ref, o_ref, wt_sc, b1_sc, addm_sc, m_sc, acc_sc):
    i = pl.program_id(1)
    nsub = TQ // BLK

    @pl.when(i == 0)
    def _():
        kr = lax.broadcasted_iota(jnp.int32, (BLK, 1), 0)
        qc = lax.broadcasted_iota(jnp.int32, (1, BLK), 1)
        for h in range(N_HEADS):
            far = rel_ref[REL_BUCKETS - 1, h]
            b0 = jnp.where(qc >= kr, _gather_bias(bkt_ref[0], rel_ref, h, far) * LOG2E, NEG)
            b1 = _gather_bias(bkt_ref[1], rel_ref, h, far) * LOG2E
            b1_sc[h] = b1
            blocks = {0: b0, 1: b1, 2: jnp.zeros((BLK, BLK), F32)}
            masked = jnp.full((BLK, BLK), NEG, F32)
            for u in range(nsub):
                row = [blocks[w - u] if w >= u else masked for w in range(nsub)]
                addm_sc[h, u * BLK:(u + 1) * BLK, :] = jnp.concatenate(row + row, axis=1)

    q = q_ref[0]
    lane_grp = lax.broadcasted_iota(jnp.int32, (1, GROUP_WIDTH), 1) // DIFF_QK
    zero = jnp.zeros_like(q)
    for h in range(N_HEADS):
        wt_sc[h, 0:TQ, :] = jnp.where(lane_grp == 2 * h, q, zero)
        wt_sc[h, TQ:2 * TQ, :] = jnp.where(lane_grp == 2 * h + 1, q, zero)
    m_sc[...] = jnp.full(m_sc.shape, NEG, F32)
    acc_sc[...] = jnp.zeros(acc_sc.shape, F32)

    ones_rows = jnp.ones((2 * 8, TQ), BF16)

    def scores(t, h):
        return _dot_nt(k_ref[0, pl.ds(pl.multiple_of(t * TQ, TQ), TQ), :], wt_sc[h])

    def consume(t, h, s, kind):
        if kind == "diag":
            s = s + addm_sc[h]
        elif kind == "near":
            top, bot = s[0:TQ - BLK], s[TQ - BLK:TQ]
            b1 = b1_sc[h]
            bot = jnp.concatenate([bot[:, 0:BLK] + b1, bot[:, BLK:TQ], bot[:, TQ:TQ + BLK] + b1,
                                   bot[:, TQ + BLK:2 * TQ]], axis=1)
            s = jnp.concatenate([top, bot], axis=0)
        m_old = m_sc[h]
        cm = jnp.maximum(jnp.maximum(s[0:BLK], s[BLK:2 * BLK]), s[2 * BLK:3 * BLK])
        n = BLK
        while n > 8:
            n //= 2
            cm = jnp.maximum(cm[0:n], cm[n:2 * n])
        m_new = jnp.maximum(m_old, jnp.max(cm, 0, keepdims=True))
        a = jnp.exp2(m_old - m_new)
        p = jnp.exp2(s - m_new).astype(BF16)
        vth = vt_ref[0, t, HEAD_DIM * h:HEAD_DIM * (h + 1), :]
        pv = _dot(jnp.concatenate([vth, ones_rows], axis=0), p)
        acc_sc[h] = a * acc_sc[h] + pv
        m_sc[h] = m_new

    def tile(t, kind):
        for h in range(N_HEADS):
            consume(t, h, scores(t, h), kind)

    def far(t, c):
        tile(t, None)
        return c

    lax.fori_loop(0, i - 1, far, 0)

    @pl.when(i >= 1)
    def _():
        tile(i - 1, "near")

    tile(i, "diag")

    lam = lam_ref[0]
    post = lam_ref[1]
    outs = []
    for h in range(N_HEADS):
        acc = acc_sc[h]
        o = acc[0:HEAD_DIM] / acc[HEAD_DIM:HEAD_DIM + 1]
        o = o[:, 0:TQ] - lam * o[:, TQ:2 * TQ]
        ms = jnp.mean(o * o, 0, keepdims=True)
        outs.append(o * lax.rsqrt(ms + 1e-5) * g_ref[...] * post)
    ot = jnp.concatenate(outs, axis=0).astype(BF16)
    eye = (lax.broadcasted_iota(jnp.int32, (TQ, TQ), 0) == lax.broadcasted_iota(jnp.int32, (TQ, TQ), 1))
    o_ref[0] = _dot_nt(eye.astype(BF16), ot).astype(BF16)


def _diff(qd, kd, vdt, lam2, rel_b, bkt_d, g_b):
    bsz, lp, _ = qd.shape
    nq = lp // TQ
    smem = pl.BlockSpec(memory_space=pltpu.SMEM)
    return pl.pallas_call(
        _diff_kernel,
        grid=(bsz, nq),
        in_specs=[
            smem, smem,
            pl.BlockSpec((2, BLK, BLK), lambda b, i: (0, 0, 0)),
            pl.BlockSpec((HEAD_DIM, 1), lambda b, i: (0, 0)),
            pl.BlockSpec((1, TQ, GROUP_WIDTH), lambda b, i: (b, i, 0)),
            pl.BlockSpec((1, lp, GROUP_WIDTH), lambda b, i: (b, 0, 0)),
            pl.BlockSpec((1, nq, GROUP_WIDTH, TQ), lambda b, i: (b, 0, 0, 0)),
        ],
        out_specs=pl.BlockSpec((1, TQ, GROUP_WIDTH), lambda b, i: (b, i, 0)),
        out_shape=jax.ShapeDtypeStruct((bsz, lp, GROUP_WIDTH), BF16),
        scratch_shapes=[
            pltpu.VMEM((N_HEADS, 2 * TQ, GROUP_WIDTH), BF16),
            pltpu.VMEM((N_HEADS, BLK, BLK), F32),
            pltpu.VMEM((N_HEADS, TQ, 2 * TQ), F32),
            pltpu.VMEM((N_HEADS, 1, 2 * TQ), F32),
            pltpu.VMEM((N_HEADS, HEAD_DIM + 2 * 8, 2 * TQ), F32),
        ],
        compiler_params=_params("arbitrary", "arbitrary"),
        name="diff_attn",
    )(lam2, rel_b, bkt_d, g_b, qd, kd, vdt)


def _rwkv_kernel(x_ref, xp_ref, mu_ref, w0_ref, w2_ref, a0_ref, a2_ref, g2_ref, kk_ref, ka_ref, rk_ref,
                 lg_ref, lb_ref, o_ref, zt_sc):
    c = pl.program_id(1)
    n = CHUNK
    rstep = x_ref.shape[1]
    ns = rstep // CHUNK
    gw = GROUP_WIDTH

    @pl.when(c == 0)
    def _():
        zt_sc[...] = jnp.zeros(zt_sc.shape, F32)

    x = x_ref[0]
    row = lax.broadcasted_iota(jnp.int32, (rstep, 1), 0)
    last_prev = jnp.where(c == 0, 0.0, xp_ref[0, 7:8, :])
    xs = jnp.where(row == 0, last_prev, pltpu.roll(x, 1, 0))
    xm = x + mu_ref[...] * (xs - x)
    r, k, v = xm[:, 0:gw], xm[:, gw:2 * gw], xm[:, 2 * gw:3 * gw]
    xw = xm[:, 768:768 + DECAY_LORA]
    xa = xm[:, 832:832 + AAA_LORA]
    xg = xm[:, 896:896 + GATE_LORA]
    logw = -math.exp(-0.5) * _sigmoid(w0_ref[...] + _dot3(jnp.tanh(xw), w2_ref[...]))
    a = _sigmoid(a0_ref[...] + _dot3(xa, a2_ref[...]))
    g = _dot3(_sigmoid(xg), g2_ref[...])

    lane_head = lax.broadcasted_iota(jnp.int32, (1, gw), 1) // HEAD_DIM
    sub_head = lax.broadcasted_iota(jnp.int32, (gw, 1), 0) // HEAD_DIM
    head_ones = (sub_head == lane_head).astype(BF16)

    kk = k * kk_ref[...]
    kk = kk / jnp.maximum(jnp.sqrt(_dot_exact_rhs(kk * kk, head_ones)), 1e-12)
    k2 = k * (1.0 + (a - 1.0) * ka_ref[...])
    bvec = kk * a

    ri = lax.broadcasted_iota(jnp.int32, (rstep, rstep), 0)
    ci = lax.broadcasted_iota(jnp.int32, (rstep, rstep), 1)
    tri = (((ri // n) == (ci // n)) & (ri >= ci)).astype(BF16)
    cum = _dot_exact_lhs(tri, logw)
    tot = jnp.concatenate([jnp.broadcast_to(cum[(s + 1) * n - 1:(s + 1) * n, :], (n, gw)) for s in range(ns)], axis=0)
    e_neg = jnp.exp(-cum)
    e_last = jnp.exp(tot - cum)
    a_t = (-kk * jnp.exp(cum - logw)).astype(BF16)
    r_t = (r * jnp.exp(cum)).astype(BF16)
    b_t, k_t = (bvec * e_neg).astype(BF16), (k2 * e_neg).astype(BF16)
    b_h, k_h = (bvec * e_last).astype(BF16), (k2 * e_last).astype(BF16)
    g_c = jnp.exp(tot)
    vb = v.astype(BF16)

    def stack(t, s):
        ts = t[s * n:(s + 1) * n]
        return jnp.concatenate([jnp.where(lane_head == h, ts, jnp.zeros_like(ts)) for h in range(N_HEADS)], axis=0)

    rr = lax.broadcasted_iota(jnp.int32, (gw, gw), 0)
    cc = lax.broadcasted_iota(jnp.int32, (gw, gw), 1)
    parts = []
    for s in range(ns):
        ast, rst, bst, kst = stack(a_t, s), stack(r_t, s), stack(b_t, s), stack(k_t, s)
        gmat = _dot_nt(jnp.concatenate([ast, rst], axis=0), jnp.concatenate([bst, kst], axis=0))
        low = jnp.where(rr > cc, gmat[0:gw, 0:gw], 0.0)
        aak = jnp.where(rr > cc, gmat[0:gw, gw:2 * gw], 0.0).astype(BF16)
        arb = jnp.where(rr >= cc, gmat[gw:2 * gw, 0:gw], 0.0).astype(BF16)
        ark = jnp.where(rr >= cc, gmat[gw:2 * gw, gw:2 * gw], 0.0).astype(BF16)
        pw = low
        tinv = jnp.where(rr == cc, 1.0, low)
        for _ in range(5):
            pwb = pw.astype(BF16)
            pw = _dot(pwb, pwb)
            tinv = tinv + _dot(tinv.astype(BF16), pw.astype(BF16))
        parts.append((ast, rst, aak, arb, ark, tinv.astype(BF16), stack(b_h, s), stack(k_h, s), stack(vb, s)))

    zt = zt_sc[...]
    ys = []
    for s in range(ns):
        ast, rst, aak, arb, ark, tinv, bhs, khs, vst = parts[s]
        ztb = zt.astype(BF16)
        u = _dot(tinv, (_dot_nt(ast, ztb) + _dot(aak, vst)).astype(BF16)).astype(BF16)
        ybd = _dot_nt(rst, ztb) + _dot(arb, u) + _dot(ark, vst)
        zt = zt * g_c[s * n:s * n + 1, :] + _dot_tn(u, bhs) + _dot_tn(vst, khs)
        ys.append(ybd[0:n] + ybd[n:2 * n] + ybd[2 * n:3 * n] + ybd[3 * n:4 * n])
    zt_sc[...] = zt
    y = jnp.concatenate(ys, axis=0)

    mean = _dot_exact_rhs(y, head_ones) * (1.0 / HEAD_DIM)
    yc = y - mean
    var = _dot_exact_rhs(yc * yc, head_ones) * (1.0 / HEAD_DIM)
    yn = yc * lax.rsqrt(var + 64e-5) * lg_ref[...] + lb_ref[...]
    bonus = _dot_exact_rhs(r * k2 * rk_ref[...], head_ones) * v
    o_ref[0] = ((yn + bonus) * g).astype(BF16)


def _rwkv(uc, mu, w0, w2, a0, a2, g2, k_k, k_a, r_k, lnx_g, lnx_b):
    bsz, lp, w = uc.shape
    rstep = _pick_tile(lp, RWKV_STEPS)
    nc = lp // rstep
    vec = lambda t: t.reshape(1, -1)
    full = lambda t: pl.BlockSpec(t.shape, lambda b, c: (0,) * t.ndim)
    args = [vec(mu), vec(w0), w2, vec(a0), a2, g2, vec(k_k), vec(k_a), vec(r_k), vec(lnx_g), vec(lnx_b)]
    return pl.pallas_call(
        _rwkv_kernel,
        grid=(bsz, nc),
        in_specs=[
            pl.BlockSpec((1, rstep, w), lambda b, c: (b, c, 0)),
            pl.BlockSpec((1, 8, w), lambda b, c: (b, jnp.maximum(c * (rstep // 8) - 1, 0), 0)),
        ] + [full(t) for t in args],
        out_specs=pl.BlockSpec((1, rstep, GROUP_WIDTH), lambda b, c: (b, c, 0)),
        out_shape=jax.ShapeDtypeStruct((bsz, lp, GROUP_WIDTH), BF16),
        scratch_shapes=[pltpu.VMEM((GROUP_WIDTH, GROUP_WIDTH), F32)],
        compiler_params=_params("arbitrary", "arbitrary"),
        name="rwkv7",
    )(uc, uc, *args)


def _conv_kernel(cur_ref, prev_ref, w_ref, b_ref, g_ref, gb_ref, o_ref, hcat_sc):
    j = pl.program_id(1)
    ch = GROUP_WIDTH

    def glu(t):
        return t[:, 0:ch] * _sigmoid(t[:, ch:2 * ch])

    hcat_sc[0:CONV_HIST, :] = jnp.where(j == 0, 0.0, glu(prev_ref[0]))
    hcat_sc[CONV_HIST:CONV_HIST + TQ, :] = glu(cur_ref[0])
    lead = CONV_HIST - (CONV_WIDTH - 1)
    acc = jnp.zeros((TQ, ch), F32) + b_ref[...]
    for ph in range(8):
        offs = [o for o in range(lead, CONV_HIST + 1) if o % 8 == ph]
        shifted = hcat_sc[ph:max(offs) + TQ, :]
        for o in offs:
            acc = acc + shifted[o - ph:o - ph + TQ] * w_ref[o - lead:o - lead + 1, :]
    lane_grp = lax.broadcasted_iota(jnp.int32, (1, ch), 1) // HEAD_DIM
    sub_grp = lax.broadcasted_iota(jnp.int32, (ch, 1), 0) // HEAD_DIM
    grp_ones = (sub_grp == lane_grp).astype(BF16)
    mean = _dot_exact_rhs(acc, grp_ones) * (1.0 / HEAD_DIM)
    xc = acc - mean
    var = _dot_exact_rhs(xc * xc, grp_ones) * (1.0 / HEAD_DIM)
    y = xc * lax.rsqrt(var + 1e-5) * g_ref[...] + gb_ref[...]
    o_ref[0] = (y * _sigmoid(y)).astype(BF16)


def _conv(ud, conv_w, conv_b, gn_g, gn_b):
    bsz, lp, w = ud.shape
    nb = lp // TQ
    vec = lambda t: t.reshape(1, -1)
    const = lambda shape: pl.BlockSpec(shape, lambda i, j: (0, 0))
    return pl.pallas_call(
        _conv_kernel,
        grid=(bsz, nb),
        in_specs=[
            pl.BlockSpec((1, TQ, w), lambda i, j: (i, j, 0)),
            pl.BlockSpec((1, CONV_HIST, w), lambda i, j: (i, jnp.maximum(j * (TQ // CONV_HIST) - 1, 0), 0)),
            const((CONV_WIDTH, GROUP_WIDTH)), const((1, GROUP_WIDTH)), const((1, GROUP_WIDTH)),
            const((1, GROUP_WIDTH)),
        ],
        out_specs=pl.BlockSpec((1, TQ, GROUP_WIDTH), lambda i, j: (i, j, 0)),
        out_shape=jax.ShapeDtypeStruct((bsz, lp, GROUP_WIDTH), BF16),
        scratch_shapes=[pltpu.VMEM((CONV_HIST + TQ, GROUP_WIDTH), F32)],
        compiler_params=_params("parallel", "arbitrary"),
        name="conv_module",
    )(ud, ud, conv_w, vec(conv_b), vec(gn_g), vec(gn_b))


def _router(logit_t, rb_ref):
    s = [_sigmoid(logit_t[e:e + 1, :]) for e in range(N_EXPERTS)]
    bz = [s[e] + rb_ref[e:e + 1, 0:1] for e in range(N_EXPERTS)]
    per = N_EXPERTS // N_GROUPS
    gsum = []
    for gi in range(N_GROUPS):
        a, b, c, d = bz[per * gi:per * gi + per]
        hi1, lo1, hi2, lo2 = jnp.maximum(a, b), jnp.minimum(a, b), jnp.maximum(c, d), jnp.minimum(c, d)
        gsum.append(jnp.maximum(hi1, hi2) + jnp.maximum(jnp.minimum(hi1, hi2), jnp.maximum(lo1, lo2)))
    best = jnp.zeros_like(gsum[0], dtype=jnp.int32)
    bval = gsum[0]
    for gi in range(1, N_GROUPS):
        take = gsum[gi] > bval
        best = jnp.where(take, gi, best)
        bval = jnp.where(take, gsum[gi], bval)

    def pick(vals, i):
        out = vals[i]
        for gi in range(1, N_GROUPS):
            out = jnp.where(best == gi, vals[per * gi + i], out)
        return out

    bv = [pick(bz, i) for i in range(per)]
    sv = [pick(s, i) for i in range(per)]
    i1 = jnp.zeros_like(best)
    v1 = bv[0]
    for i in range(1, per):
        take = bv[i] > v1
        i1 = jnp.where(take, i, i1)
        v1 = jnp.where(take, bv[i], v1)
    i2 = jnp.full_like(best, -1)
    v2 = jnp.full_like(v1, -jnp.inf)
    for i in range(per):
        take = (i1 != i) & (bv[i] > v2)
        i2 = jnp.where(take, i, i2)
        v2 = jnp.where(take, bv[i], v2)
    s1 = sv[0]
    s2 = sv[0]
    for i in range(1, per):
        s1 = jnp.where(i1 == i, sv[i], s1)
        s2 = jnp.where(i2 == i, sv[i], s2)
    tot = s1 + s2
    g1, g2 = s1 / tot, s2 / tot
    rows = []
    for e in range(N_EXPERTS):
        gi, i = divmod(e, per)
        hit = jnp.where(best == gi, jnp.where(i1 == i, g1, jnp.where(i2 == i, g2, 0.0)), 0.0)
        rows.append(hit)
    return jnp.concatenate(rows, axis=0)


def _outproj_kernel(ya_ref, yb_ref, yc_ref, yd_ref, w_ref, h_ref, g_ref, b_ref, rwt_ref, rb_ref, o_ref, comb_ref):
    gw = GROUP_WIDTH
    mix = (_dot(ya_ref[0], w_ref[0:gw, :]) + _dot(yb_ref[0], w_ref[gw:2 * gw, :])
           + _dot(yc_ref[0], w_ref[2 * gw:3 * gw, :]) + _dot(yd_ref[0], w_ref[3 * gw:4 * gw, :]))
    h1 = _ln(ALPHA * h_ref[...] + mix, g_ref[...], b_ref[...])
    o_ref[...] = h1
    wh, wl = _split2(rwt_ref[...])
    hh, hl = _split2(h1)
    logit_t = _dot_nt(wh, hh) + _dot_nt(wh, hl) + _dot_nt(wl, hh)
    comb_ref[...] = _router(logit_t, rb_ref)


def _outproj(ya, yb, yc, yd, w_bf, h, g, b, rwt, rb_b):
    bsz, lp, _ = ya.shape
    d = h.shape[1]
    tm = TQ
    nt = lp // tm
    spec = lambda w: pl.BlockSpec((1, tm, w), lambda i, j: (i, j, 0))
    flat = lambda w: pl.BlockSpec((tm, w), lambda i, j: (i * nt + j, 0))
    const = lambda shape: pl.BlockSpec(shape, lambda i, j: (0, 0))
    return pl.pallas_call(
        _outproj_kernel,
        grid=(bsz, nt),
        in_specs=[spec(GROUP_WIDTH)] * 4 + [const((d, d)), flat(d), const((1, d)), const((1, d)),
                                            const((N_EXPERTS, d)), const((N_EXPERTS, BLK))],
        out_specs=[flat(d), pl.BlockSpec((N_EXPERTS, tm), lambda i, j: (0, i * nt + j))],
        out_shape=[jax.ShapeDtypeStruct((bsz * lp, d), F32), jax.ShapeDtypeStruct((N_EXPERTS, bsz * lp), F32)],
        compiler_params=_params("parallel", "parallel"),
        name="out_proj_ln_router",
    )(ya, yb, yc, yd, w_bf, h, g.reshape(1, d), b.reshape(1, d), rwt, rb_b)


def _route_meta(comb, tm, lp, nreal):
    t = comb.shape[1]
    per = N_EXPERTS // N_GROUPS
    gsum = comb.reshape(N_GROUPS, per, t).sum(1)
    grp = jnp.argmax(gsum, axis=0).astype(jnp.int32)
    c4 = jnp.sum(jnp.where(jnp.arange(N_GROUPS)[:, None, None] == grp[None, None, :],
                           comb.reshape(N_GROUPS, per, t), 0.0), axis=0)
    mem = jnp.arange(per, dtype=jnp.int32)[:, None]
    lo = jnp.min(jnp.where(c4 > 0, mem, per), axis=0)
    hi = jnp.max(jnp.where(c4 > 0, mem, -1), axis=0)
    lo, hi = jnp.where(hi < 0, 0, lo), jnp.where(hi < 0, per - 1, hi)
    one = lo == hi
    lo, hi = jnp.where(one & (lo == per - 1), per - 2, lo), jnp.where(one, jnp.minimum(lo + 1, per - 1), hi)
    w_lo = jnp.sum(jnp.where(mem == lo[None, :], c4, 0.0), axis=0)
    w_hi = jnp.sum(jnp.where(mem == hi[None, :], c4, 0.0), axis=0)
    pair_base = lambda a: a * (2 * per - 1 - a) // 2
    ncls = N_GROUPS * MOE_PAIRS
    cls = jnp.where(jnp.arange(t) % lp < nreal, grp * MOE_PAIRS + pair_base(lo) + hi - lo - 1, ncls)
    order = jnp.argsort(cls, stable=True).astype(jnp.int32)
    counts = jnp.sum((cls[:, None] == jnp.arange(ncls)[None, :]).astype(jnp.int32), axis=0)
    start = jnp.cumsum(counts) - counts
    pcounts = (counts + tm - 1) // tm * tm
    pend = jnp.cumsum(pcounts)
    nt = pl.cdiv(t // lp * nreal, tm) + ncls
    s = jnp.arange(nt * tm, dtype=jnp.int32)
    past = s[:, None] >= pend[None, :]
    gs = jnp.minimum(jnp.sum(past.astype(jnp.int32), axis=1), ncls - 1)
    skipped = jnp.sum(jnp.where(past, (pcounts - counts)[None, :], 0), axis=1)
    in_cls = s - jnp.sum(jnp.where(past, pcounts[None, :], 0), axis=1)
    valid = in_cls < jnp.sum(jnp.where(gs[:, None] == jnp.arange(ncls)[None, :], counts[None, :], 0), axis=1)
    valid = valid & (s < pend[ncls - 1])
    tok = order[jnp.clip(s - skipped, 0, t - 1)]
    src = jnp.where(valid, tok, -1)
    c2 = jnp.where(valid[:, None], jnp.stack([w_lo[tok], w_hi[tok]], axis=-1), 0.0)
    n_tiles = (pend[-1] // tm).astype(jnp.int32)
    tile_first = jnp.minimum(jnp.arange(nt, dtype=jnp.int32), n_tiles - 1) * tm
    tcls = jnp.minimum(jnp.sum((tile_first[:, None] >= pend[None, :]).astype(jnp.int32), axis=1), ncls - 1)
    pid = tcls % MOE_PAIRS
    p_lo = (pid >= pair_base(1)).astype(jnp.int32) + (pid >= pair_base(2)).astype(jnp.int32)
    p_hi = pid - pair_base(p_lo) + p_lo + 1
    e_lo = (tcls // MOE_PAIRS * per + p_lo).astype(jnp.int32)
    e_hi = (tcls // MOE_PAIRS * per + p_hi).astype(jnp.int32)
    return src, e_lo, e_hi, n_tiles.reshape(1), c2, nt


def _moe_kernel(src_ref, elo_ref, ehi_ref, nt_ref, h_hbm, c_ref, w1a_ref, w3a_ref, w2a_ref, w1b_ref, w3b_ref, w2b_ref,
                g_ref, b_ref, o_hbm, xbuf, obuf, gsem, ssem, *, tm, bsz, lp, nreal):
    i = pl.program_id(0)
    nt = nt_ref[0]
    slot = lax.rem(i, 2)
    npad = lp - nreal

    def pad_row(q):
        return (q // npad) * lp + nreal + q % npad

    def gather(tile, sl):
        base = tile * tm
        for r in range(tm):
            tok = jnp.maximum(src_ref[base + r], 0)
            pltpu.make_async_copy(h_hbm.at[pl.ds(tok, 1)], xbuf.at[sl, pl.ds(r, 1)], gsem.at[sl]).start()

    def scatter(sl):
        base = i * tm
        for r in range(tm):
            s = src_ref[base + r]
            dst = jnp.where(s >= 0, s, pad_row(sl * tm + r))
            pltpu.make_async_copy(obuf.at[sl, pl.ds(r, 1)], o_hbm.at[pl.ds(dst, 1)], ssem.at[sl]).start()

    def wait_gather(sl):
        pltpu.make_async_copy(h_hbm.at[pl.ds(0, tm)], xbuf.at[sl], gsem.at[sl]).wait()

    def wait_scatter(sl):
        pltpu.make_async_copy(obuf.at[sl], o_hbm.at[pl.ds(0, tm)], ssem.at[sl]).wait()

    @pl.when(i == 0)
    def _():
        gather(0, 0)
        obuf[0, 0:npad, :] = jnp.zeros((npad, obuf.shape[2]), F32)
        for bi in range(bsz):
            fill = pltpu.make_async_copy(obuf.at[0, pl.ds(0, npad)], o_hbm.at[pl.ds(bi * lp + nreal, npad)],
                                         ssem.at[0])
            fill.start()
            fill.wait()

    for sl in range(2):
        @pl.when((i + 1 < nt) & (slot == 1 - sl))
        def _(sl=sl):
            gather(i + 1, sl)

    @pl.when(i < nt)
    def _():
        wait_gather(slot)

        @pl.when(i >= 2)
        def _():
            wait_scatter(slot)

        x = xbuf[slot]
        xb = x.astype(BF16)
        c = c_ref[...]
        y = jnp.zeros(x.shape, F32)
        for e, (w1_ref, w3_ref, w2_ref) in enumerate(((w1a_ref, w3a_ref, w2a_ref), (w1b_ref, w3b_ref, w2b_ref))):
            a1 = _dot(xb, w1_ref[0])
            act = (a1 * _sigmoid(a1)) * _dot(xb, w3_ref[0]) * c[:, e:e + 1]
            y = y + _dot(act.astype(BF16), w2_ref[0])
        obuf[slot] = _ln(ALPHA * x + y, g_ref[...], b_ref[...])

        for sl in range(2):
            @pl.when(slot == sl)
            def _(sl=sl):
                scatter(sl)

    @pl.when(i == nt - 1)
    def _():
        wait_scatter(slot)

        @pl.when(nt >= 2)
        def _():
            wait_scatter(1 - slot)


def _moe(h1, comb, w1, w3, w2, g, b, bsz, lp, nreal):
    t, d = h1.shape
    tm = MOE_TM
    assert bsz * (lp - nreal) >= 2 * tm and lp - nreal <= tm
    src, e_lo, e_hi, n_tiles, c2, nt = _route_meta(comb, tm, lp, nreal)
    const = lambda i, src, elo, ehi, ntl: (0, 0)
    wa = lambda a, b_: pl.BlockSpec((1, a, b_), lambda i, src, elo, ehi, ntl: (elo[i], 0, 0))
    wb = lambda a, b_: pl.BlockSpec((1, a, b_), lambda i, src, elo, ehi, ntl: (ehi[i], 0, 0))
    grid_spec = pltpu.PrefetchScalarGridSpec(
        num_scalar_prefetch=4,
        grid=(nt,),
        in_specs=[
            pl.BlockSpec(memory_space=pl.ANY),
            pl.BlockSpec((tm, 2), lambda i, src, elo, ehi, ntl: (i, 0)),
            wa(d, EXPERT_FF), wa(d, EXPERT_FF), wa(EXPERT_FF, d),
            wb(d, EXPERT_FF), wb(d, EXPERT_FF), wb(EXPERT_FF, d),
            pl.BlockSpec((1, d), const),
            pl.BlockSpec((1, d), const),
        ],
        out_specs=pl.BlockSpec(memory_space=pl.ANY),
        scratch_shapes=[
            pltpu.VMEM((2, tm, d), F32), pltpu.VMEM((2, tm, d), F32),
            pltpu.SemaphoreType.DMA((2,)), pltpu.SemaphoreType.DMA((2,)),
        ],
    )
    return pl.pallas_call(
        functools.partial(_moe_kernel, tm=tm, bsz=bsz, lp=lp, nreal=nreal),
        grid_spec=grid_spec,
        out_shape=jax.ShapeDtypeStruct((t, d), F32),
        compiler_params=_params("arbitrary"),
        name="moe_ln",
    )(src, e_lo, e_hi, n_tiles, h1, c2, w1, w3, w2, w1, w3, w2, g.reshape(1, d), b.reshape(1, d))


def kernel(x, meta, ln0_g, ln0_b, w_in, swa_sinks, rel_bias, diff_lq1, diff_lk1, diff_lq2, diff_lk2, diff_subln_g,
           rwkv_mu, rwkv_w0, rwkv_w2, rwkv_a0, rwkv_a2, rwkv_g2, rwkv_kk, rwkv_ka, rwkv_rk, rwkv_lnx_g,
           rwkv_lnx_b, conv_w, conv_b, conv_gn_g, conv_gn_b, w_out, ln1_g, ln1_b, router_w, router_b, exp_w1,
           exp_w3, exp_w2, ln2_g, ln2_b):
    bsz, seq, d = x.shape
    assert d == D_MODEL and seq % BLK == 0
    lp = pl.cdiv(seq + N_META, TQ) * TQ

    r = jnp.arange(BLK)
    bkt_band = _t5_bucket(r[None, :] - jnp.arange(2 * BLK)[:, None] + BLK)
    qpos3 = jnp.arange(3 * BLK).reshape(3, BLK)
    bkt_meta = _t5_bucket(qpos3[:, None, :] - jnp.arange(N_META)[None, :, None])
    dq0 = r[None, :] - r[:, None]
    bkt_d = jnp.stack([_t5_bucket(dq0), _t5_bucket(dq0 + BLK)])
    rel_a, rel_b = rel_bias[:, :N_HEADS], rel_bias[:, N_HEADS:]
    rwt = router_w.T
    rb_b = jnp.broadcast_to(router_b.astype(F32)[:, None], (N_EXPERTS, BLK))
    vcols = slice(SWA_IN + 512, SWA_IN + 768)

    h = _prep(x, meta, ln0_g, ln0_b, lp).reshape(bsz * lp, d)
    for l in range(DEPTH):
        w_l = w_in[l]
        ua, qd, kd, vdt, uc, ud = _inproj(h, w_l.astype(BF16), w_l[:, vcols].T.astype(BF16), bsz, lp)
        ya = _swa(ua, rel_a, swa_sinks[l], bkt_band, bkt_meta)
        lam_init = 0.8 - 0.6 * math.exp(-0.3 * l)
        lam = (jnp.exp(jnp.sum(diff_lq1[l] * diff_lk1[l])) - jnp.exp(jnp.sum(diff_lq2[l] * diff_lk2[l])) + lam_init)
        lam2 = jnp.stack([lam, jnp.asarray(1.0 - lam_init, F32)]).astype(F32)
        yb = _diff(qd, kd, vdt, lam2, rel_b, bkt_d, diff_subln_g[l].reshape(HEAD_DIM, 1))
        yc = _rwkv(uc, rwkv_mu[l], rwkv_w0[l], rwkv_w2[l], rwkv_a0[l], rwkv_a2[l], rwkv_g2[l], rwkv_kk[l],
                   rwkv_ka[l], rwkv_rk[l], rwkv_lnx_g[l], rwkv_lnx_b[l])
        yd = _conv(ud, conv_w[l], conv_b[l], conv_gn_g[l], conv_gn_b[l])
        h1, comb = _outproj(ya, yb, yc, yd, w_out[l].astype(BF16), h, ln1_g[l], ln1_b[l], rwt, rb_b)
        h = _moe(h1, comb, exp_w1[l].astype(BF16), exp_w3[l].astype(BF16), exp_w2[l].astype(BF16),
                 ln2_g[l], ln2_b[l], bsz, lp, N_META + seq)
    return h.reshape(bsz, lp, d)[:, N_META:N_META + seq]
```

```python
import functools
import math

import jax
import jax.numpy as jnp
import numpy as np
from jax import lax
from jax.experimental import pallas as pl
from jax.experimental.pallas import tpu as pltpu

F32 = jnp.float32
BF16 = jnp.bfloat16

D_MODEL = 1024
N_META = 16
BLK = 128
TQ = 3 * BLK
HEAD_DIM = 64
GROUP_WIDTH = 256
N_HEADS = 4
SWA_KV_HEADS = 2
DIFF_QK = 32
DECAY_LORA = 64
AAA_LORA = 64
GATE_LORA = 128
CONV_WIDTH = 31
CONV_HIST = 32
REL_BUCKETS = 32
REL_MAX_DIST = 128
N_EXPERTS = 16
N_GROUPS = 4
EXPERT_FF = 512
DEPTH = 2
ALPHA = (2 * DEPTH) ** 0.25
SWA_IN = 512
DIFF_IN = 768
RWKV_IN = 1024
CONV_IN = 512
IN_WIDTH = SWA_IN + DIFF_IN + RWKV_IN + CONV_IN
NEG = -1e30
LOG2E = math.log2(math.e)
CHUNK = 64
RWKV_STEPS = (11 * CHUNK, 6 * CHUNK)
MOE_TM = 256
MOE_PAIRS = 6
VMEM_LIMIT = 56 * 1024 * 1024


def _dot(a, b, prec=None):
    return jnp.dot(a, b, preferred_element_type=F32, precision=prec)


def _dot_nt(a, b, prec=None):
    return lax.dot_general(a, b, (((1,), (1,)), ((), ())), preferred_element_type=F32, precision=prec)


def _dot_tn(a, b, prec=None):
    return lax.dot_general(a, b, (((0,), (0,)), ((), ())), preferred_element_type=F32, precision=prec)


def _split2(x):
    hi = x.astype(BF16)
    return hi, (x - hi.astype(F32)).astype(BF16)


def _dot_exact_rhs(x, m_bf):
    hi, lo = _split2(x)
    return _dot(hi, m_bf) + _dot(lo, m_bf)


def _dot_exact_lhs(m_bf, x):
    hi, lo = _split2(x)
    return _dot(m_bf, hi) + _dot(m_bf, lo)


def _dot3(a, b):
    ah, al = _split2(a)
    bh, bl = _split2(b)
    return _dot(ah, bh) + _dot(ah, bl) + _dot(al, bh)


def _ln(x, g, b, eps=1e-5):
    mu = jnp.mean(x, -1, keepdims=True)
    xc = x - mu
    var = jnp.mean(xc * xc, -1, keepdims=True)
    return xc * lax.rsqrt(var + eps) * g + b


def _sigmoid(x):
    return 1.0 / (1.0 + jnp.exp(-x))


def _pick_tile(n, candidates):
    for c in candidates:
        if n % c == 0:
            return c
    raise ValueError(f"no tile in {candidates} divides {n}")


def _params(*sem):
    return pltpu.CompilerParams(dimension_semantics=sem, vmem_limit_bytes=VMEM_LIMIT)


def _t5_bucket(dist):
    n = jnp.maximum(dist, 0)
    max_exact = REL_BUCKETS // 2
    log_ratio = jnp.log(jnp.maximum(n, 1).astype(F32) / max_exact) / math.log(REL_MAX_DIST / max_exact)
    large = jnp.minimum(max_exact + (log_ratio * (REL_BUCKETS - max_exact)).astype(jnp.int32), REL_BUCKETS - 1)
    return jnp.where(n < max_exact, n, large)


def _gather_bias(bkt, rel_ref, h, offset=0.0):
    acc = jnp.zeros(bkt.shape, F32)
    for b in range(REL_BUCKETS):
        acc = jnp.where(bkt == b, rel_ref[b, h] - offset, acc)
    return acc


def _prep_kernel(xa_ref, x0_ref, x1_ref, x2_ref, meta_ref, g_ref, b_ref, o_ref, *, nbx):
    j = pl.program_id(1)
    nsub = TQ // BLK
    head = jnp.where(j == 0, meta_ref[...], xa_ref[0])
    o_ref[0, 0:N_META, :] = jnp.where(nsub * j <= nbx, _ln(head, g_ref[...], b_ref[...]), 0.0)
    for u, x_ref in enumerate((x0_ref, x1_ref, x2_ref)):
        rows = BLK if u < nsub - 1 else BLK - N_META
        lo = N_META + u * BLK
        body = _ln(x_ref[0, 0:rows, :], g_ref[...], b_ref[...])
        o_ref[0, lo:lo + rows, :] = jnp.where(nsub * j + u < nbx, body, 0.0)


def _prep(x, meta, g, b, lp):
    bsz, seq, d = x.shape
    nbx = seq // BLK
    per = BLK // N_META
    nsub = TQ // BLK
    xblk = lambda u: pl.BlockSpec((1, BLK, d), lambda i, j: (i, jnp.minimum(nsub * j + u, nbx - 1), 0))
    return pl.pallas_call(
        functools.partial(_prep_kernel, nbx=nbx),
        grid=(bsz, lp // TQ),
        in_specs=[
            pl.BlockSpec((1, N_META, d), lambda i, j: (i, jnp.clip(per * nsub * j - 1, 0, per * nbx - 1), 0)),
            xblk(0), xblk(1), xblk(2),
            pl.BlockSpec((N_META, d), lambda i, j: (0, 0)),
            pl.BlockSpec((1, d), lambda i, j: (0, 0)),
            pl.BlockSpec((1, d), lambda i, j: (0, 0)),
        ],
        out_specs=pl.BlockSpec((1, TQ, d), lambda i, j: (i, j, 0)),
        out_shape=jax.ShapeDtypeStruct((bsz, lp, d), F32),
        compiler_params=_params("parallel", "parallel"),
        name="prep_ln",
    )(x, x, x, x, meta, g.reshape(1, d), b.reshape(1, d))


def _inproj_kernel(h_ref, w_ref, wvt_ref, ua_ref, qd_ref, kd_ref, vdt_ref, uc_ref, ud_ref):
    hb = h_ref[...].astype(BF16)

    def proj(c0, c1):
        return _dot(hb, w_ref[:, c0:c1])

    ua_ref[0, :, 0:256] = (proj(0, 256) * (HEAD_DIM ** -0.5)).astype(BF16)
    ua_ref[0, :, 256:512] = proj(256, 512).astype(BF16)
    qd_ref[0] = (proj(512, 768) * (DIFF_QK ** -0.5 * LOG2E)).astype(BF16)
    kd_ref[0] = proj(768, 1024).astype(BF16)
    vdt_ref[0, 0] = _dot_nt(wvt_ref[...], hb).astype(BF16)
    uc_ref[0] = proj(1280, 2304)
    ud_ref[0] = proj(2304, 2816)


def _inproj(h, w_bf, wvt_bf, bsz, lp):
    d = h.shape[1]
    tm = TQ
    nt = lp // tm
    row = lambda w, dt: jax.ShapeDtypeStruct((bsz, lp, w), dt)
    spec = lambda w: pl.BlockSpec((1, tm, w), lambda i, j: (i, j, 0))
    return pl.pallas_call(
        _inproj_kernel,
        grid=(bsz, nt),
        in_specs=[
            pl.BlockSpec((tm, d), lambda i, j: (i * nt + j, 0)),
            pl.BlockSpec((d, IN_WIDTH), lambda i, j: (0, 0)),
            pl.BlockSpec((GROUP_WIDTH, d), lambda i, j: (0, 0)),
        ],
        out_specs=[
            spec(512), spec(256), spec(256),
            pl.BlockSpec((1, 1, GROUP_WIDTH, tm), lambda i, j: (i, j, 0, 0)),
            spec(1024), spec(512),
        ],
        out_shape=[
            row(512, BF16), row(256, BF16), row(256, BF16),
            jax.ShapeDtypeStruct((bsz, nt, GROUP_WIDTH, tm), BF16),
            row(1024, F32), row(512, F32),
        ],
        compiler_params=_params("parallel", "parallel"),
        name="in_proj",
    )(h, w_bf, wvt_bf)


def _swa_kernel(rel_ref, sink_ref, bband_ref, bmeta_ref, cur_ref, prev_ref, meta_ref, o_ref, band_sc, metab_sc):
    j = pl.program_id(1)

    @pl.when(j == 0)
    def _():
        for h in range(N_HEADS):
            band_sc[h] = _gather_bias(bband_ref[...], rel_ref, h)

        for n in range(3):
            for h in range(N_HEADS):
                metab_sc[n, h] = _gather_bias(bmeta_ref[n], rel_ref, h)

    nsub = TQ // BLK
    kr = lax.broadcasted_iota(jnp.int32, (2 * BLK, 1), 0)
    dq = lax.broadcasted_iota(jnp.int32, (1, BLK), 1) - kr + BLK
    in_window = (dq >= 0) & (dq < BLK)
    mrow = lax.broadcasted_iota(jnp.int32, (N_META, 1), 0)
    eye = (lax.broadcasted_iota(jnp.int32, (BLK, BLK), 0) == lax.broadcasted_iota(jnp.int32, (BLK, BLK), 1))
    meta = meta_ref[0, 0:N_META, :]
    for u in range(nsub):
        cur = cur_ref[0, u * BLK:(u + 1) * BLK, :]
        prev = prev_ref[0] if u == 0 else cur_ref[0, (u - 1) * BLK:u * BLK, :]
        base = (j * nsub + u) * BLK
        ok_meta = base + lax.broadcasted_iota(jnp.int32, (1, BLK), 1) >= mrow
        ok_band = in_window & (base - BLK + kr >= N_META)
        mtab = jnp.where(j == 0, u, 2)
        outs = []
        for h in range(N_HEADS):
            g = h // (N_HEADS // SWA_KV_HEADS)
            kc, vc = 256 + HEAD_DIM * g, 384 + HEAD_DIM * g
            q = cur[:, HEAD_DIM * h:HEAD_DIM * (h + 1)]
            kcat = jnp.concatenate([prev[:, kc:kc + HEAD_DIM], cur[:, kc:kc + HEAD_DIM]], axis=0)
            vcat = jnp.concatenate([prev[:, vc:vc + HEAD_DIM], cur[:, vc:vc + HEAD_DIM]], axis=0)
            s_m = jnp.where(ok_meta, _dot_nt(meta[:, kc:kc + HEAD_DIM], q) + metab_sc[mtab, h], NEG)
            s_b = jnp.where(ok_band, _dot_nt(kcat, q) + band_sc[h], NEG)
            sink = sink_ref[h]
            m = jnp.maximum(jnp.maximum(jnp.max(s_m, 0, keepdims=True), jnp.max(s_b, 0, keepdims=True)), sink)
            p_m, p_b = jnp.exp(s_m - m), jnp.exp(s_b - m)
            den = jnp.sum(p_m, 0, keepdims=True) + jnp.sum(p_b, 0, keepdims=True) + jnp.exp(sink - m)
            o = _dot_tn(meta[:, vc:vc + HEAD_DIM], p_m.astype(BF16)) + _dot_tn(vcat, p_b.astype(BF16))
            outs.append(o / den)
        ot = jnp.concatenate(outs, axis=0).astype(BF16)
        o_ref[0, u * BLK:(u + 1) * BLK, :] = _dot_nt(eye.astype(BF16), ot).astype(BF16)


def _swa(ua, rel_a, sinks, bkt_band, bkt_meta):
    bsz, lp, _ = ua.shape
    nsub = TQ // BLK
    smem = pl.BlockSpec(memory_space=pltpu.SMEM)
    return pl.pallas_call(
        _swa_kernel,
        grid=(bsz, lp // TQ),
        in_specs=[
            smem, smem,
            pl.BlockSpec((2 * BLK, BLK), lambda i, j: (0, 0)),
            pl.BlockSpec((3, N_META, BLK), lambda i, j: (0, 0, 0)),
            pl.BlockSpec((1, TQ, SWA_IN), lambda i, j: (i, j, 0)),
            pl.BlockSpec((1, BLK, SWA_IN), lambda i, j: (i, jnp.maximum(j * nsub - 1, 0), 0)),
            pl.BlockSpec((1, BLK, SWA_IN), lambda i, j: (i, 0, 0)),
        ],
        out_specs=pl.BlockSpec((1, TQ, GROUP_WIDTH), lambda i, j: (i, j, 0)),
        out_shape=jax.ShapeDtypeStruct((bsz, lp, GROUP_WIDTH), BF16),
        scratch_shapes=[pltpu.VMEM((N_HEADS, 2 * BLK, BLK), F32), pltpu.VMEM((3, N_HEADS, N_META, BLK), F32)],
        compiler_params=_params("arbitrary", "arbitrary"),
        name="swa_attn",
    )(rel_a, sinks, bkt_band, bkt_meta, ua, ua, ua)


def _diff_kernel(lam_ref, rel_ref, bkt_ref, g_ref, q_ref, k_ref, vt_ref, o_ref, wt_sc, b1_sc, addm_sc, m_sc, acc_sc):
    i = pl.program_id(1)
    nsub = TQ // BLK

    @pl.when(i == 0)
    def _():
        kr = lax.broadcasted_iota(jnp.int32, (BLK, 1), 0)
        qc = lax.broadcasted_iota(jnp.int32, (1, BLK), 1)
        for h in range(N_HEADS):
            far = rel_ref[REL_BUCKETS - 1, h]
            b0 = jnp.where(qc >= kr, _gather_bias(bkt_ref[0], rel_ref, h, far) * LOG2E, NEG)
            b1 = _gather_bias(bkt_ref[1], rel_ref, h, far) * LOG2E
            b1_sc[h] = b1
            blocks = {0: b0, 1: b1, 2: jnp.zeros((BLK, BLK), F32)}
            masked = jnp.full((BLK, BLK), NEG, F32)
            for u in range(nsub):
                row = [blocks[w - u] if w >= u else masked for w in range(nsub)]
                addm_sc[h, u * BLK:(u + 1) * BLK, :] = jnp.concatenate(row + row, axis=1)

    q = q_ref[0]
    lane_grp = lax.broadcasted_iota(jnp.int32, (1, GROUP_WIDTH), 1) // DIFF_QK
    zero = jnp.zeros_like(q)
    for h in range(N_HEADS):
        wt_sc[h, 0:TQ, :] = jnp.where(lane_grp == 2 * h, q, zero)
        wt_sc[h, TQ:2 * TQ, :] = jnp.where(lane_grp == 2 * h + 1, q, zero)
    m_sc[...] = jnp.full(m_sc.shape, NEG, F32)
    acc_sc[...] = jnp.zeros(acc_sc.shape, F32)

    ones_rows = jnp.ones((2 * 8, TQ), BF16)

    def scores(t, h):
        return _dot_nt(k_ref[0, pl.ds(pl.multiple_of(t * TQ, TQ), TQ), :], wt_sc[h])

    def consume(t, h, s, kind):
        if kind == "diag":
            s = s + addm_sc[h]
        elif kind == "near":
            top, bot = s[0:TQ - BLK], s[TQ - BLK:TQ]
            b1 = b1_sc[h]
            bot = jnp.concatenate([bot[:, 0:BLK] + b1, bot[:, BLK:TQ], bot[:, TQ:TQ + BLK] + b1,
                                   bot[:, TQ + BLK:2 * TQ]], axis=1)
            s = jnp.concatenate([top, bot], axis=0)
        m_old = m_sc[h]
        cm = jnp.maximum(jnp.maximum(s[0:BLK], s[BLK:2 * BLK]), s[2 * BLK:3 * BLK])
        n = BLK
        while n > 8:
            n //= 2
            cm = jnp.maximum(cm[0:n], cm[n:2 * n])
        m_new = jnp.maximum(m_old, jnp.max(cm, 0, keepdims=True))
        a = jnp.exp2(m_old - m_new)
        p = jnp.exp2(s - m_new).astype(BF16)
        vth = vt_ref[0, t, HEAD_DIM * h:HEAD_DIM * (h + 1), :]
        pv = _dot(jnp.concatenate([vth, ones_rows], axis=0), p)
        acc_sc[h] = a * acc_sc[h] + pv
        m_sc[h] = m_new

    def tile(t, kind):
        for h in range(N_HEADS):
            consume(t, h, scores(t, h), kind)

    def far(t, c):
        tile(t, None)
        return c

    lax.fori_loop(0, i - 1, far, 0)

    @pl.when(i >= 1)
    def _():
        tile(i - 1, "near")

    tile(i, "diag")

    lam = lam_ref[0]
    post = lam_ref[1]
    outs = []
    for h in range(N_HEADS):
        acc = acc_sc[h]
        o = acc[0:HEAD_DIM] / acc[HEAD_DIM:HEAD_DIM + 1]
        o = o[:, 0:TQ] - lam * o[:, TQ:2 * TQ]
        ms = jnp.mean(o * o, 0, keepdims=True)
        outs.append(o * lax.rsqrt(ms + 1e-5) * g_ref[...] * post)
    ot = jnp.concatenate(outs, axis=0).astype(BF16)
    eye = (lax.broadcasted_iota(jnp.int32, (TQ, TQ), 0) == lax.broadcasted_iota(jnp.int32, (TQ, TQ), 1))
    o_ref[0] = _dot_nt(eye.astype(BF16), ot).astype(BF16)


def _diff(qd, kd, vdt, lam2, rel_b, bkt_d, g_b):
    bsz, lp, _ = qd.shape
    nq = lp // TQ
    smem = pl.BlockSpec(memory_space=pltpu.SMEM)
    return pl.pallas_call(
        _diff_kernel,
        grid=(bsz, nq),
        in_specs=[
            smem, smem,
            pl.BlockSpec((2, BLK, BLK), lambda b, i: (0, 0, 0)),
            pl.BlockSpec((HEAD_DIM, 1), lambda b, i: (0, 0)),
            pl.BlockSpec((1, TQ, GROUP_WIDTH), lambda b, i: (b, i, 0)),
            pl.BlockSpec((1, lp, GROUP_WIDTH), lambda b, i: (b, 0, 0)),
            pl.BlockSpec((1, nq, GROUP_WIDTH, TQ), lambda b, i: (b, 0, 0, 0)),
        ],
        out_specs=pl.BlockSpec((1, TQ, GROUP_WIDTH), lambda b, i: (b, i, 0)),
        out_shape=jax.ShapeDtypeStruct((bsz, lp, GROUP_WIDTH), BF16),
        scratch_shapes=[
            pltpu.VMEM((N_HEADS, 2 * TQ, GROUP_WIDTH), BF16),
            pltpu.VMEM((N_HEADS, BLK, BLK), F32),
            pltpu.VMEM((N_HEADS, TQ, 2 * TQ), F32),
            pltpu.VMEM((N_HEADS, 1, 2 * TQ), F32),
            pltpu.VMEM((N_HEADS, HEAD_DIM + 2 * 8, 2 * TQ), F32),
        ],
        compiler_params=_params("arbitrary", "arbitrary"),
        name="diff_attn",
    )(lam2, rel_b, bkt_d, g_b, qd, kd, vdt)


def _rwkv_kernel(x_ref, xp_ref, mu_ref, w0_ref, w2_ref, a0_ref, a2_ref, g2_ref, kk_ref, ka_ref, rk_ref,
                 lg_ref, lb_ref, o_ref, zt_sc):
    c = pl.program_id(1)
    n = CHUNK
    rstep = x_ref.shape[1]
    ns = rstep // CHUNK
    gw = GROUP_WIDTH

    @pl.when(c == 0)
    def _():
        zt_sc[...] = jnp.zeros(zt_sc.shape, F32)

    x = x_ref[0]
    row = lax.broadcasted_iota(jnp.int32, (rstep, 1), 0)
    last_prev = jnp.where(c == 0, 0.0, xp_ref[0, 7:8, :])
    xs = jnp.where(row == 0, last_prev, pltpu.roll(x, 1, 0))
    xm = x + mu_ref[...] * (xs - x)
    r, k, v = xm[:, 0:gw], xm[:, gw:2 * gw], xm[:, 2 * gw:3 * gw]
    xw = xm[:, 768:768 + DECAY_LORA]
    xa = xm[:, 832:832 + AAA_LORA]
    xg = xm[:, 896:896 + GATE_LORA]
    logw = -math.exp(-0.5) * _sigmoid(w0_ref[...] + _dot3(jnp.tanh(xw), w2_ref[...]))
    a = _sigmoid(a0_ref[...] + _dot3(xa, a2_ref[...]))
    g = _dot3(_sigmoid(xg), g2_ref[...])

    lane_head = lax.broadcasted_iota(jnp.int32, (1, gw), 1) // HEAD_DIM
    sub_head = lax.broadcasted_iota(jnp.int32, (gw, 1), 0) // HEAD_DIM
    head_ones = (sub_head == lane_head).astype(BF16)

    kk = k * kk_ref[...]
    kk = kk / jnp.maximum(jnp.sqrt(_dot_exact_rhs(kk * kk, head_ones)), 1e-12)
    k2 = k * (1.0 + (a - 1.0) * ka_ref[...])
    bvec = kk * a

    ri = lax.broadcasted_iota(jnp.int32, (rstep, rstep), 0)
    ci = lax.broadcasted_iota(jnp.int32, (rstep, rstep), 1)
    tri = (((ri // n) == (ci // n)) & (ri >= ci)).astype(BF16)
    cum = _dot_exact_lhs(tri, logw)
    tot = jnp.concatenate([jnp.broadcast_to(cum[(s + 1) * n - 1:(s + 1) * n, :], (n, gw)) for s in range(ns)], axis=0)
    e_neg = jnp.exp(-cum)
    e_last = jnp.exp(tot - cum)
    a_t = (-kk * jnp.exp(cum - logw)).astype(BF16)
    r_t = (r * jnp.exp(cum)).astype(BF16)
    b_t, k_t = (bvec * e_neg).astype(BF16), (k2 * e_neg).astype(BF16)
    b_h, k_h = (bvec * e_last).astype(BF16), (k2 * e_last).astype(BF16)
    g_c = jnp.exp(tot)
    vb = v.astype(BF16)

    def stack(t, s):
        ts = t[s * n:(s + 1) * n]
        return jnp.concatenate([jnp.where(lane_head == h, ts, jnp.zeros_like(ts)) for h in range(N_HEADS)], axis=0)

    rr = lax.broadcasted_iota(jnp.int32, (gw, gw), 0)
    cc = lax.broadcasted_iota(jnp.int32, (gw, gw), 1)
    parts = []
    for s in range(ns):
        ast, rst, bst, kst = stack(a_t, s), stack(r_t, s), stack(b_t, s), stack(k_t, s)
        gmat = _dot_nt(jnp.concatenate([ast, rst], axis=0), jnp.concatenate([bst, kst], axis=0))
        low = jnp.where(rr > cc, gmat[0:gw, 0:gw], 0.0)
        aak = jnp.where(rr > cc, gmat[0:gw, gw:2 * gw], 0.0).astype(BF16)
        arb = jnp.where(rr >= cc, gmat[gw:2 * gw, 0:gw], 0.0).astype(BF16)
        ark = jnp.where(rr >= cc, gmat[gw:2 * gw, gw:2 * gw], 0.0).astype(BF16)
        pw = low
        tinv = jnp.where(rr == cc, 1.0, low)
        for _ in range(5):
            pwb = pw.astype(BF16)
            pw = _dot(pwb, pwb)
            tinv = tinv + _dot(tinv.astype(BF16), pw.astype(BF16))
        parts.append((ast, rst, aak, arb, ark, tinv.astype(BF16), stack(b_h, s), stack(k_h, s), stack(vb, s)))

    zt = zt_sc[...]
    ys = []
    for s in range(ns):
        ast, rst, aak, arb, ark, tinv, bhs, khs, vst = parts[s]
        ztb = zt.astype(BF16)
        u = _dot(tinv, (_dot_nt(ast, ztb) + _dot(aak, vst)).astype(BF16)).astype(BF16)
        ybd = _dot_nt(rst, ztb) + _dot(arb, u) + _dot(ark, vst)
        zt = zt * g_c[s * n:s * n + 1, :] + _dot_tn(u, bhs) + _dot_tn(vst, khs)
        ys.append(ybd[0:n] + ybd[n:2 * n] + ybd[2 * n:3 * n] + ybd[3 * n:4 * n])
    zt_sc[...] = zt
    y = jnp.concatenate(ys, axis=0)

    mean = _dot_exact_rhs(y, head_ones) * (1.0 / HEAD_DIM)
    yc = y - mean
    var = _dot_exact_rhs(yc * yc, head_ones) * (1.0 / HEAD_DIM)
    yn = yc * lax.rsqrt(var + 64e-5) * lg_ref[...] + lb_ref[...]
    bonus = _dot_exact_rhs(r * k2 * rk_ref[...], head_ones) * v
    o_ref[0] = ((yn + bonus) * g).astype(BF16)


def _rwkv(uc, mu, w0, w2, a0, a2, g2, k_k, k_a, r_k, lnx_g, lnx_b):
    bsz, lp, w = uc.shape
    rstep = _pick_tile(lp, RWKV_STEPS)
    nc = lp // rstep
    vec = lambda t: t.reshape(1, -1)
    full = lambda t: pl.BlockSpec(t.shape, lambda b, c: (0,) * t.ndim)
    args = [vec(mu), vec(w0), w2, vec(a0), a2, g2, vec(k_k), vec(k_a), vec(r_k), vec(lnx_g), vec(lnx_b)]
    return pl.pallas_call(
        _rwkv_kernel,
        grid=(bsz, nc),
        in_specs=[
            pl.BlockSpec((1, rstep, w), lambda b, c: (b, c, 0)),
            pl.BlockSpec((1, 8, w), lambda b, c: (b, jnp.maximum(c * (rstep // 8) - 1, 0), 0)),
        ] + [full(t) for t in args],
        out_specs=pl.BlockSpec((1, rstep, GROUP_WIDTH), lambda b, c: (b, c, 0)),
        out_shape=jax.ShapeDtypeStruct((bsz, lp, GROUP_WIDTH), BF16),
        scratch_shapes=[pltpu.VMEM((GROUP_WIDTH, GROUP_WIDTH), F32)],
        compiler_params=_params("arbitrary", "arbitrary"),
        name="rwkv7",
    )(uc, uc, *args)


def _conv_kernel(cur_ref, prev_ref, w_ref, b_ref, g_ref, gb_ref, o_ref, hcat_sc):
    j = pl.program_id(1)
    ch = GROUP_WIDTH

    def glu(t):
        return t[:, 0:ch] * _sigmoid(t[:, ch:2 * ch])

    hcat_sc[0:CONV_HIST, :] = jnp.where(j == 0, 0.0, glu(prev_ref[0]))
    hcat_sc[CONV_HIST:CONV_HIST + TQ, :] = glu(cur_ref[0])
    lead = CONV_HIST - (CONV_WIDTH - 1)
    acc = jnp.zeros((TQ, ch), F32) + b_ref[...]
    for ph in range(8):
        offs = [o for o in range(lead, CONV_HIST + 1) if o % 8 == ph]
        shifted = hcat_sc[ph:max(offs) + TQ, :]
        for o in offs:
            acc = acc + shifted[o - ph:o - ph + TQ] * w_ref[o - lead:o - lead + 1, :]
    lane_grp = lax.broadcasted_iota(jnp.int32, (1, ch), 1) // HEAD_DIM
    sub_grp = lax.broadcasted_iota(jnp.int32, (ch, 1), 0) // HEAD_DIM
    grp_ones = (sub_grp == lane_grp).astype(BF16)
    mean = _dot_exact_rhs(acc, grp_ones) * (1.0 / HEAD_DIM)
    xc = acc - mean
    var = _dot_exact_rhs(xc * xc, grp_ones) * (1.0 / HEAD_DIM)
    y = xc * lax.rsqrt(var + 1e-5) * g_ref[...] + gb_ref[...]
    o_ref[0] = (y * _sigmoid(y)).astype(BF16)


def _conv(ud, conv_w, conv_b, gn_g, gn_b):
    bsz, lp, w = ud.shape
    nb = lp // TQ
    vec = lambda t: t.reshape(1, -1)
    const = lambda shape: pl.BlockSpec(shape, lambda i, j: (0, 0))
    return pl.pallas_call(
        _conv_kernel,
        grid=(bsz, nb),
        in_specs=[
            pl.BlockSpec((1, TQ, w), lambda i, j: (i, j, 0)),
            pl.BlockSpec((1, CONV_HIST, w), lambda i, j: (i, jnp.maximum(j * (TQ // CONV_HIST) - 1, 0), 0)),
            const((CONV_WIDTH, GROUP_WIDTH)), const((1, GROUP_WIDTH)), const((1, GROUP_WIDTH)),
            const((1, GROUP_WIDTH)),
        ],
        out_specs=pl.BlockSpec((1, TQ, GROUP_WIDTH), lambda i, j: (i, j, 0)),
        out_shape=jax.ShapeDtypeStruct((bsz, lp, GROUP_WIDTH), BF16),
        scratch_shapes=[pltpu.VMEM((CONV_HIST + TQ, GROUP_WIDTH), F32)],
        compiler_params=_params("parallel", "arbitrary"),
        name="conv_module",
    )(ud, ud, conv_w, vec(conv_b), vec(gn_g), vec(gn_b))


def _router(logit_t, rb_ref):
    s = [_sigmoid(logit_t[e:e + 1, :]) for e in range(N_EXPERTS)]
    bz = [s[e] + rb_ref[e:e + 1, 0:1] for e in range(N_EXPERTS)]
    per = N_EXPERTS // N_GROUPS
    gsum = []
    for gi in range(N_GROUPS):
        a, b, c, d = bz[per * gi:per * gi + per]
        hi1, lo1, hi2, lo2 = jnp.maximum(a, b), jnp.minimum(a, b), jnp.maximum(c, d), jnp.minimum(c, d)
        gsum.append(jnp.maximum(hi1, hi2) + jnp.maximum(jnp.minimum(hi1, hi2), jnp.maximum(lo1, lo2)))
    best = jnp.zeros_like(gsum[0], dtype=jnp.int32)
    bval = gsum[0]
    for gi in range(1, N_GROUPS):
        take = gsum[gi] > bval
        best = jnp.where(take, gi, best)
        bval = jnp.where(take, gsum[gi], bval)

    def pick(vals, i):
        out = vals[i]
        for gi in range(1, N_GROUPS):
            out = jnp.where(best == gi, vals[per * gi + i], out)
        return out

    bv = [pick(bz, i) for i in range(per)]
    sv = [pick(s, i) for i in range(per)]
    i1 = jnp.zeros_like(best)
    v1 = bv[0]
    for i in range(1, per):
        take = bv[i] > v1
        i1 = jnp.where(take, i, i1)
        v1 = jnp.where(take, bv[i], v1)
    i2 = jnp.full_like(best, -1)
    v2 = jnp.full_like(v1, -jnp.inf)
    for i in range(per):
        take = (i1 != i) & (bv[i] > v2)
        i2 = jnp.where(take, i, i2)
        v2 = jnp.where(take, bv[i], v2)
    s1 = sv[0]
    s2 = sv[0]
    for i in range(1, per):
        s1 = jnp.where(i1 == i, sv[i], s1)
        s2 = jnp.where(i2 == i, sv[i], s2)
    tot = s1 + s2
    g1, g2 = s1 / tot, s2 / tot
    rows = []
    for e in range(N_EXPERTS):
        gi, i = divmod(e, per)
        hit = jnp.where(best == gi, jnp.where(i1 == i, g1, jnp.where(i2 == i, g2, 0.0)), 0.0)
        rows.append(hit)
    return jnp.concatenate(rows, axis=0)


def _outproj_kernel(ya_ref, yb_ref, yc_ref, yd_ref, w_ref, h_ref, g_ref, b_ref, rwt_ref, rb_ref, o_ref, comb_ref):
    gw = GROUP_WIDTH
    mix = (_dot(ya_ref[0], w_ref[0:gw, :]) + _dot(yb_ref[0], w_ref[gw:2 * gw, :])
           + _dot(yc_ref[0], w_ref[2 * gw:3 * gw, :]) + _dot(yd_ref[0], w_ref[3 * gw:4 * gw, :]))
    h1 = _ln(ALPHA * h_ref[...] + mix, g_ref[...], b_ref[...])
    o_ref[...] = h1
    wh, wl = _split2(rwt_ref[...])
    hh, hl = _split2(h1)
    logit_t = _dot_nt(wh, hh) + _dot_nt(wh, hl) + _dot_nt(wl, hh)
    comb_ref[...] = _router(logit_t, rb_ref)


def _outproj(ya, yb, yc, yd, w_bf, h, g, b, rwt, rb_b):
    bsz, lp, _ = ya.shape
    d = h.shape[1]
    tm = TQ
    nt = lp // tm
    spec = lambda w: pl.BlockSpec((1, tm, w), lambda i, j: (i, j, 0))
    flat = lambda w: pl.BlockSpec((tm, w), lambda i, j: (i * nt + j, 0))
    const = lambda shape: pl.BlockSpec(shape, lambda i, j: (0, 0))
    return pl.pallas_call(
        _outproj_kernel,
        grid=(bsz, nt),
        in_specs=[spec(GROUP_WIDTH)] * 4 + [const((d, d)), flat(d), const((1, d)), const((1, d)),
                                            const((N_EXPERTS, d)), const((N_EXPERTS, BLK))],
        out_specs=[flat(d), pl.BlockSpec((N_EXPERTS, tm), lambda i, j: (0, i * nt + j))],
        out_shape=[jax.ShapeDtypeStruct((bsz * lp, d), F32), jax.ShapeDtypeStruct((N_EXPERTS, bsz * lp), F32)],
        compiler_params=_params("parallel", "parallel"),
        name="out_proj_ln_router",
    )(ya, yb, yc, yd, w_bf, h, g.reshape(1, d), b.reshape(1, d), rwt, rb_b)


def _route_meta(comb, tm, lp, nreal):
    t = comb.shape[1]
    per = N_EXPERTS // N_GROUPS
    gsum = comb.reshape(N_GROUPS, per, t).sum(1)
    grp = jnp.argmax(gsum, axis=0).astype(jnp.int32)
    c4 = jnp.sum(jnp.where(jnp.arange(N_GROUPS)[:, None, None] == grp[None, None, :],
                           comb.reshape(N_GROUPS, per, t), 0.0), axis=0)
    mem = jnp.arange(per, dtype=jnp.int32)[:, None]
    lo = jnp.min(jnp.where(c4 > 0, mem, per), axis=0)
    hi = jnp.max(jnp.where(c4 > 0, mem, -1), axis=0)
    lo, hi = jnp.where(hi < 0, 0, lo), jnp.where(hi < 0, per - 1, hi)
    one = lo == hi
    lo, hi = jnp.where(one & (lo == per - 1), per - 2, lo), jnp.where(one, jnp.minimum(lo + 1, per - 1), hi)
    w_lo = jnp.sum(jnp.where(mem == lo[None, :], c4, 0.0), axis=0)
    w_hi = jnp.sum(jnp.where(mem == hi[None, :], c4, 0.0), axis=0)
    pair_base = lambda a: a * (2 * per - 1 - a) // 2
    ncls = N_GROUPS * MOE_PAIRS
    cls = jnp.where(jnp.arange(t) % lp < nreal, grp * MOE_PAIRS + pair_base(lo) + hi - lo - 1, ncls)
    order = jnp.argsort(cls, stable=True).astype(jnp.int32)
    counts = jnp.sum((cls[:, None] == jnp.arange(ncls)[None, :]).astype(jnp.int32), axis=0)
    start = jnp.cumsum(counts) - counts
    pcounts = (counts + tm - 1) // tm * tm
    pend = jnp.cumsum(pcounts)
    nt = pl.cdiv(t // lp * nreal, tm) + ncls
    s = jnp.arange(nt * tm, dtype=jnp.int32)
    past = s[:, None] >= pend[None, :]
    gs = jnp.minimum(jnp.sum(past.astype(jnp.int32), axis=1), ncls - 1)
    skipped = jnp.sum(jnp.where(past, (pcounts - counts)[None, :], 0), axis=1)
    in_cls = s - jnp.sum(jnp.where(past, pcounts[None, :], 0), axis=1)
    valid = in_cls < jnp.sum(jnp.where(gs[:, None] == jnp.arange(ncls)[None, :], counts[None, :], 0), axis=1)
    valid = valid & (s < pend[ncls - 1])
    tok = order[jnp.clip(s - skipped, 0, t - 1)]
    src = jnp.where(valid, tok, -1)
    c2 = jnp.where(valid[:, None], jnp.stack([w_lo[tok], w_hi[tok]], axis=-1), 0.0)
    n_tiles = (pend[-1] // tm).astype(jnp.int32)
    tile_first = jnp.minimum(jnp.arange(nt, dtype=jnp.int32), n_tiles - 1) * tm
    tcls = jnp.minimum(jnp.sum((tile_first[:, None] >= pend[None, :]).astype(jnp.int32), axis=1), ncls - 1)
    pid = tcls % MOE_PAIRS
    p_lo = (pid >= pair_base(1)).astype(jnp.int32) + (pid >= pair_base(2)).astype(jnp.int32)
    p_hi = pid - pair_base(p_lo) + p_lo + 1
    e_lo = (tcls // MOE_PAIRS * per + p_lo).astype(jnp.int32)
    e_hi = (tcls // MOE_PAIRS * per + p_hi).astype(jnp.int32)
    return src, e_lo, e_hi, n_tiles.reshape(1), c2, nt


def _moe_kernel(src_ref, elo_ref, ehi_ref, nt_ref, h_hbm, c_ref, w1a_ref, w3a_ref, w2a_ref, w1b_ref, w3b_ref, w2b_ref,
                g_ref, b_ref, o_hbm, xbuf, obuf, gsem, ssem, *, tm, bsz, lp, nreal):
    i = pl.program_id(0)
    nt = nt_ref[0]
    slot = lax.rem(i, 2)
    npad = lp - nreal

    def pad_row(q):
        return (q // npad) * lp + nreal + q % npad

    def gather(tile, sl):
        base = tile * tm
        for r in range(tm):
            tok = jnp.maximum(src_ref[base + r], 0)
            pltpu.make_async_copy(h_hbm.at[pl.ds(tok, 1)], xbuf.at[sl, pl.ds(r, 1)], gsem.at[sl]).start()

    def scatter(sl):
        base = i * tm
        for r in range(tm):
            s = src_ref[base + r]
            dst = jnp.where(s >= 0, s, pad_row(sl * tm + r))
            pltpu.make_async_copy(obuf.at[sl, pl.ds(r, 1)], o_hbm.at[pl.ds(dst, 1)], ssem.at[sl]).start()

    def wait_gather(sl):
        pltpu.make_async_copy(h_hbm.at[pl.ds(0, tm)], xbuf.at[sl], gsem.at[sl]).wait()

    def wait_scatter(sl):
        pltpu.make_async_copy(obuf.at[sl], o_hbm.at[pl.ds(0, tm)], ssem.at[sl]).wait()

    @pl.when(i == 0)
    def _():
        gather(0, 0)
        obuf[0, 0:npad, :] = jnp.zeros((npad, obuf.shape[2]), F32)
        for bi in range(bsz):
            fill = pltpu.make_async_copy(obuf.at[0, pl.ds(0, npad)], o_hbm.at[pl.ds(bi * lp + nreal, npad)],
                                         ssem.at[0])
            fill.start()
            fill.wait()

    for sl in range(2):
        @pl.when((i + 1 < nt) & (slot == 1 - sl))
        def _(sl=sl):
            gather(i + 1, sl)

    @pl.when(i < nt)
    def _():
        wait_gather(slot)

        @pl.when(i >= 2)
        def _():
            wait_scatter(slot)

        x = xbuf[slot]
        xb = x.astype(BF16)
        c = c_ref[...]
        y = jnp.zeros(x.shape, F32)
        for e, (w1_ref, w3_ref, w2_ref) in enumerate(((w1a_ref, w3a_ref, w2a_ref), (w1b_ref, w3b_ref, w2b_ref))):
            a1 = _dot(xb, w1_ref[0])
            act = (a1 * _sigmoid(a1)) * _dot(xb, w3_ref[0]) * c[:, e:e + 1]
            y = y + _dot(act.astype(BF16), w2_ref[0])
        obuf[slot] = _ln(ALPHA * x + y, g_ref[...], b_ref[...])

        for sl in range(2):
            @pl.when(slot == sl)
            def _(sl=sl):
                scatter(sl)

    @pl.when(i == nt - 1)
    def _():
        wait_scatter(slot)

        @pl.when(nt >= 2)
        def _():
            wait_scatter(1 - slot)


def _moe(h1, comb, w1, w3, w2, g, b, bsz, lp, nreal):
    t, d = h1.shape
    tm = MOE_TM
    assert bsz * (lp - nreal) >= 2 * tm and lp - nreal <= tm
    src, e_lo, e_hi, n_tiles, c2, nt = _route_meta(comb, tm, lp, nreal)
    const = lambda i, src, elo, ehi, ntl: (0, 0)
    wa = lambda a, b_: pl.BlockSpec((1, a, b_), lambda i, src, elo, ehi, ntl: (elo[i], 0, 0))
    wb = lambda a, b_: pl.BlockSpec((1, a, b_), lambda i, src, elo, ehi, ntl: (ehi[i], 0, 0))
    grid_spec = pltpu.PrefetchScalarGridSpec(
        num_scalar_prefetch=4,
        grid=(nt,),
        in_specs=[
            pl.BlockSpec(memory_space=pl.ANY),
            pl.BlockSpec((tm, 2), lambda i, src, elo, ehi, ntl: (i, 0)),
            wa(d, EXPERT_FF), wa(d, EXPERT_FF), wa(EXPERT_FF, d),
            wb(d, EXPERT_FF), wb(d, EXPERT_FF), wb(EXPERT_FF, d),
            pl.BlockSpec((1, d), const),
            pl.BlockSpec((1, d), const),
        ],
        out_specs=pl.BlockSpec(memory_space=pl.ANY),
        scratch_shapes=[
            pltpu.VMEM((2, tm, d), F32), pltpu.VMEM((2, tm, d), F32),
            pltpu.SemaphoreType.DMA((2,)), pltpu.SemaphoreType.DMA((2,)),
        ],
    )
    return pl.pallas_call(
        functools.partial(_moe_kernel, tm=tm, bsz=bsz, lp=lp, nreal=nreal),
        grid_spec=grid_spec,
        out_shape=jax.ShapeDtypeStruct((t, d), F32),
        compiler_params=_params("arbitrary"),
        name="moe_ln",
    )(src, e_lo, e_hi, n_tiles, h1, c2, w1, w3, w2, w1, w3, w2, g.reshape(1, d), b.reshape(1, d))


def kernel(x, meta, ln0_g, ln0_b, w_in, swa_sinks, rel_bias, diff_lq1, diff_lk1, diff_lq2, diff_lk2, diff_subln_g,
           rwkv_mu, rwkv_w0, rwkv_w2, rwkv_a0, rwkv_a2, rwkv_g2, rwkv_kk, rwkv_ka, rwkv_rk, rwkv_lnx_g,
           rwkv_lnx_b, conv_w, conv_b, conv_gn_g, conv_gn_b, w_out, ln1_g, ln1_b, router_w, router_b, exp_w1,
           exp_w3, exp_w2, ln2_g, ln2_b):
    bsz, seq, d = x.shape
    assert d == D_MODEL and seq % BLK == 0
    lp = pl.cdiv(seq + N_META, TQ) * TQ

    r = jnp.arange(BLK)
    bkt_band = _t5_bucket(r[None, :] - jnp.arange(2 * BLK)[:, None] + BLK)
    qpos3 = jnp.arange(3 * BLK).reshape(3, BLK)
    bkt_meta = _t5_bucket(qpos3[:, None, :] - jnp.arange(N_META)[None, :, None])
    dq0 = r[None, :] - r[:, None]
    bkt_d = jnp.stack([_t5_bucket(dq0), _t5_bucket(dq0 + BLK)])
    rel_a, rel_b = rel_bias[:, :N_HEADS], rel_bias[:, N_HEADS:]
    rwt = router_w.T
    rb_b = jnp.broadcast_to(router_b.astype(F32)[:, None], (N_EXPERTS, BLK))
    vcols = slice(SWA_IN + 512, SWA_IN + 768)

    h = _prep(x, meta, ln0_g, ln0_b, lp).reshape(bsz * lp, d)
    for l in range(DEPTH):
        w_l = w_in[l]
        ua, qd, kd, vdt, uc, ud = _inproj(h, w_l.astype(BF16), w_l[:, vcols].T.astype(BF16), bsz, lp)
        ya = _swa(ua, rel_a, swa_sinks[l], bkt_band, bkt_meta)
        lam_init = 0.8 - 0.6 * math.exp(-0.3 * l)
        lam = (jnp.exp(jnp.sum(diff_lq1[l] * diff_lk1[l])) - jnp.exp(jnp.sum(diff_lq2[l] * diff_lk2[l])) + lam_init)
        lam2 = jnp.stack([lam, jnp.asarray(1.0 - lam_init, F32)]).astype(F32)
        yb = _diff(qd, kd, vdt, lam2, rel_b, bkt_d, diff_subln_g[l].reshape(HEAD_DIM, 1))
        yc = _rwkv(uc, rwkv_mu[l], rwkv_w0[l], rwkv_w2[l], rwkv_a0[l], rwkv_a2[l], rwkv_g2[l], rwkv_kk[l],
                   rwkv_ka[l], rwkv_rk[l], rwkv_lnx_g[l], rwkv_lnx_b[l])
        yd = _conv(ud, conv_w[l], conv_b[l], conv_gn_g[l], conv_gn_b[l])
        h1, comb = _outproj(ya, yb, yc, yd, w_out[l].astype(BF16), h, ln1_g[l], ln1_b[l], rwt, rb_b)
        h = _moe(h1, comb, exp_w1[l].astype(BF16), exp_w3[l].astype(BF16), exp_w2[l].astype(BF16),
                 ln2_g[l], ln2_b[l], bsz, lp, N_META + seq)
    return h.reshape(bsz, lp, d)[:, N_META:N_META + seq]
```
